```python
import math
import jax, jax.numpy as jnp
from jax import lax
import numpy as np

D_MODEL = 1024
BATCH = 8
SEQ = 2048
DEPTH = 1
DEC_BATCH = 8
DEC_SEQ = 64
PAST_LEN = 2048

CHUNK = 64
Q_BLOCK = 128
A_HEADS = 8
A_DH = 64
A_DV = 2 * A_DH
B_HEADS = 16
B_DH = 64
ROPE_THETA = 500000.0
ROPE_DIMS = A_DH // 4
N_GROUPS = 4
EXP_PER_GROUP = 8
N_EXPERTS = N_GROUPS * EXP_PER_GROUP
TOP_K = 2
D_EXPERT = 256
EPS = 1e-6
NEG_INF = -1e30
A_QW = 2 * A_HEADS * A_DH
A_KW = 2 * A_HEADS * A_DH
A_VW = A_HEADS * A_DV
B_W = B_HEADS * B_DH
IN_COLS = A_QW + A_KW + A_VW + 3 * B_W + 2 * D_MODEL

kernel_name = 'hybrid_diffattn_stickbreak_hmoe_stream_step'


def _rmsnorm(x, g):
    xf = x.astype(jnp.float32)
    y = xf * lax.rsqrt(jnp.mean(xf * xf, axis=-1, keepdims=True) + EPS)
    return (y * g.astype(jnp.float32)).astype(x.dtype)


def _partial_rope(x, pos):
    half = ROPE_DIMS // 2
    inv_freq = jnp.exp(-math.log(ROPE_THETA) * 2.0 * jnp.arange(half, dtype=jnp.float32) / ROPE_DIMS)
    ang = pos.astype(jnp.float32)[:, None] * inv_freq[None, :]
    cos = jnp.cos(ang)[None, :, None, :]
    sin = jnp.sin(ang)[None, :, None, :]
    xf = x.astype(jnp.float32)
    x1 = xf[..., :half]
    x2 = xf[..., half:ROPE_DIMS]
    out = jnp.concatenate([x1 * cos - x2 * sin, x2 * cos + x1 * sin, xf[..., ROPE_DIMS:]], axis=-1)
    return out.astype(x.dtype)


def _diff_attention(q, k, v, q_off, lam, lam_init, subln_g):
    bsz, tq = q.shape[0], q.shape[1]
    tk = k.shape[1]
    qb = Q_BLOCK if tq % Q_BLOCK == 0 else tq
    qf = q.astype(jnp.float32) * (1.0 / math.sqrt(A_DH))
    kf = k.astype(jnp.float32)
    vf = v.astype(jnp.float32)
    outs = []
    for j in range(tq // qb):
        t0 = q_off + j * qb
        kend = min(tk, ((t0 + qb - 1) // CHUNK + 1) * CHUNK)
        s = jnp.einsum('bqhd,bkhd->bhqk', qf[:, j * qb:(j + 1) * qb], kf[:, :kend])
        qpos = t0 + jnp.arange(qb)
        kpos = jnp.arange(kend)
        mask = (kpos[None, :] // CHUNK) <= (qpos[:, None] // CHUNK)
        p = jax.nn.softmax(jnp.where(mask, s, NEG_INF), axis=-1)
        p = p.reshape(bsz, A_HEADS, 2, qb, kend)
        a = p[:, :, 0] - lam * p[:, :, 1]
        outs.append(jnp.einsum('bhqk,bkhd->bqhd', a, vf[:, :kend]))
    o = jnp.concatenate(outs, axis=1)
    return _rmsnorm(o, subln_g) * (1.0 - lam_init)


def _stick_breaking(q, k, v, q_off):
    tq = q.shape[1]
    tk = k.shape[1]
    qb = Q_BLOCK if tq % Q_BLOCK == 0 else tq
    qf = q.astype(jnp.float32) * (1.0 / math.sqrt(B_DH))
    kf = k.astype(jnp.float32)
    vf = v.astype(jnp.float32)
    outs = []
    for j in range(tq // qb):
        t0 = q_off + j * qb
        kend = max(1, min(tk, t0 + qb - 1))
        z = jnp.einsum('bqhd,bkhd->bhqk', qf[:, j * qb:(j + 1) * qb], kf[:, :kend])
        qpos = t0 + jnp.arange(qb)
        kpos = jnp.arange(kend)
        mask = kpos[None, :] < qpos[:, None]
        log_beta = jax.nn.log_sigmoid(z)
        log_rest = jnp.where(mask, jax.nn.log_sigmoid(-z), 0.0)
        after = lax.cumsum(log_rest, axis=3, reverse=True) - log_rest
        w = jnp.where(mask, jnp.exp(log_beta + after), 0.0)
        outs.append(jnp.einsum('bhqk,bkhd->bqhd', w, vf[:, :kend]))
    return jnp.concatenate(outs, axis=1)


def _hier_moe(h, w_rg, b_rg, w_re, b_re, w1, w3, w2):
    bsz, t, d = h.shape
    hf = h.reshape(bsz * t, d)
    n = hf.shape[0]
    rows = jnp.arange(n)
    pg = jax.nn.softmax((hf @ w_rg + b_rg).astype(jnp.float32), axis=-1)
    g_sel = jnp.argmax(pg, axis=-1)
    pg_sel = pg[rows, g_sel][:, None]
    le = (hf @ w_re + b_re).astype(jnp.float32).reshape(n, N_GROUPS, EXP_PER_GROUP)
    pe = jax.nn.softmax(le[rows, g_sel], axis=-1)
    top_w, top_i = lax.top_k(pe, TOP_K)
    top_w = top_w / jnp.sum(top_w, axis=-1, keepdims=True) * pg_sel
    eid = g_sel[:, None] * EXP_PER_GROUP + top_i
    gate = jnp.sum(jax.nn.one_hot(eid, N_EXPERTS, dtype=jnp.float32) * top_w[..., None], axis=1)
    hid = jax.nn.silu(jnp.einsum('nd,edf->nef', hf, w1)) * jnp.einsum('nd,edf->nef', hf, w3)
    hid = hid * gate[..., None].astype(hid.dtype)
    y = jnp.einsum('nef,efd->nd', hid, w2)
    return y.reshape(bsz, t, d)


def _layer(x, c, past, lam_init, norm1_g, norm2_g, w_ada, b_ada, w_in, a_qnorm_g, a_knorm_g,
           a_lam_q1, a_lam_k1, a_lam_q2, a_lam_k2, a_subln_g, w_oa, w_ob, w_out,
           w_rg, b_rg, w_re, b_re, w1, w3, w2):
    bsz, t, _ = x.shape
    q_off = 0 if past is None else past[0].shape[1]
    pos = q_off + jnp.arange(t)
    mod = jax.nn.silu(c) @ w_ada + b_ada
    sh1, sc1, g1, sh2, sc2, g2 = [m[:, None, :] for m in jnp.split(mod, 6, axis=-1)]
    h = _rmsnorm(x, norm1_g) * (1.0 + sc1) + sh1
    proj = h @ w_in
    o1 = A_QW
    o2 = o1 + A_KW
    o3 = o2 + A_VW
    o4 = o3 + B_W
    o5 = o4 + B_W
    o6 = o5 + B_W
    o7 = o6 + D_MODEL
    qa = proj[..., :o1].reshape(bsz, t, 2 * A_HEADS, A_DH)
    ka = proj[..., o1:o2].reshape(bsz, t, 2 * A_HEADS, A_DH)
    va = proj[..., o2:o3].reshape(bsz, t, A_HEADS, A_DV)
    qs = proj[..., o3:o4].reshape(bsz, t, B_HEADS, B_DH)
    ks = proj[..., o4:o5].reshape(bsz, t, B_HEADS, B_DH)
    vs = proj[..., o5:o6].reshape(bsz, t, B_HEADS, B_DH)
    gate_a = jax.nn.sigmoid(proj[..., o6:o7])
    gate_b = jax.nn.sigmoid(proj[..., o7:])
    qa = _partial_rope(_rmsnorm(qa, a_qnorm_g), pos)
    ka = _partial_rope(_rmsnorm(ka, a_knorm_g), pos)
    if past is None:
        ka_all, va_all, ks_all, vs_all = ka, va, ks, vs
    else:
        ka_all = jnp.concatenate([past[0].astype(ka.dtype), ka], axis=1)
        va_all = jnp.concatenate([past[1].astype(va.dtype), va], axis=1)
        ks_all = jnp.concatenate([past[2].astype(ks.dtype), ks], axis=1)
        vs_all = jnp.concatenate([past[3].astype(vs.dtype), vs], axis=1)
    lam = (jnp.exp(jnp.sum(a_lam_q1.astype(jnp.float32) * a_lam_k1.astype(jnp.float32)))
           - jnp.exp(jnp.sum(a_lam_q2.astype(jnp.float32) * a_lam_k2.astype(jnp.float32))) + lam_init)
    oa = _diff_attention(qa, ka_all, va_all, q_off, lam, lam_init, a_subln_g).astype(x.dtype)
    ob = _stick_breaking(qs, ks_all, vs_all, q_off).astype(x.dtype)
    mix = gate_a * (oa.reshape(bsz, t, A_VW) @ w_oa) + gate_b * (ob.reshape(bsz, t, B_W) @ w_ob)
    x = x + g1 * (mix @ w_out)
    h2 = _rmsnorm(x, norm2_g) * (1.0 + sc2) + sh2
    x = x + g2 * _hier_moe(h2, w_rg, b_rg, w_re, b_re, w1, w3, w2)
    return x, (ka, va, ks, vs)


def setup_inputs(seed: int = 0) -> dict:
    key = jax.random.key(seed)
    ks = jax.random.split(key, 32)
    f32 = jnp.float32

    def nrm(k, shape, scale):
        return jax.random.normal(k, shape, dtype=f32) * scale

    D = D_MODEL
    return {
        'x_prompt': nrm(ks[0], (BATCH, SEQ, D), 1.0),
        'x_sample': nrm(ks[1], (DEC_BATCH, DEC_SEQ, D), 1.0),
        'cache_a_k': nrm(ks[2], (DEPTH, DEC_BATCH, PAST_LEN, 2 * A_HEADS, A_DH), 1.0),
        'cache_a_v': nrm(ks[3], (DEPTH, DEC_BATCH, PAST_LEN, A_HEADS, A_DV), 1.0),
        'cache_b_k': nrm(ks[4], (DEPTH, DEC_BATCH, PAST_LEN, B_HEADS, B_DH), 1.0),
        'cache_b_v': nrm(ks[5], (DEPTH, DEC_BATCH, PAST_LEN, B_HEADS, B_DH), 1.0),
        'c_prompt': nrm(ks[6], (BATCH, D), 1.0),
        'c_sample': nrm(ks[7], (DEC_BATCH, D), 1.0),
        'norm1_g': 1.0 + nrm(ks[8], (DEPTH, D), 0.02),
        'norm2_g': 1.0 + nrm(ks[9], (DEPTH, D), 0.02),
        'w_ada': nrm(ks[10], (DEPTH, D, 6 * D), 0.5 * D ** -0.5),
        'b_ada': nrm(ks[11], (DEPTH, 6 * D), 0.02),
        'w_in': nrm(ks[12], (DEPTH, D, IN_COLS), D ** -0.5),
        'a_qnorm_g': 1.0 + nrm(ks[13], (DEPTH, A_DH), 0.02),
        'a_knorm_g': 1.0 + nrm(ks[14], (DEPTH, A_DH), 0.02),
        'a_lam_q1': nrm(ks[15], (DEPTH, A_DH), 0.1),
        'a_lam_k1': nrm(ks[16], (DEPTH, A_DH), 0.1),
        'a_lam_q2': nrm(ks[17], (DEPTH, A_DH), 0.1),
        'a_lam_k2': nrm(ks[18], (DEPTH, A_DH), 0.1),
        'a_subln_g': 1.0 + nrm(ks[19], (DEPTH, A_DV), 0.02),
        'w_oa': nrm(ks[20], (DEPTH, A_VW, D), A_VW ** -0.5),
        'w_ob': nrm(ks[21], (DEPTH, B_W, D), B_W ** -0.5),
        'w_out': nrm(ks[22], (DEPTH, D, D), D ** -0.5),
        'w_rg': nrm(ks[23], (DEPTH, D, N_GROUPS), D ** -0.5),
        'b_rg': nrm(ks[24], (DEPTH, N_GROUPS), 0.01),
        'w_re': nrm(ks[25], (DEPTH, D, N_EXPERTS), D ** -0.5),
        'b_re': nrm(ks[26], (DEPTH, N_EXPERTS), 0.01),
        'w1': nrm(ks[27], (DEPTH, N_EXPERTS, D, D_EXPERT), D ** -0.5),
        'w3': nrm(ks[28], (DEPTH, N_EXPERTS, D, D_EXPERT), D ** -0.5),
        'w2': nrm(ks[29], (DEPTH, N_EXPERTS, D_EXPERT, D), D_EXPERT ** -0.5),
    }


def reference(x_prompt, x_sample, cache_a_k, cache_a_v, cache_b_k, cache_b_v, c_prompt, c_sample,
              norm1_g, norm2_g, w_ada, b_ada, w_in, a_qnorm_g, a_knorm_g, a_lam_q1, a_lam_k1,
              a_lam_q2, a_lam_k2, a_subln_g, w_oa, w_ob, w_out, w_rg, b_rg, w_re, b_re, w1, w3, w2):
    xp = x_prompt
    xs = x_sample
    rows_p = ([], [], [], [])
    rows_s = ([], [], [], [])
    for l in range(DEPTH):
        lam_init = 0.8 - 0.6 * math.exp(-0.3 * l)
        params = (norm1_g[l], norm2_g[l], w_ada[l], b_ada[l], w_in[l], a_qnorm_g[l], a_knorm_g[l],
                  a_lam_q1[l], a_lam_k1[l], a_lam_q2[l], a_lam_k2[l], a_subln_g[l], w_oa[l], w_ob[l],
                  w_out[l], w_rg[l], b_rg[l], w_re[l], b_re[l], w1[l], w3[l], w2[l])
        xp, new_p = _layer(xp, c_prompt, None, lam_init, *params)
        past = (cache_a_k[l], cache_a_v[l], cache_b_k[l], cache_b_v[l])
        xs, new_s = _layer(xs, c_sample, past, lam_init, *params)
        for lst, r in zip(rows_p, new_p):
            lst.append(r)
        for lst, r in zip(rows_s, new_s):
            lst.append(r)
    return (xp, xs,
            jnp.stack(rows_p[0]), jnp.stack(rows_p[1]), jnp.stack(rows_p[2]), jnp.stack(rows_p[3]),
            jnp.stack(rows_s[0]), jnp.stack(rows_s[1]), jnp.stack(rows_s[2]), jnp.stack(rows_s[3]))
```

```python
import functools
import math

import jax
import jax.numpy as jnp
from jax import lax
from jax.experimental import pallas as pl
from jax.experimental.pallas import tpu as pltpu

_F32 = jnp.float32
_BF16 = jnp.bfloat16

_LANES = 128
_MXU_DIM = 256
_VMEM_LIMIT_BYTES = 48 * 1024 * 1024

_CHUNK = 64
_HEAD_DIM = 64
_ROPE_DIMS = _HEAD_DIM // 4
_ROPE_THETA = 500000.0
_EXP_PER_GROUP = 8
_EPS = 1e-6
_NEG_INF = -1e30
_Q_SCALE = 1.0 / math.sqrt(_HEAD_DIM)

_TOKEN_TILE = 512
_OUTPROJ_TILE = 256
_ATTN_TILE = 256
_EXPERTS_PER_STEP = 4

_NT = (((1,), (1,)), ((), ()))


def _dot(a, b):
    return jnp.dot(a, b, preferred_element_type=_F32)


def _split(x):
    hi = x.astype(_BF16)
    lo = (x - hi.astype(_F32)).astype(_BF16)
    return hi, lo


def _dot3(a, b):
    a_hi, a_lo = _split(a)
    b_hi, b_lo = _split(b)
    return _dot(a_hi, b_hi) + (_dot(a_hi, b_lo) + _dot(a_lo, b_hi))


def _silu(x):
    return x / (1.0 + jnp.exp(-x))


def _sigmoid(x):
    return 1.0 / (1.0 + jnp.exp(-x))


def _rms(x):
    return x * lax.rsqrt(jnp.mean(x * x, axis=-1, keepdims=True) + _EPS)


def _mod_rows(mod_ref, k, d, seq, n_batch):
    if n_batch == 1:
        return mod_ref[0, :, k * d:(k + 1) * d]
    rows = [jnp.broadcast_to(mod_ref[b, :, k * d:(k + 1) * d], (seq, d)) for b in range(n_batch)]
    return jnp.concatenate(rows, axis=0)


def _lane_tile(x, n):
    if n % _LANES == 0:
        return jnp.concatenate([x] * (n // _LANES), axis=1) if n > _LANES else x
    return x[:, :n]


def _compiler_params(semantics):
    return pltpu.CompilerParams(dimension_semantics=semantics, vmem_limit_bytes=_VMEM_LIMIT_BYTES)


def _ada_kernel(c_ref, w_ref, b_ref, lam_ref, mod_ref, lam_out_ref, *, lam_init):
    mod_ref[...] = _dot3(_silu(c_ref[...]), w_ref[...]) + b_ref[...]
    lv = lam_ref[...]
    s1 = jnp.sum(lv[0:1] * lv[1:2], axis=-1, keepdims=True)
    s2 = jnp.sum(lv[2:3] * lv[3:4], axis=-1, keepdims=True)
    lam = jnp.exp(s1) - jnp.exp(s2) + lam_init
    lam_out_ref[...] = jnp.broadcast_to(lam, lam_out_ref.shape)


def _ada_call(c_all, w_ada, b_ada, lam_vecs, lam_init):
    rows, d = c_all.shape
    cols = w_ada.shape[1]
    tn = d
    return pl.pallas_call(
        functools.partial(_ada_kernel, lam_init=lam_init),
        grid=(cols // tn,),
        in_specs=[
            pl.BlockSpec((rows, d), lambda j: (0, 0)),
            pl.BlockSpec((d, tn), lambda j: (0, j)),
            pl.BlockSpec((1, tn), lambda j: (0, j)),
            pl.BlockSpec(lam_vecs.shape, lambda j: (0, 0)),
        ],
        out_specs=[
            pl.BlockSpec((rows, tn), lambda j: (0, j)),
            pl.BlockSpec((8, _LANES), lambda j: (0, 0)),
        ],
        out_shape=[
            jax.ShapeDtypeStruct((rows, cols), _F32),
            jax.ShapeDtypeStruct((8, _LANES), _F32),
        ],
        compiler_params=_compiler_params(("arbitrary",)),
        name="ada",
    )(c_all, w_ada, b_ada.reshape(1, cols), lam_vecs)


def _adaln(x, gain, scale, shift):
    return (_rms(x) * gain) * (1.0 + scale) + shift


def _proj_kernel(mod_ref, x_ref, n1_ref, w_ref, gq_ref, gk_ref, rc_ref, rlo_ref, rhi_ref,
                 qa_ref, ka_ref, va_ref, qs_ref, ks_ref, vs_ref, h_s, *, seq, n_batch):
    j = pl.program_id(1)
    d = x_ref.shape[1]

    @pl.when(j == 0)
    def _():
        shift = _mod_rows(mod_ref, 0, d, seq, n_batch)
        scale = _mod_rows(mod_ref, 1, d, seq, n_batch)
        h_s[...] = _adaln(x_ref[...], n1_ref[...], scale, shift).astype(_BF16)

    def proj():
        return _dot(h_s[...], w_ref[...])

    def head_norm_rope(gain_ref, out_ref, out_scale):
        acc = proj()
        w = _MXU_DIM
        row = lax.broadcasted_iota(jnp.int32, (w, w), 0) // _HEAD_DIM
        col = lax.broadcasted_iota(jnp.int32, (w, w), 1) // _HEAD_DIM
        head_mean = jnp.where(row == col, 1.0 / _HEAD_DIM, 0.0).astype(_BF16)
        cos, sin_lo, sin_hi = rc_ref[...], rlo_ref[...], rhi_ref[...]
        gain = gain_ref[...]
        for c in range(acc.shape[1] // w):
            a = acc[:, c * w:(c + 1) * w]
            ms = _dot((a * a).astype(_BF16), head_mean)
            y = a * lax.rsqrt(ms + _EPS)
            for s in range(w // _LANES):
                t = y[:, s * _LANES:(s + 1) * _LANES] * gain
                t = (t * cos + pltpu.roll(t, _LANES - _ROPE_DIMS // 2, 1) * sin_lo
                     + pltpu.roll(t, _ROPE_DIMS // 2, 1) * sin_hi)
                lo = c * w + s * _LANES
                out_ref[:, lo:lo + _LANES] = (t * out_scale).astype(out_ref.dtype)

    @pl.when(j == 0)
    def _():
        head_norm_rope(gq_ref, qa_ref, _Q_SCALE)

    @pl.when(j == 1)
    def _():
        head_norm_rope(gk_ref, ka_ref, 1.0)

    @pl.when(j == 2)
    def _():
        va_ref[...] = proj()

    @pl.when(j == 3)
    def _():
        qs_ref[...] = (proj() * _Q_SCALE).astype(_BF16)

    @pl.when(j == 4)
    def _():
        ks_ref[...] = proj()

    @pl.when(j == 5)
    def _():
        vs_ref[...] = proj()


def _rope_tables(pos):
    half = _ROPE_DIMS // 2
    inv_freq = jnp.exp(-math.log(_ROPE_THETA) * 2.0 * jnp.arange(half, dtype=_F32) / _ROPE_DIMS)
    ang = pos.astype(_F32)[:, None] * inv_freq[None, :]
    dim = jnp.arange(_LANES) % _HEAD_DIM
    cos = jnp.cos(ang)[:, dim % half]
    sin = jnp.sin(ang)[:, dim % half]
    first = (dim < half)[None, :]
    second = ((dim >= half) & (dim < _ROPE_DIMS))[None, :]
    return (jnp.where(first | second, cos, 1.0), jnp.where(first, -sin, 0.0), jnp.where(second, sin, 0.0))


def _token_tiling(n_tokens, seq, tile=_TOKEN_TILE):
    tm = min(tile, n_tokens)
    if tm >= seq:
        assert tm % seq == 0 and n_tokens % tm == 0
        return tm, tm // seq, 1
    assert seq % tm == 0
    return tm, 1, seq // tm


def _mod_spec(width, n_batch, tiles_per_batch, mod_off):
    assert mod_off % n_batch == 0
    first = mod_off // n_batch
    return pl.BlockSpec((n_batch, 1, width), lambda i, *_: (first + i // tiles_per_batch, 0, 0))


def _proj_call(x2, mod3, mod_off, seq, pos, norm1_g, w_in_bf, gq, gk):
    n, d = x2.shape
    tm, n_batch, tiles_per_batch = _token_tiling(n, seq)
    tables = _rope_tables(pos)
    if n_batch > 1:
        tables = tuple(jnp.tile(t, (n_batch, 1)) for t in tables)
    tile = lambda g: jnp.tile(g.reshape(1, _HEAD_DIM), (1, _LANES // _HEAD_DIM))
    row_spec = lambda: pl.BlockSpec((tm, d), lambda i, j: (i, 0))
    rope_spec = lambda: pl.BlockSpec((tm, _LANES), lambda i, j: (i % tiles_per_batch, 0))
    vec_spec = lambda w: pl.BlockSpec((1, w), lambda i, j: (0, 0))
    f32_out = jax.ShapeDtypeStruct((n, d), _F32)
    bf_out = jax.ShapeDtypeStruct((n, d), _BF16)
    return pl.pallas_call(
        functools.partial(_proj_kernel, seq=seq, n_batch=n_batch),
        grid=(n // tm, 6),
        in_specs=[
            _mod_spec(mod3.shape[2], n_batch, tiles_per_batch, mod_off),
            row_spec(),
            vec_spec(d),
            pl.BlockSpec((d, d), lambda i, j: (0, j)),
            vec_spec(_LANES), vec_spec(_LANES),
            rope_spec(), rope_spec(), rope_spec(),
        ],
        out_specs=[row_spec() for _ in range(6)],
        out_shape=[bf_out, f32_out, f32_out, bf_out, f32_out, f32_out],
        scratch_shapes=[pltpu.VMEM((tm, d), _BF16)],
        compiler_params=_compiler_params(("arbitrary", "arbitrary")),
        name="proj",
    )(mod3, x2, norm1_g.reshape(1, d), w_in_bf, tile(gq), tile(gk), *tables)


def _stacked_heads(q):
    lane = lax.broadcasted_iota(jnp.int32, q.shape, 1)
    zero = jnp.zeros_like(q)
    return jnp.concatenate([jnp.where(lane < _HEAD_DIM, q, zero), jnp.where(lane >= _HEAD_DIM, q, zero)], axis=0)


def _diag_positions(tq, tk):
    assert tq & (tq - 1) == 0
    qpos = lax.broadcasted_iota(jnp.int32, (2 * tq, tk), 0) & (tq - 1)
    kpos = lax.broadcasted_iota(jnp.int32, (2 * tq, tk), 1)
    return qpos, kpos


def _attn_a_kernel(*refs, tq, tkp, n_past, lam_init):
    if n_past:
        (lam_ref, subg_ref, q_ref, k_ref, v_ref, pk_ref, pv_ref, o_ref,
         kb_s, vb_s, pkb_s, pvb_s, m_s, l_s, acc_s) = refs
    else:
        lam_ref, subg_ref, q_ref, k_ref, v_ref, o_ref, kb_s, vb_s, m_s, l_s, acc_s = refs
    qi = pl.program_id(2)

    @pl.when(qi == 0)
    def _():
        kb_s[...] = k_ref[...].astype(_BF16)
        vb_s[...] = v_ref[...].astype(_BF16)
        if n_past:
            pkb_s[...] = pk_ref[...].astype(_BF16)
            pvb_s[...] = pv_ref[...].astype(_BF16)

    qq = _stacked_heads(q_ref[...])
    m_s[...] = jnp.full(m_s.shape, _NEG_INF, _F32)
    l_s[...] = jnp.zeros(l_s.shape, _F32)
    acc_s[...] = jnp.zeros(acc_s.shape, _F32)

    def step(kblk, vblk, mask):
        s = lax.dot_general(qq, kblk, _NT, preferred_element_type=_F32)
        if mask is not None:
            s = jnp.where(mask, s, _NEG_INF)
        tk = s.shape[1]
        m_prev = m_s[...]
        m_next = jnp.maximum(m_prev, jnp.max(s, axis=1, keepdims=True))
        alpha = jnp.exp(m_prev - m_next)
        p = jnp.exp(s - _lane_tile(m_next, tk))
        l_s[...] = alpha * l_s[...] + jnp.sum(p, axis=1, keepdims=True)
        m_s[...] = m_next
        acc_s[...] = alpha * acc_s[...] + _dot(p.astype(_BF16), vblk)

    def past_body(i, carry):
        lo = pl.multiple_of(i * tkp, tkp)
        step(pkb_s[pl.ds(lo, tkp), :], pvb_s[pl.ds(lo, tkp), :], None)
        return carry

    if n_past:
        lax.fori_loop(0, n_past, past_body, 0)

    def self_body(i, carry):
        lo = pl.multiple_of(i * tq, tq)
        step(kb_s[pl.ds(lo, tq), :], vb_s[pl.ds(lo, tq), :], None)
        return carry

    lax.fori_loop(0, qi, self_body, 0)

    lo = pl.multiple_of(qi * tq, tq)
    qpos, kpos = _diag_positions(tq, tq)
    chunk_bits = _CHUNK.bit_length() - 1
    step(kb_s[pl.ds(lo, tq), :], vb_s[pl.ds(lo, tq), :], (kpos >> chunk_bits) <= (qpos >> chunk_bits))

    out = acc_s[...] / l_s[...]
    o = out[:tq] - lam_ref[0:1, 0:1] * out[tq:]
    o_ref[...] = ((_rms(o) * subg_ref[...]) * (1.0 - lam_init)).astype(o_ref.dtype)


def _attn_b_kernel(*refs, tq, tkp, n_past):
    if n_past:
        q_ref, k_ref, v_ref, pk_ref, pv_ref, o_ref, kb_s, vb_s, pkb_s, pvb_s, c_s, acc_s = refs
    else:
        q_ref, k_ref, v_ref, o_ref, kb_s, vb_s, c_s, acc_s = refs
    qi = pl.program_id(2)

    @pl.when(qi == 0)
    def _():
        kb_s[...] = k_ref[...].astype(_BF16)
        vb_s[...] = v_ref[...].astype(_BF16)
        if n_past:
            pkb_s[...] = pk_ref[...].astype(_BF16)
            pvb_s[...] = pv_ref[...].astype(_BF16)

    qq = _stacked_heads(q_ref[...])
    c_s[...] = jnp.zeros(c_s.shape, _F32)
    acc_s[...] = jnp.zeros(acc_s.shape, _F32)

    def later_keys(tk):
        j = lax.broadcasted_iota(jnp.int32, (tk, tk), 0)
        s = lax.broadcasted_iota(jnp.int32, (tk, tk), 1)
        return jnp.where(j > s, 1.0, 0.0).astype(_BF16)

    def step(kblk, vblk, mask):
        z = lax.dot_general(qq, kblk, _NT, preferred_element_type=_F32)
        tk = z.shape[1]
        softplus = jnp.log(1.0 + jnp.exp(-jnp.abs(z)))
        log_beta = jnp.minimum(z, 0.0) - softplus
        log_rest = -jnp.maximum(z, 0.0) - softplus
        if mask is not None:
            log_rest = jnp.where(mask, log_rest, 0.0)
        hi, lo = _split(log_rest)
        later = later_keys(tk)
        after = (_dot(hi, later) + _dot(lo, later)) + _lane_tile(c_s[...], tk)
        w = jnp.exp(log_beta + after)
        if mask is not None:
            w = jnp.where(mask, w, 0.0)
        acc_s[...] += _dot(w.astype(_BF16), vblk)
        c_s[...] += jnp.sum(log_rest, axis=1, keepdims=True)

    lo = pl.multiple_of(qi * tq, tq)
    qpos, kpos = _diag_positions(tq, tq)
    step(kb_s[pl.ds(lo, tq), :], vb_s[pl.ds(lo, tq), :], kpos < qpos)

    def self_body(i, carry):
        lo = pl.multiple_of((qi - 1 - i) * tq, tq)
        step(kb_s[pl.ds(lo, tq), :], vb_s[pl.ds(lo, tq), :], None)
        return carry

    lax.fori_loop(0, qi, self_body, 0)

    def past_body(i, carry):
        lo = pl.multiple_of((n_past - 1 - i) * tkp, tkp)
        step(pkb_s[pl.ds(lo, tkp), :], pvb_s[pl.ds(lo, tkp), :], None)
        return carry

    if n_past:
        lax.fori_loop(0, n_past, past_body, 0)

    acc = acc_s[...]
    lane = lax.broadcasted_iota(jnp.int32, (tq, _LANES), 1)
    o_ref[...] = jnp.where(lane < _HEAD_DIM, acc[:tq], acc[tq:]).astype(o_ref.dtype)


def _attn_call(kind, q2, k2, v2, past, batch, seq, extra_inputs, extra_specs, **kernel_kwargs):
    n, d = q2.shape
    tq = min(_ATTN_TILE, seq)
    assert seq % tq == 0
    nq = seq // tq
    groups = d // _LANES
    q_spec = pl.BlockSpec((tq, _LANES), lambda b, g, i: (b * nq + i, g))
    kv_spec = pl.BlockSpec((seq, _LANES), lambda b, g, i: (b, g))
    inputs = list(extra_inputs) + [q2, k2, v2]
    in_specs = list(extra_specs) + [q_spec, kv_spec, kv_spec]
    scratch = [pltpu.VMEM((seq, _LANES), _BF16), pltpu.VMEM((seq, _LANES), _BF16)]
    tkp, n_past = _ATTN_TILE, 0
    if past is not None:
        pk2, pv2 = past
        past_len = pk2.shape[0] // batch
        assert past_len % tkp == 0
        n_past = past_len // tkp
        past_spec = pl.BlockSpec((past_len, _LANES), lambda b, g, i: (b, g))
        inputs += [pk2, pv2]
        in_specs += [past_spec, past_spec]
        scratch += [pltpu.VMEM((past_len, _LANES), _BF16), pltpu.VMEM((past_len, _LANES), _BF16)]
    if kind == "a":
        body = functools.partial(_attn_a_kernel, tq=tq, tkp=tkp, n_past=n_past, **kernel_kwargs)
        scratch += [pltpu.VMEM((2 * tq, _LANES), _F32)] * 3
    else:
        body = functools.partial(_attn_b_kernel, tq=tq, tkp=tkp, n_past=n_past)
        scratch += [pltpu.VMEM((2 * tq, _LANES), _F32)] * 2
    return pl.pallas_call(
        body,
        grid=(batch, groups, nq),
        in_specs=in_specs,
        out_specs=q_spec,
        out_shape=jax.ShapeDtypeStruct((n, d), _BF16),
        scratch_shapes=scratch,
        compiler_params=_compiler_params(("arbitrary", "arbitrary", "arbitrary")),
        name="attn_" + kind,
    )(*inputs)


def _router_gate(logits, n_experts, n_groups):
    lane_i = lax.broadcasted_iota(jnp.int32, logits.shape, 1)
    lane = lane_i.astype(_F32)
    group_of_lane = (lane_i >> (_EXP_PER_GROUP.bit_length() - 1)).astype(_F32)
    big = float(4 * _LANES)
    row_max = lambda v: jnp.max(v, axis=1, keepdims=True)
    row_min = lambda v: jnp.min(v, axis=1, keepdims=True)
    row_sum = lambda v: jnp.sum(v, axis=1, keepdims=True)

    is_group = (lane_i >= n_experts) & (lane_i < n_experts + n_groups)
    lg = jnp.where(is_group, logits, _NEG_INF)
    eg = jnp.where(is_group, jnp.exp(lg - row_max(lg)), 0.0)
    pg = eg / row_sum(eg)
    pg_sel = row_max(pg)
    g_sel = row_min(jnp.where(is_group & (pg == pg_sel), lane - n_experts, big))

    in_group = (lane_i < n_experts) & (group_of_lane == g_sel)
    le = jnp.where(in_group, logits, _NEG_INF)
    ee = jnp.where(in_group, jnp.exp(le - row_max(le)), 0.0)
    pe = ee / row_sum(ee)
    p1 = row_max(pe)
    i1 = row_min(jnp.where(in_group & (pe == p1), lane, big))
    rest = in_group & (lane != i1)
    p2 = row_max(jnp.where(rest, pe, -1.0))
    i2 = row_min(jnp.where(rest & (pe == p2), lane, big))
    total = p1 + p2
    return jnp.where(lane == i1, p1 / total * pg_sel, 0.0) + jnp.where(lane == i2, p2 / total * pg_sel, 0.0)


def _outproj_kernel(mod_ref, x_ref, oa_ref, ob_ref, n1_ref, n2_ref, wg_ref, woa_ref, wob_ref, wout_ref,
                    wr_hi_ref, wr_lo_ref, br_ref, x1_ref, h2_ref, gate_ref, *, seq, n_batch, n_experts, n_groups):
    d = x_ref.shape[1]
    rows = lambda k: _mod_rows(mod_ref, k, d, seq, n_batch)
    x = x_ref[...]
    h = _adaln(x, n1_ref[...], rows(1), rows(0)).astype(_BF16)
    gate_a = _sigmoid(_dot(h, wg_ref[:, :d]))
    gate_b = _sigmoid(_dot(h, wg_ref[:, d:]))
    mix = gate_a * _dot(oa_ref[...], woa_ref[...]) + gate_b * _dot(ob_ref[...], wob_ref[...])
    x1 = x + rows(2) * _dot(mix.astype(_BF16), wout_ref[...])
    x1_ref[...] = x1
    h2 = _adaln(x1, n2_ref[...], rows(4), rows(3))
    h2_ref[...] = h2.astype(_BF16)
    h2_hi, h2_lo = _split(h2)
    logits = _dot(h2_hi, wr_hi_ref[...]) + (_dot(h2_hi, wr_lo_ref[...]) + _dot(h2_lo, wr_hi_ref[...])) + br_ref[...]
    gate_ref[...] = _router_gate(logits, n_experts, n_groups)


def _outproj_call(x2, oa, ob, mod3, mod_off, seq, norm1_g, norm2_g, w_in_bf, w_oa_bf, w_ob_bf, w_out_bf,
                  w_router, b_router, n_experts, n_groups):
    n, d = x2.shape
    tm, n_batch, tiles_per_batch = _token_tiling(n, seq, _OUTPROJ_TILE)
    wr_hi, wr_lo = _split(w_router)
    gate_cols = w_in_bf.shape[1] // (2 * d) - 1
    row_spec = lambda: pl.BlockSpec((tm, d), lambda i: (i, 0))
    full = lambda a: pl.BlockSpec(a.shape, lambda i: (0,) * a.ndim)
    return pl.pallas_call(
        functools.partial(_outproj_kernel, seq=seq, n_batch=n_batch, n_experts=n_experts, n_groups=n_groups),
        grid=(n // tm,),
        in_specs=[
            _mod_spec(mod3.shape[2], n_batch, tiles_per_batch, mod_off),
            row_spec(), row_spec(), row_spec(),
            pl.BlockSpec((1, d), lambda i: (0, 0)), pl.BlockSpec((1, d), lambda i: (0, 0)),
            pl.BlockSpec((d, 2 * d), lambda i: (0, gate_cols)),
            full(w_oa_bf), full(w_ob_bf), full(w_out_bf), full(wr_hi), full(wr_lo),
            pl.BlockSpec((1, _LANES), lambda i: (0, 0)),
        ],
        out_specs=[row_spec(), row_spec(), pl.BlockSpec((tm, _LANES), lambda i: (i, 0))],
        out_shape=[
            jax.ShapeDtypeStruct((n, d), _F32),
            jax.ShapeDtypeStruct((n, d), _BF16),
            jax.ShapeDtypeStruct((n, _LANES), _F32),
        ],
        compiler_params=_compiler_params(("arbitrary",)),
        name="outproj",
    )(mod3, x2, oa, ob, norm1_g.reshape(1, d), norm2_g.reshape(1, d), w_in_bf, w_oa_bf, w_ob_bf, w_out_bf,
      wr_hi, wr_lo, b_router)


def _moe_kernel(mod_ref, x1_ref, h2_ref, gate_ref, w1_ref, w3_ref, w2_ref, y_ref, acc_s, *, seq, n_batch):
    e = pl.program_id(1)
    d = x1_ref.shape[1]
    experts_per_step = w1_ref.shape[0]

    @pl.when(e == 0)
    def _():
        acc_s[...] = jnp.zeros(acc_s.shape, _F32)

    h = h2_ref[...]
    gate = gate_ref[...]
    lane = lax.broadcasted_iota(jnp.int32, gate.shape, 1)
    for k in range(experts_per_step):
        g = jnp.sum(jnp.where(lane == e * experts_per_step + k, gate, 0.0), axis=1, keepdims=True)
        hid = _silu(_dot(h, w1_ref[k])) * _dot(h, w3_ref[k])
        acc_s[...] += _dot((hid * g).astype(_BF16), w2_ref[k])

    @pl.when(e == pl.num_programs(1) - 1)
    def _():
        y_ref[...] = x1_ref[...] + _mod_rows(mod_ref, 5, d, seq, n_batch) * acc_s[...]


def _moe_call(x1, h2, gate, mod3, mod_off, seq, w1_bf, w3_bf, w2_bf):
    n, d = x1.shape
    tm, n_batch, tiles_per_batch = _token_tiling(n, seq)
    n_experts, _, d_expert = w1_bf.shape
    eb = _EXPERTS_PER_STEP
    assert n_experts % eb == 0
    row_spec = lambda w: pl.BlockSpec((tm, w), lambda i, e: (i, 0))
    return pl.pallas_call(
        functools.partial(_moe_kernel, seq=seq, n_batch=n_batch),
        grid=(n // tm, n_experts // eb),
        in_specs=[
            _mod_spec(mod3.shape[2], n_batch, tiles_per_batch, mod_off),
            row_spec(d), row_spec(d), row_spec(_LANES),
            pl.BlockSpec((eb, d, d_expert), lambda i, e: (e, 0, 0)),
            pl.BlockSpec((eb, d, d_expert), lambda i, e: (e, 0, 0)),
            pl.BlockSpec((eb, d_expert, d), lambda i, e: (e, 0, 0)),
        ],
        out_specs=row_spec(d),
        out_shape=jax.ShapeDtypeStruct((n, d), _F32),
        scratch_shapes=[pltpu.VMEM((tm, d), _F32)],
        compiler_params=_compiler_params(("arbitrary", "arbitrary")),
        name="moe",
    )(mod3, x1, h2, gate, w1_bf, w3_bf, w2_bf)


def _layer(x, mod3, mod_off, lam, lam_init, past, p):
    batch, seq, d = x.shape
    x2 = x.reshape(batch * seq, d)
    past_len = 0 if past is None else past[0].shape[1]
    pos = past_len + jnp.arange(seq)
    qa, ka, va, qs, ks, vs = _proj_call(x2, mod3, mod_off, seq, pos, p["norm1_g"], p["w_in"], p["gq"], p["gk"])
    flat = lambda c: c.reshape(batch * past_len, d)
    past_a = None if past is None else (flat(past[0]), flat(past[1]))
    past_b = None if past is None else (flat(past[2]), flat(past[3]))
    const = lambda shape: pl.BlockSpec(shape, lambda b, g, i: (0, 0))
    oa = _attn_call("a", qa, ka, va, past_a, batch, seq, [lam, p["subln_g"]],
                    [const(lam.shape), const(p["subln_g"].shape)], lam_init=lam_init)
    ob = _attn_call("b", qs, ks, vs, past_b, batch, seq, [], [])
    x1, h2, gate = _outproj_call(x2, oa, ob, mod3, mod_off, seq, p["norm1_g"], p["norm2_g"], p["w_in"],
                                 p["w_oa"], p["w_ob"], p["w_out"], p["w_router"], p["b_router"],
                                 p["n_experts"], p["n_groups"])
    y = _moe_call(x1, h2, gate, mod3, mod_off, seq, p["w1"], p["w3"], p["w2"])
    new = (ka.reshape(batch, seq, -1, _HEAD_DIM), va.reshape(batch, seq, -1, 2 * _HEAD_DIM),
           ks.reshape(batch, seq, -1, _HEAD_DIM), vs.reshape(batch, seq, -1, _HEAD_DIM))
    return y.reshape(batch, seq, d), new


def kernel(x_prompt, x_sample, cache_a_k, cache_a_v, cache_b_k, cache_b_v, c_prompt, c_sample, norm1_g, norm2_g, w_ada, b_ada, w_in, a_qnorm_g, a_knorm_g, a_lam_q1, a_lam_k1, a_lam_q2, a_lam_k2, a_subln_g, w_oa, w_ob, w_out, w_rg, b_rg, w_re, b_re, w1, w3, w2):
    depth, d = norm1_g.shape
    n_groups, n_experts = w_rg.shape[2], w_re.shape[2]
    assert w_in.shape[2] == 8 * d and n_experts + n_groups <= _LANES
    batch_p, batch_s = x_prompt.shape[0], x_sample.shape[0]
    xp, xs = x_prompt, x_sample
    c_all = jnp.concatenate([c_prompt, c_sample], axis=0)
    rows_p, rows_s = [], []
    for l in range(depth):
        lam_init = 0.8 - 0.6 * math.exp(-0.3 * l)
        lam_vecs = jnp.stack([a_lam_q1[l], a_lam_k1[l], a_lam_q2[l], a_lam_k2[l]])
        mod, lam = _ada_call(c_all, w_ada[l], b_ada[l], lam_vecs, lam_init)
        mod3 = mod.reshape(mod.shape[0], 1, mod.shape[1])
        pad = _LANES - n_experts - n_groups
        params = dict(
            norm1_g=norm1_g[l], norm2_g=norm2_g[l], w_in=w_in[l].astype(_BF16), gq=a_qnorm_g[l], gk=a_knorm_g[l],
            subln_g=a_subln_g[l].reshape(1, -1),
            w_oa=w_oa[l].astype(_BF16), w_ob=w_ob[l].astype(_BF16), w_out=w_out[l].astype(_BF16),
            w_router=jnp.pad(jnp.concatenate([w_re[l], w_rg[l]], axis=1), ((0, 0), (0, pad))),
            b_router=jnp.pad(jnp.concatenate([b_re[l], b_rg[l]]), (0, pad)).reshape(1, _LANES),
            w1=w1[l].astype(_BF16), w3=w3[l].astype(_BF16), w2=w2[l].astype(_BF16),
            n_experts=n_experts, n_groups=n_groups,
        )
        xp, new_p = _layer(xp, mod3, 0, lam, lam_init, None, params)
        past = (cache_a_k[l], cache_a_v[l], cache_b_k[l], cache_b_v[l])
        xs, new_s = _layer(xs, mod3, batch_p, lam, lam_init, past, params)
        rows_p.append(new_p)
        rows_s.append(new_s)
    stack = lambda rows, k: jnp.stack([r[k] for r in rows])
    return (xp, xs, stack(rows_p, 0), stack(rows_p, 1), stack(rows_p, 2), stack(rows_p, 3),
            stack(rows_s, 0), stack(rows_s, 1), stack(rows_s, 2), stack(rows_s, 3))
```

```python
import functools
import math

import jax
import jax.numpy as jnp
from jax import lax
from jax.experimental import pallas as pl
from jax.experimental.pallas import tpu as pltpu

_F32 = jnp.float32
_BF16 = jnp.bfloat16

_LANES = 128
_SUBLANES = 8
_MXU_DIM = 256
_VMEM_LIMIT_BYTES = 48 * 1024 * 1024

_CHUNK = 64
_HEAD_DIM = 64
_ROPE_DIMS = _HEAD_DIM // 4
_ROPE_THETA = 500000.0
_EXP_PER_GROUP = 8
_EPS = 1e-6
_NEG_INF = -1e30
_Q_SCALE = 1.0 / math.sqrt(_HEAD_DIM)
_EXP_ZERO_BELOW = -104.0
_LOG2_E = math.log2(math.e)

_TOKEN_TILE = 512
_OUTPROJ_TILE = 256
_ATTN_TILE = _MXU_DIM
_ATTN_GROUPS = 2
_EXPERTS_PER_STEP = 4


def _dot(a, b):
    return jnp.dot(a, b, preferred_element_type=_F32)


def _split(x):
    hi = x.astype(_BF16)
    lo = (x - hi.astype(_F32)).astype(_BF16)
    return hi, lo


def _dot3(a, b):
    a_hi, a_lo = _split(a)
    b_hi, b_lo = _split(b)
    return _dot(a_hi, b_hi) + (_dot(a_hi, b_lo) + _dot(a_lo, b_hi))


def _silu(x):
    return x / (1.0 + jnp.exp(-x))


def _sigmoid(x):
    return 1.0 / (1.0 + jnp.exp(-x))


def _rms(x, axis=-1):
    return x * lax.rsqrt(jnp.mean(x * x, axis=axis, keepdims=True) + _EPS)


def _mod_rows(mod_ref, k, d, seq, n_batch):
    if n_batch == 1:
        return mod_ref[0, :, k * d:(k + 1) * d]
    rows = [jnp.broadcast_to(mod_ref[b, :, k * d:(k + 1) * d], (seq, d)) for b in range(n_batch)]
    return jnp.concatenate(rows, axis=0)


def _lane_tile(x, n):
    assert n % _LANES == 0
    return jnp.concatenate([x] * (n // _LANES), axis=1) if n > _LANES else x


def _compiler_params(semantics):
    return pltpu.CompilerParams(dimension_semantics=semantics, vmem_limit_bytes=_VMEM_LIMIT_BYTES)


def _ada_kernel(c_ref, w_ref, b_ref, lam_ref, mod_ref, lam_out_ref, *, lam_init):
    mod_ref[...] = _dot3(_silu(c_ref[...]), w_ref[...]) + b_ref[...]
    lv = lam_ref[...]
    s1 = jnp.sum(lv[0:1] * lv[1:2], axis=-1, keepdims=True)
    s2 = jnp.sum(lv[2:3] * lv[3:4], axis=-1, keepdims=True)
    lam = jnp.exp(s1) - jnp.exp(s2) + lam_init
    lam_out_ref[...] = jnp.broadcast_to(lam, lam_out_ref.shape)


def _ada_call(c_all, w_ada, b_ada, lam_vecs, lam_init):
    rows, d = c_all.shape
    cols = w_ada.shape[1]
    tn = d
    return pl.pallas_call(
        functools.partial(_ada_kernel, lam_init=lam_init),
        grid=(cols // tn,),
        in_specs=[
            pl.BlockSpec((rows, d), lambda j: (0, 0)),
            pl.BlockSpec((d, tn), lambda j: (0, j)),
            pl.BlockSpec((1, tn), lambda j: (0, j)),
            pl.BlockSpec(lam_vecs.shape, lambda j: (0, 0)),
        ],
        out_specs=[
            pl.BlockSpec((rows, tn), lambda j: (0, j)),
            pl.BlockSpec((_SUBLANES, _LANES), lambda j: (0, 0)),
        ],
        out_shape=[
            jax.ShapeDtypeStruct((rows, cols), _F32),
            jax.ShapeDtypeStruct((_SUBLANES, _LANES), _F32),
        ],
        compiler_params=_compiler_params(("arbitrary",)),
        name="ada",
    )(c_all, w_ada, b_ada.reshape(1, cols), lam_vecs)


def _adaln(x, gain, scale, shift):
    return (_rms(x) * gain) * (1.0 + scale) + shift


def _proj_kernel(mod_ref, x_ref, n1_ref, wt_ref, gq_ref, gk_ref, cos_ref, sin_ref,
                 qa_ref, ka_ref, va_ref, vat_ref, qs_ref, ks_ref, vs_ref, ht_s, *, seq, n_batch):
    j = pl.program_id(1)
    tm, d = x_ref.shape

    @pl.when(j == 0)
    def _():
        shift = _mod_rows(mod_ref, 0, d, seq, n_batch)
        scale = _mod_rows(mod_ref, 1, d, seq, n_batch)
        ht_s[...] = _adaln(x_ref[...], n1_ref[...], scale, shift).T.astype(_BF16)

    def proj_t():
        return _dot(wt_ref[...], ht_s[...])

    def store_t(out_ref, val):
        if n_batch == 1:
            out_ref[0] = val.astype(out_ref.dtype)
        else:
            for b in range(n_batch):
                out_ref[b] = val[:, b * seq:(b + 1) * seq].astype(out_ref.dtype)

    def head_norm_rope(gain_ref, out_ref, out_scale):
        acc = proj_t()
        gain = _lane_tile(gain_ref[...], tm)
        cos, sin = cos_ref[...], sin_ref[...]
        half = _ROPE_DIMS // 2
        parts = []
        for h in range(d // _HEAD_DIM):
            y = _rms(acc[h * _HEAD_DIM:(h + 1) * _HEAD_DIM], axis=0) * gain
            x1, x2 = y[:half], y[half:2 * half]
            parts += [x1 * cos - x2 * sin, x2 * cos + x1 * sin, y[2 * half:]]
        out = jnp.concatenate(parts, axis=0)
        store_t(out_ref, out if out_scale == 1.0 else out * out_scale)

    @pl.when(j == 0)
    def _():
        head_norm_rope(gq_ref, qa_ref, _Q_SCALE)

    @pl.when(j == 1)
    def _():
        head_norm_rope(gk_ref, ka_ref, 1.0)

    @pl.when(j == 2)
    def _():
        acc = proj_t()
        store_t(vat_ref, acc)
        va_ref[...] = acc.T

    @pl.when(j == 3)
    def _():
        store_t(qs_ref, proj_t() * _Q_SCALE)

    @pl.when(j == 4)
    def _():
        store_t(ks_ref, proj_t())

    @pl.when(j == 5)
    def _():
        store_t(vs_ref, proj_t())


def _rope_tables_t(pos):
    half = _ROPE_DIMS // 2
    inv_freq = jnp.exp(-math.log(_ROPE_THETA) * 2.0 * jnp.arange(half, dtype=_F32) / _ROPE_DIMS)
    ang = inv_freq[:, None] * pos.astype(_F32)[None, :]
    return jnp.cos(ang), jnp.sin(ang)


def _token_tiling(n_tokens, seq, tile=_TOKEN_TILE):
    tm = min(tile, n_tokens)
    if tm >= seq:
        assert tm % seq == 0 and n_tokens % tm == 0
        return tm, tm // seq, 1
    assert seq % tm == 0
    return tm, 1, seq // tm


def _mod_spec(width, n_batch, tiles_per_batch, mod_off):
    assert mod_off % n_batch == 0
    first = mod_off // n_batch
    return pl.BlockSpec((n_batch, 1, width), lambda i, *_: (first + i // tiles_per_batch, 0, 0))


def _proj_call(x2, mod3, mod_off, batch, seq, pos, norm1_g, w_qkv_t, gq, gk):
    n, d = x2.shape
    tm, n_batch, tiles_per_batch = _token_tiling(n, seq)
    cos, sin = _rope_tables_t(pos)
    if n_batch > 1:
        cos, sin = jnp.tile(cos, (1, n_batch)), jnp.tile(sin, (1, n_batch))
    lanes_t = tm if n_batch == 1 else seq
    gain_t = lambda g: jnp.broadcast_to(g.reshape(_HEAD_DIM, 1), (_HEAD_DIM, _LANES))
    row_spec = pl.BlockSpec((tm, d), lambda i, j: (i, 0))
    t_spec = lambda: pl.BlockSpec((n_batch, d, lanes_t), lambda i, j: (i // tiles_per_batch, 0, i % tiles_per_batch))
    rope_spec = lambda: pl.BlockSpec((_ROPE_DIMS // 2, tm), lambda i, j: (0, i % tiles_per_batch))
    const = lambda shape: pl.BlockSpec(shape, lambda i, j: (0, 0))
    t_out = lambda dt: jax.ShapeDtypeStruct((batch, d, seq), dt)
    return pl.pallas_call(
        functools.partial(_proj_kernel, seq=seq, n_batch=n_batch),
        grid=(n // tm, 6),
        in_specs=[
            _mod_spec(mod3.shape[2], n_batch, tiles_per_batch, mod_off),
            row_spec,
            const((1, d)),
            pl.BlockSpec((d, d), lambda i, j: (j, 0)),
            const((_HEAD_DIM, _LANES)), const((_HEAD_DIM, _LANES)),
            rope_spec(), rope_spec(),
        ],
        out_specs=[t_spec(), t_spec(), row_spec, t_spec(), t_spec(), t_spec(), t_spec()],
        out_shape=[t_out(_BF16), t_out(_F32), jax.ShapeDtypeStruct((n, d), _F32), t_out(_BF16),
                   t_out(_BF16), t_out(_F32), t_out(_F32)],
        scratch_shapes=[pltpu.VMEM((d, tm), _BF16)],
        compiler_params=_compiler_params(("arbitrary", "arbitrary")),
        name="proj",
    )(mod3, x2, norm1_g.reshape(1, d), w_qkv_t, gain_t(gq), gain_t(gk), cos, sin)


def _stack_heads_t(qt):
    row = lax.broadcasted_iota(jnp.int32, qt.shape, 0)
    zero = jnp.zeros_like(qt)
    return jnp.concatenate([jnp.where(row < _HEAD_DIM, qt, zero), jnp.where(row >= _HEAD_DIM, qt, zero)], axis=1)


def _diag_offsets(tq):
    assert tq & (tq - 1) == 0
    kpos = lax.broadcasted_iota(jnp.int32, (tq, 2 * tq), 0)
    qpos = lax.broadcasted_iota(jnp.int32, (tq, 2 * tq), 1) & (tq - 1)
    return kpos, qpos


def _load_kv_blocks(groups, tq, n_past, first, kt_ref, vt_ref, past_refs, past_values_token_major, k_s, v_s):
    g = _LANES
    for c in range(groups):
        rows = slice(c * g, (c + 1) * g)
        for jb in range(n_past):
            pk_ref, pv_ref = past_refs
            cols = slice(jb * tq, (jb + 1) * tq)
            k_s[c, first + jb] = pk_ref[0, rows, cols].T.astype(_BF16)
            if past_values_token_major:
                v_s[c, first + jb] = pv_ref[cols, rows].T.astype(_BF16)
            else:
                v_s[c, first + jb] = pv_ref[0, rows, cols].astype(_BF16)
        for jb in range(k_s.shape[1] - n_past - first):
            cols = slice(jb * tq, (jb + 1) * tq)
            k_s[c, first + n_past + jb] = kt_ref[0, rows, cols].T.astype(_BF16)
            v_s[c, first + n_past + jb] = vt_ref[0, rows, cols].astype(_BF16)


def _attn_a_kernel(*refs, tq, n_past, groups, seq, lam_init):
    if n_past:
        lam_ref, subg_ref, qt_ref, kt_ref, vt_ref, pk_ref, pv_ref, o_ref, k_s, v_s, qq_s, s_s, m_s, l_s, acc_s = refs
        past_refs = (pk_ref, pv_ref)
    else:
        lam_ref, subg_ref, qt_ref, kt_ref, vt_ref, o_ref, k_s, v_s, qq_s, s_s, m_s, l_s, acc_s = refs
        past_refs = None
    qi = pl.program_id(2)

    @pl.when(qi == 0)
    def _():
        _load_kv_blocks(groups, tq, n_past, 0, kt_ref, vt_ref, past_refs, True, k_s, v_s)

    for c in range(groups):
        qq_s[c] = _stack_heads_t(qt_ref[0, c * _LANES:(c + 1) * _LANES, :])
    m_s[...] = jnp.full(m_s.shape, _NEG_INF, _F32)
    l_s[...] = jnp.zeros(l_s.shape, _F32)
    acc_s[...] = jnp.zeros(acc_s.shape, _F32)

    def scores(i):
        return [_dot(k_s[c, i], qq_s[c]) for c in range(groups)]

    def store(slot, blocks):
        for c, st in enumerate(blocks):
            s_s[slot, c] = st

    def update(slot, i, mask):
        for c in range(groups):
            st = s_s[slot, c]
            if mask is not None:
                st = jnp.where(mask, st, _NEG_INF)
            m_prev = m_s[c]
            m_next = jnp.maximum(m_prev, jnp.max(st, axis=0, keepdims=True))
            alpha = jnp.exp(m_prev - m_next)
            pt = jnp.exp(st - m_next)
            l_s[c] = alpha * l_s[c] + jnp.sum(pt, axis=0, keepdims=True)
            m_s[c] = m_next
            acc_s[c] = alpha * acc_s[c] + _dot(v_s[c, i], pt.astype(_BF16))

    def step(slot, i, nxt_i, mask=None):
        nxt = scores(nxt_i)
        update(slot, i, mask)
        store(1 - slot, nxt)

    last = n_past + qi
    kpos, qpos = _diag_offsets(tq)
    chunk_bits = _CHUNK.bit_length() - 1
    mask = (kpos >> chunk_bits) <= (qpos >> chunk_bits)
    if seq < tq:
        mask = mask & (kpos < seq)
    store(0, scores(last))
    step(0, last, 0, mask)

    def pair(j, carry):
        step(1, 2 * j, 2 * j + 1)
        step(0, 2 * j + 1, jnp.minimum(2 * j + 2, last))
        return carry

    lax.fori_loop(0, last // 2, pair, 0)

    @pl.when((last & 1) == 1)
    def _():
        update(1, last - 1, None)

    gain = _lane_tile(subg_ref[...], tq)
    for c in range(groups):
        out = acc_s[c] / l_s[c]
        o = out[:, :tq] - lam_ref[0:1, 0:1] * out[:, tq:]
        y = (_rms(o, axis=0) * gain) * (1.0 - lam_init)
        o_ref[:, c * _LANES:(c + 1) * _LANES] = y.T.astype(o_ref.dtype)


def _attn_b_kernel(*refs, tq, n_past, groups, seq):
    if n_past:
        qt_ref, kt_ref, vt_ref, pk_ref, pv_ref, o_ref, k_s, v_s, qq_s, s_s, u_s, c_s, acc_s = refs
        past_refs = (pk_ref, pv_ref)
    else:
        qt_ref, kt_ref, vt_ref, o_ref, k_s, v_s, qq_s, s_s, u_s, c_s, acc_s = refs
        past_refs = None
    qi = pl.program_id(2)

    @pl.when(qi == 0)
    def _():
        for c in range(groups):
            k_s[c, 0] = jnp.zeros(k_s.shape[2:], _BF16)
            v_s[c, 0] = jnp.zeros(v_s.shape[2:], _BF16)
        _load_kv_blocks(groups, tq, n_past, 1, kt_ref, vt_ref, past_refs, False, k_s, v_s)

    for c in range(groups):
        qq_s[c] = _stack_heads_t(qt_ref[0, c * _LANES:(c + 1) * _LANES, :])
    s_idx = lax.broadcasted_iota(jnp.int32, (tq, 2 * tq), 0)
    j_idx = lax.broadcasted_iota(jnp.int32, (tq, 2 * tq), 1) & (tq - 1)
    u_s[...] = jnp.where(j_idx >= s_idx, -1.0, 0.0).astype(_BF16)
    c_s[...] = jnp.zeros(c_s.shape, _F32)
    acc_s[...] = jnp.zeros(acc_s.shape, _F32)

    def logits(i):
        return [_dot(k_s[c, i], qq_s[c]) for c in range(groups)]

    def update(slot, i, mask):
        rests = []
        for c in range(groups):
            zt = s_s[slot, c]
            neg_log_rest = jnp.maximum(zt, 0.0) + jnp.log(1.0 + jnp.exp2(jnp.abs(zt) * (-_LOG2_E)))
            if mask is not None:
                neg_log_rest = jnp.where(mask, neg_log_rest, 0.0)
            hi, lo = _split(neg_log_rest)
            rests.append(_dot(u_s[...], jnp.concatenate([hi, lo], axis=0)))
        for c, rest_from_here in enumerate(rests):
            wt = jnp.exp(s_s[slot, c] + rest_from_here + c_s[c])
            if mask is not None:
                wt = jnp.where(mask, wt, 0.0)
            acc_s[c] = acc_s[c] + _dot(v_s[c, i], wt.astype(_BF16))
            c_s[c] = c_s[c] + rest_from_here[0:1, :]

    def carry_max():
        return functools.reduce(jnp.maximum, [jnp.max(c_s[c]) for c in range(groups)])

    top = n_past + qi + 1

    def store(slot, blocks):
        for c, zt in enumerate(blocks):
            s_s[slot, c] = zt

    def step(slot, i, nxt_i, mask=None):
        nxt = logits(nxt_i)
        update(slot, i, mask)
        store(1 - slot, nxt)

    kpos, qpos = _diag_offsets(tq)
    store(0, logits(top))
    step(0, top, top - 1, kpos < qpos)

    def pair(state):
        i = state[0]
        step(1, i, i - 1)
        step(0, i - 1, jnp.maximum(i - 2, 0))
        return i - 2, carry_max()

    lax.while_loop(lambda s: (s[0] >= 1) & (s[1] > _EXP_ZERO_BELOW), pair, (top - 1, carry_max()))

    row = lax.broadcasted_iota(jnp.int32, (_LANES, tq), 0)
    for c in range(groups):
        acc = acc_s[c]
        out = jnp.where(row < _HEAD_DIM, acc[:, :tq], acc[:, tq:])
        o_ref[:, c * _LANES:(c + 1) * _LANES] = out.T.astype(o_ref.dtype)


def _attn_call(kind, qt, kt, vt, past, seq, extra_inputs, **kernel_kwargs):
    batch, d, seq_pad = qt.shape
    tq = min(_ATTN_TILE, seq_pad)
    assert seq_pad % tq == 0 and tq % _LANES == 0
    nq = seq_pad // tq
    groups = _ATTN_GROUPS
    gw = groups * _LANES
    assert d % gw == 0
    const = lambda a: pl.BlockSpec(a.shape, lambda b, g, i: (0,) * a.ndim)
    q_spec = pl.BlockSpec((1, gw, tq), lambda b, g, i: (b, g, i))
    kv_spec = pl.BlockSpec((1, gw, seq_pad), lambda b, g, i: (b, g, 0))
    inputs = list(extra_inputs) + [qt, kt, vt]
    in_specs = [const(a) for a in extra_inputs] + [q_spec, kv_spec, kv_spec]
    n_past = 0
    if past is not None:
        pkt, pv = past
        past_len = pkt.shape[2]
        assert past_len % tq == 0 and past_len % _CHUNK == 0
        n_past = past_len // tq
        inputs += [pkt, pv]
        in_specs.append(pl.BlockSpec((1, gw, past_len), lambda b, g, i: (b, g, 0)))
        if kind == "a":
            in_specs.append(pl.BlockSpec((past_len, gw), lambda b, g, i: (b, g)))
        else:
            in_specs.append(pl.BlockSpec((1, gw, past_len), lambda b, g, i: (b, g, 0)))
    n_blocks = n_past + nq + (1 if kind == "b" else 0)
    scratch = [
        pltpu.VMEM((groups, n_blocks, tq, _LANES), _BF16),
        pltpu.VMEM((groups, n_blocks, _LANES, tq), _BF16),
        pltpu.VMEM((groups, _LANES, 2 * tq), _BF16),
        pltpu.VMEM((2, groups, tq, 2 * tq), _F32),
    ]
    row_state = pltpu.VMEM((groups, 1, 2 * tq), _F32)
    acc_state = pltpu.VMEM((groups, _LANES, 2 * tq), _F32)
    if kind == "a":
        body = functools.partial(_attn_a_kernel, tq=tq, n_past=n_past, groups=groups, seq=seq, **kernel_kwargs)
        scratch += [row_state, row_state, acc_state]
    else:
        body = functools.partial(_attn_b_kernel, tq=tq, n_past=n_past, groups=groups, seq=seq)
        scratch += [pltpu.VMEM((tq, 2 * tq), _BF16), row_state, acc_state]
    return pl.pallas_call(
        body,
        grid=(batch, d // gw, nq),
        in_specs=in_specs,
        out_specs=pl.BlockSpec((tq, gw), lambda b, g, i: (b * nq + i, g)),
        out_shape=jax.ShapeDtypeStruct((batch * seq_pad, d), _BF16),
        scratch_shapes=scratch,
        compiler_params=_compiler_params(("arbitrary", "arbitrary", "arbitrary")),
        name="attn_" + kind,
    )(*inputs)


def _router_gate(logits, n_experts, n_groups):
    lane_i = lax.broadcasted_iota(jnp.int32, logits.shape, 1)
    lane = lane_i.astype(_F32)
    group_of_lane = (lane_i >> (_EXP_PER_GROUP.bit_length() - 1)).astype(_F32)
    big = float(4 * _LANES)
    row_max = lambda v: jnp.max(v, axis=1, keepdims=True)
    row_min = lambda v: jnp.min(v, axis=1, keepdims=True)
    row_sum = lambda v: jnp.sum(v, axis=1, keepdims=True)

    is_group = (lane_i >= n_experts) & (lane_i < n_experts + n_groups)
    lg = jnp.where(is_group, logits, _NEG_INF)
    eg = jnp.where(is_group, jnp.exp(lg - row_max(lg)), 0.0)
    pg = eg / row_sum(eg)
    pg_sel = row_max(pg)
    g_sel = row_min(jnp.where(is_group & (pg == pg_sel), lane - n_experts, big))

    in_group = (lane_i < n_experts) & (group_of_lane == g_sel)
    le = jnp.where(in_group, logits, _NEG_INF)
    ee = jnp.where(in_group, jnp.exp(le - row_max(le)), 0.0)
    pe = ee / row_sum(ee)
    p1 = row_max(pe)
    i1 = row_min(jnp.where(in_group & (pe == p1), lane, big))
    rest = in_group & (lane != i1)
    p2 = row_max(jnp.where(rest, pe, -1.0))
    i2 = row_min(jnp.where(rest & (pe == p2), lane, big))
    total = p1 + p2
    return jnp.where(lane == i1, p1 / total * pg_sel, 0.0) + jnp.where(lane == i2, p2 / total * pg_sel, 0.0)


def _outproj_kernel(mod_ref, x_ref, oa_ref, ob_ref, n1_ref, n2_ref, wg_ref, woa_ref, wob_ref, wout_ref,
                    wr_hi_ref, wr_lo_ref, br_ref, x1_ref, h2_ref, gate_ref, *, seq, n_batch, n_experts, n_groups):
    d = x_ref.shape[1]
    rows = lambda k: _mod_rows(mod_ref, k, d, seq, n_batch)
    x = x_ref[...]
    h = _adaln(x, n1_ref[...], rows(1), rows(0)).astype(_BF16)
    gate_a = _sigmoid(_dot(h, wg_ref[:, :d]))
    gate_b = _sigmoid(_dot(h, wg_ref[:, d:]))
    mix = gate_a * _dot(oa_ref[...], woa_ref[...]) + gate_b * _dot(ob_ref[...], wob_ref[...])
    x1 = x + rows(2) * _dot(mix.astype(_BF16), wout_ref[...])
    x1_ref[...] = x1
    h2 = _adaln(x1, n2_ref[...], rows(4), rows(3))
    h2_ref[...] = h2.astype(_BF16)
    h2_hi, h2_lo = _split(h2)
    logits = _dot(h2_hi, wr_hi_ref[...]) + (_dot(h2_hi, wr_lo_ref[...]) + _dot(h2_lo, wr_hi_ref[...])) + br_ref[...]
    gate_ref[...] = _router_gate(logits, n_experts, n_groups)


def _outproj_call(x2, oa, ob, mod3, mod_off, seq, norm1_g, norm2_g, w_gate_bf, w_oa_bf, w_ob_bf, w_out_bf,
                  w_router, b_router, n_experts, n_groups):
    n, d = x2.shape
    tm, n_batch, tiles_per_batch = _token_tiling(n, seq, _OUTPROJ_TILE)
    wr_hi, wr_lo = _split(w_router)
    row_spec = lambda: pl.BlockSpec((tm, d), lambda i: (i, 0))
    full = lambda a: pl.BlockSpec(a.shape, lambda i: (0,) * a.ndim)
    return pl.pallas_call(
        functools.partial(_outproj_kernel, seq=seq, n_batch=n_batch, n_experts=n_experts, n_groups=n_groups),
        grid=(n // tm,),
        in_specs=[
            _mod_spec(mod3.shape[2], n_batch, tiles_per_batch, mod_off),
            row_spec(), row_spec(), row_spec(),
            pl.BlockSpec((1, d), lambda i: (0, 0)), pl.BlockSpec((1, d), lambda i: (0, 0)),
            full(w_gate_bf), full(w_oa_bf), full(w_ob_bf), full(w_out_bf), full(wr_hi), full(wr_lo),
            pl.BlockSpec((1, _LANES), lambda i: (0, 0)),
        ],
        out_specs=[row_spec(), row_spec(), pl.BlockSpec((tm, _LANES), lambda i: (i, 0))],
        out_shape=[
            jax.ShapeDtypeStruct((n, d), _F32),
            jax.ShapeDtypeStruct((n, d), _BF16),
            jax.ShapeDtypeStruct((n, _LANES), _F32),
        ],
        compiler_params=_compiler_params(("arbitrary",)),
        name="outproj",
    )(mod3, x2, oa, ob, norm1_g.reshape(1, d), norm2_g.reshape(1, d), w_gate_bf, w_oa_bf, w_ob_bf, w_out_bf,
      wr_hi, wr_lo, b_router)


def _moe_kernel(mod_ref, x1_ref, h2_ref, gate_ref, w1_ref, w3_ref, w2_ref, y_ref, acc_s, *, seq, n_batch):
    e = pl.program_id(1)
    d = x1_ref.shape[1]
    experts_per_step = w1_ref.shape[0]

    @pl.when(e == 0)
    def _():
        acc_s[...] = jnp.zeros(acc_s.shape, _F32)

    h = h2_ref[...]
    gate = gate_ref[...]
    lane = lax.broadcasted_iota(jnp.int32, gate.shape, 1)
    for k in range(experts_per_step):
        g = jnp.sum(jnp.where(lane == e * experts_per_step + k, gate, 0.0), axis=1, keepdims=True)
        hid = _silu(_dot(h, w1_ref[k])) * _dot(h, w3_ref[k])
        acc_s[...] += _dot((hid * g).astype(_BF16), w2_ref[k])

    @pl.when(e == pl.num_programs(1) - 1)
    def _():
        y_ref[...] = x1_ref[...] + _mod_rows(mod_ref, 5, d, seq, n_batch) * acc_s[...]


def _moe_call(x1, h2, gate, mod3, mod_off, seq, w1_bf, w3_bf, w2_bf):
    n, d = x1.shape
    tm, n_batch, tiles_per_batch = _token_tiling(n, seq)
    n_experts, _, d_expert = w1_bf.shape
    eb = _EXPERTS_PER_STEP
    assert n_experts % eb == 0
    row_spec = lambda w: pl.BlockSpec((tm, w), lambda i, e: (i, 0))
    return pl.pallas_call(
        functools.partial(_moe_kernel, seq=seq, n_batch=n_batch),
        grid=(n // tm, n_experts // eb),
        in_specs=[
            _mod_spec(mod3.shape[2], n_batch, tiles_per_batch, mod_off),
            row_spec(d), row_spec(d), row_spec(_LANES),
            pl.BlockSpec((eb, d, d_expert), lambda i, e: (e, 0, 0)),
            pl.BlockSpec((eb, d, d_expert), lambda i, e: (e, 0, 0)),
            pl.BlockSpec((eb, d_expert, d), lambda i, e: (e, 0, 0)),
        ],
        out_specs=row_spec(d),
        out_shape=jax.ShapeDtypeStruct((n, d), _F32),
        scratch_shapes=[pltpu.VMEM((tm, d), _F32)],
        compiler_params=_compiler_params(("arbitrary", "arbitrary")),
        name="moe",
    )(mod3, x1, h2, gate, w1_bf, w3_bf, w2_bf)


def _feature_major(c):
    b, t, h, dh = c.shape
    return jnp.transpose(c, (0, 2, 3, 1)).reshape(b, h * dh, t)


def _token_major(ct, head_dim):
    b, d, t = ct.shape
    return jnp.transpose(ct.reshape(b, d // head_dim, head_dim, t), (0, 3, 1, 2))


def _layer(x, mod3, mod_off, lam, lam_init, past, p):
    batch, seq, d = x.shape
    x2 = x.reshape(batch * seq, d)
    past_len = 0 if past is None else past[0].shape[1]
    pos = past_len + jnp.arange(seq)
    qa_t, ka_t, va, va_t, qs_t, ks_t, vs_t = _proj_call(
        x2, mod3, mod_off, batch, seq, pos, p["norm1_g"], p["w_qkv_t"], p["gq"], p["gk"])
    seq_pad = -(-seq // _LANES) * _LANES
    pad = lambda a: a if seq_pad == seq else jnp.pad(a, ((0, 0), (0, 0), (0, seq_pad - seq)))
    past_a = past_b = None
    if past is not None:
        past_a = (_feature_major(past[0]), past[1].reshape(batch * past_len, d))
        past_b = (_feature_major(past[2]), _feature_major(past[3]))
    oa = _attn_call("a", pad(qa_t), pad(ka_t), pad(va_t), past_a, seq, [lam, p["subln_g"]], lam_init=lam_init)
    ob = _attn_call("b", pad(qs_t), pad(ks_t), pad(vs_t), past_b, seq, [])
    if seq_pad != seq:
        unpad = lambda o: o.reshape(batch, seq_pad, d)[:, :seq].reshape(batch * seq, d)
        oa, ob = unpad(oa), unpad(ob)
    x1, h2, gate = _outproj_call(x2, oa, ob, mod3, mod_off, seq, p["norm1_g"], p["norm2_g"], p["w_gate"],
                                 p["w_oa"], p["w_ob"], p["w_out"], p["w_router"], p["b_router"],
                                 p["n_experts"], p["n_groups"])
    y = _moe_call(x1, h2, gate, mod3, mod_off, seq, p["w1"], p["w3"], p["w2"])
    new = (_token_major(ka_t, _HEAD_DIM), va.reshape(batch, seq, -1, 2 * _HEAD_DIM),
           _token_major(ks_t, _HEAD_DIM), _token_major(vs_t, _HEAD_DIM))
    return y.reshape(batch, seq, d), new


def kernel(x_prompt, x_sample, cache_a_k, cache_a_v, cache_b_k, cache_b_v, c_prompt, c_sample, norm1_g, norm2_g, w_ada, b_ada, w_in, a_qnorm_g, a_knorm_g, a_lam_q1, a_lam_k1, a_lam_q2, a_lam_k2, a_subln_g, w_oa, w_ob, w_out, w_rg, b_rg, w_re, b_re, w1, w3, w2):
    depth, d = norm1_g.shape
    n_groups, n_experts = w_rg.shape[2], w_re.shape[2]
    assert w_in.shape[2] == 8 * d and n_experts + n_groups <= _LANES
    batch_p = x_prompt.shape[0]
    xp, xs = x_prompt, x_sample
    c_all = jnp.concatenate([c_prompt, c_sample], axis=0)
    rows_p, rows_s = [], []
    for l in range(depth):
        lam_init = 0.8 - 0.6 * math.exp(-0.3 * l)
        lam_vecs = jnp.stack([a_lam_q1[l], a_lam_k1[l], a_lam_q2[l], a_lam_k2[l]])
        mod, lam = _ada_call(c_all, w_ada[l], b_ada[l], lam_vecs, lam_init)
        mod3 = mod.reshape(mod.shape[0], 1, mod.shape[1])
        pad = _LANES - n_experts - n_groups
        params = dict(
            norm1_g=norm1_g[l], norm2_g=norm2_g[l],
            w_qkv_t=w_in[l][:, :6 * d].T.astype(_BF16), w_gate=w_in[l][:, 6 * d:].astype(_BF16),
            gq=a_qnorm_g[l], gk=a_knorm_g[l],
            subln_g=jnp.broadcast_to(a_subln_g[l].reshape(-1, 1), (a_subln_g.shape[1], _LANES)),
            w_oa=w_oa[l].astype(_BF16), w_ob=w_ob[l].astype(_BF16), w_out=w_out[l].astype(_BF16),
            w_router=jnp.pad(jnp.concatenate([w_re[l], w_rg[l]], axis=1), ((0, 0), (0, pad))),
            b_router=jnp.pad(jnp.concatenate([b_re[l], b_rg[l]]), (0, pad)).reshape(1, _LANES),
            w1=w1[l].astype(_BF16), w3=w3[l].astype(_BF16), w2=w2[l].astype(_BF16),
            n_experts=n_experts, n_groups=n_groups,
        )
        xp, new_p = _layer(xp, mod3, 0, lam, lam_init, None, params)
        past = (cache_a_k[l], cache_a_v[l], cache_b_k[l], cache_b_v[l])
        xs, new_s = _layer(xs, mod3, batch_p, lam, lam_init, past, params)
        rows_p.append(new_p)
        rows_s.append(new_s)
    stack = lambda rows, k: jnp.stack([r[k] for r in rows])
    return (xp, xs, stack(rows_p, 0), stack(rows_p, 1), stack(rows_p, 2), stack(rows_p, 3),
            stack(rows_s, 0), stack(rows_s, 1), stack(rows_s, 2), stack(rows_s, 3))
```

```python
import functools
import math

import jax
import jax.numpy as jnp
from jax import lax
from jax.experimental import pallas as pl
from jax.experimental.pallas import tpu as pltpu

_F32 = jnp.float32
_BF16 = jnp.bfloat16

_LANES = 128
_SUBLANES = 8
_MXU_DIM = 256
_VMEM_LIMIT_BYTES = 48 * 1024 * 1024

_CHUNK = 64
_HEAD_DIM = 64
_ROPE_DIMS = _HEAD_DIM // 4
_ROPE_THETA = 500000.0
_EXP_PER_GROUP = 8
_EPS = 1e-6
_NEG_INF = -1e30
_Q_SCALE = 1.0 / math.sqrt(_HEAD_DIM)
_EXP_ZERO_BELOW = -104.0
_LOG2_E = math.log2(math.e)

_TOKEN_TILE = 512
_OUTPROJ_TILE = 512
_ATTN_TILE = _MXU_DIM
_ATTN_GROUPS = 2
_MOE_TILE = 1024
_MOE_CHUNK_SLACK = 1.25
_MOE_VMEM_LIMIT_BYTES = 56 * 1024 * 1024


def _dot(a, b):
    return jnp.dot(a, b, preferred_element_type=_F32)


def _split(x):
    hi = x.astype(_BF16)
    lo = (x - hi.astype(_F32)).astype(_BF16)
    return hi, lo


def _dot3(a, b):
    a_hi, a_lo = _split(a)
    b_hi, b_lo = _split(b)
    return _dot(a_hi, b_hi) + (_dot(a_hi, b_lo) + _dot(a_lo, b_hi))


def _silu(x):
    return x / (1.0 + jnp.exp(-x))


def _sigmoid(x):
    return 1.0 / (1.0 + jnp.exp(-x))


def _rms(x, axis=-1):
    return x * lax.rsqrt(jnp.mean(x * x, axis=axis, keepdims=True) + _EPS)


def _mod_rows(mod_ref, k, d, seq, n_batch):
    if n_batch == 1:
        return mod_ref[0, :, k * d:(k + 1) * d]
    rows = [jnp.broadcast_to(mod_ref[b, :, k * d:(k + 1) * d], (seq, d)) for b in range(n_batch)]
    return jnp.concatenate(rows, axis=0)


def _lane_tile(x, n):
    assert n % _LANES == 0
    return jnp.concatenate([x] * (n // _LANES), axis=1) if n > _LANES else x


def _compiler_params(semantics, vmem_limit_bytes=_VMEM_LIMIT_BYTES):
    return pltpu.CompilerParams(dimension_semantics=semantics, vmem_limit_bytes=vmem_limit_bytes)


def _ada_kernel(c_ref, w_ref, b_ref, lam_ref, mod_ref, lam_out_ref, *, lam_init):
    mod_ref[...] = _dot3(_silu(c_ref[...]), w_ref[...]) + b_ref[...]
    lv = lam_ref[...]
    s1 = jnp.sum(lv[0:1] * lv[1:2], axis=-1, keepdims=True)
    s2 = jnp.sum(lv[2:3] * lv[3:4], axis=-1, keepdims=True)
    lam = jnp.exp(s1) - jnp.exp(s2) + lam_init
    lam_out_ref[...] = jnp.broadcast_to(lam, lam_out_ref.shape)


def _ada_call(c_all, w_ada, b_ada, lam_vecs, lam_init):
    rows, d = c_all.shape
    cols = w_ada.shape[1]
    tn = d
    return pl.pallas_call(
        functools.partial(_ada_kernel, lam_init=lam_init),
        grid=(cols // tn,),
        in_specs=[
            pl.BlockSpec((rows, d), lambda j: (0, 0)),
            pl.BlockSpec((d, tn), lambda j: (0, j)),
            pl.BlockSpec((1, tn), lambda j: (0, j)),
            pl.BlockSpec(lam_vecs.shape, lambda j: (0, 0)),
        ],
        out_specs=[
            pl.BlockSpec((rows, tn), lambda j: (0, j)),
            pl.BlockSpec((_SUBLANES, _LANES), lambda j: (0, 0)),
        ],
        out_shape=[
            jax.ShapeDtypeStruct((rows, cols), _F32),
            jax.ShapeDtypeStruct((_SUBLANES, _LANES), _F32),
        ],
        compiler_params=_compiler_params(("arbitrary",)),
        name="ada",
    )(c_all, w_ada, b_ada.reshape(1, cols), lam_vecs)


def _adaln(x, gain, scale, shift):
    return (_rms(x) * gain) * (1.0 + scale) + shift


def _proj_kernel(mod_ref, x_ref, n1_ref, wt_ref, gq_ref, gk_ref, cos_ref, sin_ref,
                 qa_ref, ka_ref, va_ref, vat_ref, qs_ref, ks_ref, vs_ref, ht_s, *, seq, n_batch):
    j = pl.program_id(1)
    tm, d = x_ref.shape

    @pl.when(j == 0)
    def _():
        shift = _mod_rows(mod_ref, 0, d, seq, n_batch)
        scale = _mod_rows(mod_ref, 1, d, seq, n_batch)
        ht_s[...] = _adaln(x_ref[...], n1_ref[...], scale, shift).T.astype(_BF16)

    def proj_t():
        w = wt_ref[pl.ds(pl.multiple_of(j * d, d), d), :]
        return _dot(w, ht_s[...])

    def store_t(out_ref, val):
        if n_batch == 1:
            out_ref[0] = val.astype(out_ref.dtype)
        else:
            for b in range(n_batch):
                out_ref[b] = val[:, b * seq:(b + 1) * seq].astype(out_ref.dtype)

    def head_norm_rope(gain_ref, out_ref, out_scale):
        acc = proj_t()
        gain = _lane_tile(gain_ref[...], tm)
        cos, sin = cos_ref[...], sin_ref[...]
        half = _ROPE_DIMS // 2
        parts = []
        for h in range(d // _HEAD_DIM):
            y = _rms(acc[h * _HEAD_DIM:(h + 1) * _HEAD_DIM], axis=0) * gain
            x1, x2 = y[:half], y[half:2 * half]
            parts += [x1 * cos - x2 * sin, x2 * cos + x1 * sin, y[2 * half:]]
        out = jnp.concatenate(parts, axis=0)
        store_t(out_ref, out if out_scale == 1.0 else out * out_scale)

    @pl.when(j == 0)
    def _():
        head_norm_rope(gq_ref, qa_ref, _Q_SCALE)

    @pl.when(j == 1)
    def _():
        head_norm_rope(gk_ref, ka_ref, 1.0)

    @pl.when(j == 2)
    def _():
        acc = proj_t()
        store_t(vat_ref, acc)
        va_ref[...] = acc.T

    @pl.when(j == 3)
    def _():
        store_t(qs_ref, proj_t() * _Q_SCALE)

    @pl.when(j == 4)
    def _():
        store_t(ks_ref, proj_t())

    @pl.when(j == 5)
    def _():
        store_t(vs_ref, proj_t())


def _rope_tables_t(pos):
    half = _ROPE_DIMS // 2
    inv_freq = jnp.exp(-math.log(_ROPE_THETA) * 2.0 * jnp.arange(half, dtype=_F32) / _ROPE_DIMS)
    ang = inv_freq[:, None] * pos.astype(_F32)[None, :]
    return jnp.cos(ang), jnp.sin(ang)


def _token_tiling(n_tokens, seq, tile=_TOKEN_TILE):
    tm = min(tile, n_tokens)
    if tm >= seq:
        assert tm % seq == 0 and n_tokens % tm == 0
        return tm, tm // seq, 1
    assert seq % tm == 0
    return tm, 1, seq // tm


def _mod_spec(width, n_batch, tiles_per_batch, mod_off):
    assert mod_off % n_batch == 0
    first = mod_off // n_batch
    return pl.BlockSpec((n_batch, 1, width), lambda i, *_: (first + i // tiles_per_batch, 0, 0))


def _proj_call(x2, mod3, mod_off, batch, seq, pos, norm1_g, w_qkv_t, gq, gk):
    n, d = x2.shape
    tm, n_batch, tiles_per_batch = _token_tiling(n, seq)
    cos, sin = _rope_tables_t(pos)
    if n_batch > 1:
        cos, sin = jnp.tile(cos, (1, n_batch)), jnp.tile(sin, (1, n_batch))
    lanes_t = tm if n_batch == 1 else seq
    gain_t = lambda g: jnp.broadcast_to(g.reshape(_HEAD_DIM, 1), (_HEAD_DIM, _LANES))
    row_spec = pl.BlockSpec((tm, d), lambda i, j: (i, 0))
    t_spec = lambda: pl.BlockSpec((n_batch, d, lanes_t), lambda i, j: (i // tiles_per_batch, 0, i % tiles_per_batch))
    rope_spec = lambda: pl.BlockSpec((_ROPE_DIMS // 2, tm), lambda i, j: (0, i % tiles_per_batch))
    const = lambda shape: pl.BlockSpec(shape, lambda i, j: (0, 0))
    t_out = lambda dt: jax.ShapeDtypeStruct((batch, d, seq), dt)
    return pl.pallas_call(
        functools.partial(_proj_kernel, seq=seq, n_batch=n_batch),
        grid=(n // tm, 6),
        in_specs=[
            _mod_spec(mod3.shape[2], n_batch, tiles_per_batch, mod_off),
            row_spec,
            const((1, d)),
            pl.BlockSpec(w_qkv_t.shape, lambda i, j: (0, 0), pipeline_mode=pl.Buffered(1)),
            const((_HEAD_DIM, _LANES)), const((_HEAD_DIM, _LANES)),
            rope_spec(), rope_spec(),
        ],
        out_specs=[t_spec(), t_spec(), row_spec, t_spec(), t_spec(), t_spec(), t_spec()],
        out_shape=[t_out(_BF16), t_out(_F32), jax.ShapeDtypeStruct((n, d), _F32), t_out(_BF16),
                   t_out(_BF16), t_out(_F32), t_out(_F32)],
        scratch_shapes=[pltpu.VMEM((d, tm), _BF16)],
        compiler_params=_compiler_params(("arbitrary", "arbitrary")),
        name="proj",
    )(mod3, x2, norm1_g.reshape(1, d), w_qkv_t, gain_t(gq), gain_t(gk), cos, sin)


def _stack_heads_t(qt):
    row = lax.broadcasted_iota(jnp.int32, qt.shape, 0)
    zero = jnp.zeros_like(qt)
    return jnp.concatenate([jnp.where(row < _HEAD_DIM, qt, zero), jnp.where(row >= _HEAD_DIM, qt, zero)], axis=1)


def _diag_offsets(tq):
    assert tq & (tq - 1) == 0
    kpos = lax.broadcasted_iota(jnp.int32, (tq, 2 * tq), 0)
    qpos = lax.broadcasted_iota(jnp.int32, (tq, 2 * tq), 1) & (tq - 1)
    return kpos, qpos


def _load_kv_blocks(groups, tq, n_past, first, kt_ref, vt_ref, past_refs, past_values_token_major, k_s, v_s):
    g = _LANES
    for c in range(groups):
        rows = slice(c * g, (c + 1) * g)
        for jb in range(n_past):
            pk_ref, pv_ref = past_refs
            cols = slice(jb * tq, (jb + 1) * tq)
            k_s[c, first + jb] = pk_ref[0, rows, cols].T.astype(_BF16)
            if past_values_token_major:
                v_s[c, first + jb] = pv_ref[cols, rows].T.astype(_BF16)
            else:
                v_s[c, first + jb] = pv_ref[0, rows, cols].astype(_BF16)
        for jb in range(k_s.shape[1] - n_past - first):
            cols = slice(jb * tq, (jb + 1) * tq)
            k_s[c, first + n_past + jb] = kt_ref[0, rows, cols].T.astype(_BF16)
            v_s[c, first + n_past + jb] = vt_ref[0, rows, cols].astype(_BF16)


def _attn_a_kernel(*refs, tq, n_past, groups, seq, lam_init):
    if n_past:
        lam_ref, subg_ref, qt_ref, kt_ref, vt_ref, pk_ref, pv_ref, o_ref, k_s, v_s, qq_s, s_s, m_s, l_s, acc_s = refs
        past_refs = (pk_ref, pv_ref)
    else:
        lam_ref, subg_ref, qt_ref, kt_ref, vt_ref, o_ref, k_s, v_s, qq_s, s_s, m_s, l_s, acc_s = refs
        past_refs = None
    qi = pl.program_id(2)

    @pl.when(qi == 0)
    def _():
        _load_kv_blocks(groups, tq, n_past, 0, kt_ref, vt_ref, past_refs, True, k_s, v_s)

    for c in range(groups):
        qq_s[c] = _stack_heads_t(qt_ref[0, c * _LANES:(c + 1) * _LANES, :])
    m_s[...] = jnp.full(m_s.shape, _NEG_INF, _F32)
    l_s[...] = jnp.zeros(l_s.shape, _F32)
    acc_s[...] = jnp.zeros(acc_s.shape, _F32)

    def scores(i):
        return [_dot(k_s[c, i], qq_s[c]) for c in range(groups)]

    def store(slot, blocks):
        for c, st in enumerate(blocks):
            s_s[slot, c] = st

    def update(slot, i, mask):
        for c in range(groups):
            st = s_s[slot, c]
            if mask is not None:
                st = jnp.where(mask, st, _NEG_INF)
            m_prev = m_s[c]
            m_next = jnp.maximum(m_prev, jnp.max(st, axis=0, keepdims=True))
            alpha = jnp.exp(m_prev - m_next)
            pt = jnp.exp(st - m_next)
            l_s[c] = alpha * l_s[c] + jnp.sum(pt, axis=0, keepdims=True)
            m_s[c] = m_next
            acc_s[c] = alpha * acc_s[c] + _dot(v_s[c, i], pt.astype(_BF16))

    def step(slot, i, nxt_i, mask=None):
        nxt = scores(nxt_i)
        update(slot, i, mask)
        store(1 - slot, nxt)

    last = n_past + qi
    kpos, qpos = _diag_offsets(tq)
    chunk_bits = _CHUNK.bit_length() - 1
    mask = (kpos >> chunk_bits) <= (qpos >> chunk_bits)
    if seq < tq:
        mask = mask & (kpos < seq)
    store(0, scores(last))
    step(0, last, 0, mask)

    def pair(j, carry):
        step(1, 2 * j, 2 * j + 1)
        step(0, 2 * j + 1, jnp.minimum(2 * j + 2, last))
        return carry

    lax.fori_loop(0, last // 2, pair, 0)

    @pl.when((last & 1) == 1)
    def _():
        update(1, last - 1, None)

    gain = _lane_tile(subg_ref[...], tq)
    for c in range(groups):
        out = acc_s[c] / l_s[c]
        o = out[:, :tq] - lam_ref[0:1, 0:1] * out[:, tq:]
        y = (_rms(o, axis=0) * gain) * (1.0 - lam_init)
        o_ref[:, c * _LANES:(c + 1) * _LANES] = y.T.astype(o_ref.dtype)


def _attn_b_kernel(*refs, tq, n_past, groups, seq):
    if n_past:
        qt_ref, kt_ref, vt_ref, pk_ref, pv_ref, o_ref, k_s, v_s, qq_s, s_s, u_s, c_s, acc_s = refs
        past_refs = (pk_ref, pv_ref)
    else:
        qt_ref, kt_ref, vt_ref, o_ref, k_s, v_s, qq_s, s_s, u_s, c_s, acc_s = refs
        past_refs = None
    qi = pl.program_id(2)

    @pl.when(qi == 0)
    def _():
        for c in range(groups):
            k_s[c, 0] = jnp.zeros(k_s.shape[2:], _BF16)
            v_s[c, 0] = jnp.zeros(v_s.shape[2:], _BF16)
        _load_kv_blocks(groups, tq, n_past, 1, kt_ref, vt_ref, past_refs, False, k_s, v_s)

    for c in range(groups):
        qq_s[c] = _stack_heads_t(qt_ref[0, c * _LANES:(c + 1) * _LANES, :])
    s_idx = lax.broadcasted_iota(jnp.int32, (tq, 2 * tq), 0)
    j_idx = lax.broadcasted_iota(jnp.int32, (tq, 2 * tq), 1) & (tq - 1)
    u_s[...] = jnp.where(j_idx >= s_idx, -1.0, 0.0).astype(_BF16)
    c_s[...] = jnp.zeros(c_s.shape, _F32)
    acc_s[...] = jnp.zeros(acc_s.shape, _F32)

    def logits(i):
        return [_dot(k_s[c, i], qq_s[c]) for c in range(groups)]

    def update(slot, i, mask):
        rests = []
        for c in range(groups):
            zt = s_s[slot, c]
            neg_log_rest = jnp.maximum(zt, 0.0) + jnp.log(1.0 + jnp.exp2(jnp.abs(zt) * (-_LOG2_E)))
            if mask is not None:
                neg_log_rest = jnp.where(mask, neg_log_rest, 0.0)
            hi, lo = _split(neg_log_rest)
            rests.append(_dot(u_s[...], jnp.concatenate([hi, lo], axis=0)))
        for c, rest_from_here in enumerate(rests):
            wt = jnp.exp(s_s[slot, c] + rest_from_here + c_s[c])
            if mask is not None:
                wt = jnp.where(mask, wt, 0.0)
            acc_s[c] = acc_s[c] + _dot(v_s[c, i], wt.astype(_BF16))
            c_s[c] = c_s[c] + rest_from_here[0:1, :]

    def carry_max():
        return functools.reduce(jnp.maximum, [jnp.max(c_s[c]) for c in range(groups)])

    top = n_past + qi + 1

    def store(slot, blocks):
        for c, zt in enumerate(blocks):
            s_s[slot, c] = zt

    def step(slot, i, nxt_i, mask=None):
        nxt = logits(nxt_i)
        update(slot, i, mask)
        store(1 - slot, nxt)

    kpos, qpos = _diag_offsets(tq)
    store(0, logits(top))
    step(0, top, top - 1, kpos < qpos)

    def pair(state):
        i = state[0]
        step(1, i, i - 1)
        step(0, i - 1, jnp.maximum(i - 2, 0))
        return i - 2, carry_max()

    lax.while_loop(lambda s: (s[0] >= 1) & (s[1] > _EXP_ZERO_BELOW), pair, (top - 1, carry_max()))

    row = lax.broadcasted_iota(jnp.int32, (_LANES, tq), 0)
    for c in range(groups):
        acc = acc_s[c]
        out = jnp.where(row < _HEAD_DIM, acc[:, :tq], acc[:, tq:])
        o_ref[:, c * _LANES:(c + 1) * _LANES] = out.T.astype(o_ref.dtype)


def _attn_call(kind, qt, kt, vt, past, seq, extra_inputs, **kernel_kwargs):
    batch, d, seq_pad = qt.shape
    tq = min(_ATTN_TILE, seq_pad)
    assert seq_pad % tq == 0 and tq % _LANES == 0
    nq = seq_pad // tq
    groups = _ATTN_GROUPS
    gw = groups * _LANES
    assert d % gw == 0
    const = lambda a: pl.BlockSpec(a.shape, lambda b, g, i: (0,) * a.ndim)
    q_spec = pl.BlockSpec((1, gw, tq), lambda b, g, i: (b, g, i))
    kv_spec = pl.BlockSpec((1, gw, seq_pad), lambda b, g, i: (b, g, 0))
    inputs = list(extra_inputs) + [qt, kt, vt]
    in_specs = [const(a) for a in extra_inputs] + [q_spec, kv_spec, kv_spec]
    n_past = 0
    if past is not None:
        pkt, pv = past
        past_len = pkt.shape[2]
        assert past_len % tq == 0 and past_len % _CHUNK == 0
        n_past = past_len // tq
        inputs += [pkt, pv]
        in_specs.append(pl.BlockSpec((1, gw, past_len), lambda b, g, i: (b, g, 0)))
        if kind == "a":
            in_specs.append(pl.BlockSpec((past_len, gw), lambda b, g, i: (b, g)))
        else:
            in_specs.append(pl.BlockSpec((1, gw, past_len), lambda b, g, i: (b, g, 0)))
    n_blocks = n_past + nq + (1 if kind == "b" else 0)
    scratch = [
        pltpu.VMEM((groups, n_blocks, tq, _LANES), _BF16),
        pltpu.VMEM((groups, n_blocks, _LANES, tq), _BF16),
        pltpu.VMEM((groups, _LANES, 2 * tq), _BF16),
        pltpu.VMEM((2, groups, tq, 2 * tq), _F32),
    ]
    row_state = pltpu.VMEM((groups, 1, 2 * tq), _F32)
    acc_state = pltpu.VMEM((groups, _LANES, 2 * tq), _F32)
    if kind == "a":
        body = functools.partial(_attn_a_kernel, tq=tq, n_past=n_past, groups=groups, seq=seq, **kernel_kwargs)
        scratch += [row_state, row_state, acc_state]
    else:
        body = functools.partial(_attn_b_kernel, tq=tq, n_past=n_past, groups=groups, seq=seq)
        scratch += [pltpu.VMEM((tq, 2 * tq), _BF16), row_state, acc_state]
    return pl.pallas_call(
        body,
        grid=(batch, d // gw, nq),
        in_specs=in_specs,
        out_specs=pl.BlockSpec((tq, gw), lambda b, g, i: (b * nq + i, g)),
        out_shape=jax.ShapeDtypeStruct((batch * seq_pad, d), _BF16),
        scratch_shapes=scratch,
        compiler_params=_compiler_params(("arbitrary", "arbitrary", "arbitrary")),
        name="attn_" + kind,
    )(*inputs)


def _router_gate(logits, n_experts, n_groups):
    lane_i = lax.broadcasted_iota(jnp.int32, logits.shape, 1)
    lane = lane_i.astype(_F32)
    group_of_lane = (lane_i >> (_EXP_PER_GROUP.bit_length() - 1)).astype(_F32)
    big = float(4 * _LANES)
    row_max = lambda v: jnp.max(v, axis=1, keepdims=True)
    row_min = lambda v: jnp.min(v, axis=1, keepdims=True)
    row_sum = lambda v: jnp.sum(v, axis=1, keepdims=True)

    is_group = (lane_i >= n_experts) & (lane_i < n_experts + n_groups)
    lg = jnp.where(is_group, logits, _NEG_INF)
    eg = jnp.where(is_group, jnp.exp(lg - row_max(lg)), 0.0)
    pg = eg / row_sum(eg)
    pg_sel = row_max(pg)
    g_sel = row_min(jnp.where(is_group & (pg == pg_sel), lane - n_experts, big))

    in_group = (lane_i < n_experts) & (group_of_lane == g_sel)
    le = jnp.where(in_group, logits, _NEG_INF)
    ee = jnp.where(in_group, jnp.exp(le - row_max(le)), 0.0)
    pe = ee / row_sum(ee)
    p1 = row_max(pe)
    i1 = row_min(jnp.where(in_group & (pe == p1), lane, big))
    rest = in_group & (lane != i1)
    p2 = row_max(jnp.where(rest, pe, -1.0))
    i2 = row_min(jnp.where(rest & (pe == p2), lane, big))
    total = p1 + p2
    gate = jnp.where(lane == i1, p1 / total * pg_sel, 0.0) + jnp.where(lane == i2, p2 / total * pg_sel, 0.0)
    return jnp.where(lane_i == n_experts, g_sel, gate)


def _outproj_kernel(mod_ref, x_ref, oa_ref, ob_ref, n1_ref, n2_ref, wg_ref, woa_ref, wob_ref, wout_ref,
                    wr_hi_ref, wr_lo_ref, br_ref, x1_ref, h2_ref, gate_ref, *, seq, n_batch, n_experts, n_groups):
    d = x_ref.shape[1]
    rows = lambda k: _mod_rows(mod_ref, k, d, seq, n_batch)
    x = x_ref[...]
    h = _adaln(x, n1_ref[...], rows(1), rows(0)).astype(_BF16)
    gate_a = _sigmoid(_dot(h, wg_ref[:, :d]))
    gate_b = _sigmoid(_dot(h, wg_ref[:, d:]))
    mix = gate_a * _dot(oa_ref[...], woa_ref[...]) + gate_b * _dot(ob_ref[...], wob_ref[...])
    x1 = x + rows(2) * _dot(mix.astype(_BF16), wout_ref[...])
    x1_ref[...] = x1
    h2 = _adaln(x1, n2_ref[...], rows(4), rows(3))
    h2_ref[...] = h2.astype(_BF16)
    h2_hi, h2_lo = _split(h2)
    logits = _dot(h2_hi, wr_hi_ref[...]) + (_dot(h2_hi, wr_lo_ref[...]) + _dot(h2_lo, wr_hi_ref[...])) + br_ref[...]
    gate_ref[...] = _router_gate(logits, n_experts, n_groups)


def _outproj_call(x2, oa, ob, mod3, mod_off, seq, norm1_g, norm2_g, w_gate_bf, w_oa_bf, w_ob_bf, w_out_bf,
                  w_router, b_router, n_experts, n_groups):
    n, d = x2.shape
    tm, n_batch, tiles_per_batch = _token_tiling(n, seq, _OUTPROJ_TILE)
    wr_hi, wr_lo = _split(w_router)
    row_spec = lambda: pl.BlockSpec((tm, d), lambda i: (i, 0))
    full = lambda a: pl.BlockSpec(a.shape, lambda i: (0,) * a.ndim, pipeline_mode=pl.Buffered(1))
    return pl.pallas_call(
        functools.partial(_outproj_kernel, seq=seq, n_batch=n_batch, n_experts=n_experts, n_groups=n_groups),
        grid=(n // tm,),
        in_specs=[
            _mod_spec(mod3.shape[2], n_batch, tiles_per_batch, mod_off),
            row_spec(), row_spec(), row_spec(),
            pl.BlockSpec((1, d), lambda i: (0, 0)), pl.BlockSpec((1, d), lambda i: (0, 0)),
            full(w_gate_bf), full(w_oa_bf), full(w_ob_bf), full(w_out_bf), full(wr_hi), full(wr_lo),
            pl.BlockSpec((1, _LANES), lambda i: (0, 0)),
        ],
        out_specs=[row_spec(), row_spec(), pl.BlockSpec((tm, _LANES), lambda i: (i, 0))],
        out_shape=[
            jax.ShapeDtypeStruct((n, d), _F32),
            jax.ShapeDtypeStruct((n, d), _BF16),
            jax.ShapeDtypeStruct((n, _LANES), _F32),
        ],
        compiler_params=_compiler_params(("arbitrary",)),
        name="outproj",
    )(mod3, x2, oa, ob, norm1_g.reshape(1, d), norm2_g.reshape(1, d), w_gate_bf, w_oa_bf, w_ob_bf, w_out_bf,
      wr_hi, wr_lo, b_router)


def _split3(x):
    hi = x.astype(_BF16)
    r = x - hi.astype(_F32)
    mid = r.astype(_BF16)
    return hi, mid, (r - mid.astype(_F32)).astype(_BF16)


def _moe_chunk_rows(tm, n_groups):
    return -(-int(tm / n_groups * _MOE_CHUNK_SLACK) // 32) * 32


def _moe_kernel(mod_ref, x1_ref, h2_ref, gate_ref, w1_ref, w3_ref, w2_ref, y_ref,
                gpack_s, rank_col_s, rank_row_s, count_s, *, seq, n_batch, n_experts, n_groups):
    g = pl.program_id(1)
    tm, d = x1_ref.shape
    chunk = _moe_chunk_rows(tm, n_groups)
    lane = lax.broadcasted_iota(jnp.int32, (tm, _LANES), 1)
    assert 3 * n_experts <= _LANES and n_experts & (n_experts - 1) == 0

    @pl.when(g == 0)
    def _():
        y_ref[...] = jnp.zeros(y_ref.shape, _F32)
        gate = gate_ref[...]
        hi, mid, lo = [p.astype(_F32) for p in _split3(jnp.where(lane < n_experts, gate, 0.0))]
        gpack_s[...] = (hi + pltpu.roll(mid, n_experts, 1) + pltpu.roll(lo, 2 * n_experts, 1)).astype(_BF16)
        group_id = gate[:, n_experts:n_experts + 1]
        member = jnp.where((lane < n_groups) & (lane.astype(_F32) == group_id), 1.0, 0.0)
        t_idx = lax.broadcasted_iota(jnp.int32, (tm, tm), 0)
        u_idx = lax.broadcasted_iota(jnp.int32, (tm, tm), 1)
        earlier = jnp.where(u_idx < t_idx, 1.0, 0.0).astype(_BF16)
        rank_col = jnp.where(member > 0.0, _dot(earlier, member.astype(_BF16)), -1.0)
        rank_col_s[...] = rank_col
        rank_row = rank_col.T
        for k in range(n_groups):
            rank_row_s[k] = jnp.broadcast_to(rank_row[k:k + 1, :], (_SUBLANES, tm))
        count_s[...] = jnp.sum(member, axis=0, keepdims=True)

    lane_row = lax.broadcasted_iota(jnp.int32, (1, _LANES), 1)
    count = jnp.sum(jnp.where(lane_row == g, count_s[...], 0.0)).astype(jnp.int32)
    n_chunks = (count + (chunk - 1)) // chunk
    rank_c = jnp.sum(jnp.where(lane == g, rank_col_s[...], 0.0), axis=1, keepdims=True)
    rank_r = jnp.concatenate([rank_row_s[g]] * (chunk // _SUBLANES), axis=0)
    slot_r = lax.broadcasted_iota(jnp.int32, (chunk, tm), 0).astype(_F32)
    slot_c = lax.broadcasted_iota(jnp.int32, (tm, chunk), 1).astype(_F32)
    lane_c = lax.broadcasted_iota(jnp.int32, (chunk, _LANES), 1)
    experts_per_group = w1_ref.shape[0]

    def chunk_body(k, carry):
        base = (k * chunk).astype(_F32)
        pick = jnp.where(rank_r == slot_r + base, 1.0, 0.0).astype(_BF16)
        put = jnp.where(rank_c == slot_c + base, 1.0, 0.0).astype(_BF16)
        xc = _dot(pick, h2_ref[...]).astype(_BF16)
        gc = _dot(pick, gpack_s[...])
        acc = jnp.zeros((chunk, d), _F32)
        for e in range(experts_per_group):
            mine = (lane_c & (n_experts - 1)) == g * experts_per_group + e
            ge = jnp.sum(jnp.where(mine, gc, 0.0), axis=1, keepdims=True)
            hid = _silu(_dot(xc, w1_ref[e])) * _dot(xc, w3_ref[e])
            acc = acc + _dot((hid * ge).astype(_BF16), w2_ref[e])
        y_ref[...] += _dot(put, acc.astype(_BF16))
        return carry

    lax.fori_loop(0, n_chunks, chunk_body, 0)

    @pl.when(g == pl.num_programs(1) - 1)
    def _():
        y_ref[...] = x1_ref[...] + _mod_rows(mod_ref, 5, d, seq, n_batch) * y_ref[...]


def _moe_call(x1, h2, gate, mod3, mod_off, seq, w1_bf, w3_bf, w2_bf, n_groups):
    n, d = x1.shape
    tm, n_batch, tiles_per_batch = _token_tiling(n, seq, _MOE_TILE)
    n_experts, _, d_expert = w1_bf.shape
    eg = n_experts // n_groups
    assert eg == _EXP_PER_GROUP
    row_spec = lambda w, **kw: pl.BlockSpec((tm, w), lambda i, g: (i, 0), **kw)
    return pl.pallas_call(
        functools.partial(_moe_kernel, seq=seq, n_batch=n_batch, n_experts=n_experts, n_groups=n_groups),
        grid=(n // tm, n_groups),
        in_specs=[
            _mod_spec(mod3.shape[2], n_batch, tiles_per_batch, mod_off),
            row_spec(d, pipeline_mode=pl.Buffered(1)), row_spec(d), row_spec(_LANES),
            pl.BlockSpec((eg, d, d_expert), lambda i, g: (g, 0, 0)),
            pl.BlockSpec((eg, d, d_expert), lambda i, g: (g, 0, 0)),
            pl.BlockSpec((eg, d_expert, d), lambda i, g: (g, 0, 0)),
        ],
        out_specs=row_spec(d),
        out_shape=jax.ShapeDtypeStruct((n, d), _F32),
        scratch_shapes=[
            pltpu.VMEM((tm, _LANES), _BF16),
            pltpu.VMEM((tm, _LANES), _F32),
            pltpu.VMEM((n_groups, _SUBLANES, tm), _F32),
            pltpu.VMEM((1, _LANES), _F32),
        ],
        compiler_params=_compiler_params(("arbitrary", "arbitrary"), _MOE_VMEM_LIMIT_BYTES),
        name="moe",
    )(mod3, x1, h2, gate, w1_bf, w3_bf, w2_bf)


def _feature_major(c):
    b, t, h, dh = c.shape
    return jnp.transpose(c, (0, 2, 3, 1)).reshape(b, h * dh, t)


def _token_major(ct, head_dim):
    b, d, t = ct.shape
    return jnp.transpose(ct.reshape(b, d // head_dim, head_dim, t), (0, 3, 1, 2))


def _layer(x, mod3, mod_off, lam, lam_init, past, p):
    batch, seq, d = x.shape
    x2 = x.reshape(batch * seq, d)
    past_len = 0 if past is None else past[0].shape[1]
    pos = past_len + jnp.arange(seq)
    qa_t, ka_t, va, va_t, qs_t, ks_t, vs_t = _proj_call(
        x2, mod3, mod_off, batch, seq, pos, p["norm1_g"], p["w_qkv_t"], p["gq"], p["gk"])
    seq_pad = -(-seq // _LANES) * _LANES
    pad = lambda a: a if seq_pad == seq else jnp.pad(a, ((0, 0), (0, 0), (0, seq_pad - seq)))
    past_a = past_b = None
    if past is not None:
        past_a = (_feature_major(past[0]), past[1].reshape(batch * past_len, d))
        past_b = (_feature_major(past[2]), _feature_major(past[3]))
    oa = _attn_call("a", pad(qa_t), pad(ka_t), pad(va_t), past_a, seq, [lam, p["subln_g"]], lam_init=lam_init)
    ob = _attn_call("b", pad(qs_t), pad(ks_t), pad(vs_t), past_b, seq, [])
    if seq_pad != seq:
        unpad = lambda o: o.reshape(batch, seq_pad, d)[:, :seq].reshape(batch * seq, d)
        oa, ob = unpad(oa), unpad(ob)
    x1, h2, gate = _outproj_call(x2, oa, ob, mod3, mod_off, seq, p["norm1_g"], p["norm2_g"], p["w_gate"],
                                 p["w_oa"], p["w_ob"], p["w_out"], p["w_router"], p["b_router"],
                                 p["n_experts"], p["n_groups"])
    y = _moe_call(x1, h2, gate, mod3, mod_off, seq, p["w1"], p["w3"], p["w2"], p["n_groups"])
    new = (_token_major(ka_t, _HEAD_DIM), va.reshape(batch, seq, -1, 2 * _HEAD_DIM),
           _token_major(ks_t, _HEAD_DIM), _token_major(vs_t, _HEAD_DIM))
    return y.reshape(batch, seq, d), new


def kernel(x_prompt, x_sample, cache_a_k, cache_a_v, cache_b_k, cache_b_v, c_prompt, c_sample, norm1_g, norm2_g, w_ada, b_ada, w_in, a_qnorm_g, a_knorm_g, a_lam_q1, a_lam_k1, a_lam_q2, a_lam_k2, a_subln_g, w_oa, w_ob, w_out, w_rg, b_rg, w_re, b_re, w1, w3, w2):
    depth, d = norm1_g.shape
    n_groups, n_experts = w_rg.shape[2], w_re.shape[2]
    assert w_in.shape[2] == 8 * d and n_experts + n_groups <= _LANES
    batch_p = x_prompt.shape[0]
    xp, xs = x_prompt, x_sample
    c_all = jnp.concatenate([c_prompt, c_sample], axis=0)
    rows_p, rows_s = [], []
    for l in range(depth):
        lam_init = 0.8 - 0.6 * math.exp(-0.3 * l)
        lam_vecs = jnp.stack([a_lam_q1[l], a_lam_k1[l], a_lam_q2[l], a_lam_k2[l]])
        mod, lam = _ada_call(c_all, w_ada[l], b_ada[l], lam_vecs, lam_init)
        mod3 = mod.reshape(mod.shape[0], 1, mod.shape[1])
        pad = _LANES - n_experts - n_groups
        params = dict(
            norm1_g=norm1_g[l], norm2_g=norm2_g[l],
            w_qkv_t=w_in[l][:, :6 * d].T.astype(_BF16), w_gate=w_in[l][:, 6 * d:].astype(_BF16),
            gq=a_qnorm_g[l], gk=a_knorm_g[l],
            subln_g=jnp.broadcast_to(a_subln_g[l].reshape(-1, 1), (a_subln_g.shape[1], _LANES)),
            w_oa=w_oa[l].astype(_BF16), w_ob=w_ob[l].astype(_BF16), w_out=w_out[l].astype(_BF16),
            w_router=jnp.pad(jnp.concatenate([w_re[l], w_rg[l]], axis=1), ((0, 0), (0, pad))),
            b_router=jnp.pad(jnp.concatenate([b_re[l], b_rg[l]]), (0, pad)).reshape(1, _LANES),
            w1=w1[l].astype(_BF16), w3=w3[l].astype(_BF16), w2=w2[l].astype(_BF16),
            n_experts=n_experts, n_groups=n_groups,
        )
        xp, new_p = _layer(xp, mod3, 0, lam, lam_init, None, params)
        past = (cache_a_k[l], cache_a_v[l], cache_b_k[l], cache_b_v[l])
        xs, new_s = _layer(xs, mod3, batch_p, lam, lam_init, past, params)
        rows_p.append(new_p)
        rows_s.append(new_s)
    stack = lambda rows, k: jnp.stack([r[k] for r in rows])
    return (xp, xs, stack(rows_p, 0), stack(rows_p, 1), stack(rows_p, 2), stack(rows_p, 3),
            stack(rows_s, 0), stack(rows_s, 1), stack(rows_s, 2), stack(rows_s, 3))
```

```python
import functools
import math

import jax
import jax.numpy as jnp
from jax import lax
from jax.experimental import pallas as pl
from jax.experimental.pallas import tpu as pltpu

_F32 = jnp.float32
_BF16 = jnp.bfloat16

_LANES = 128
_SUBLANES = 8
_MXU_DIM = 256
_VMEM_LIMIT_BYTES = 48 * 1024 * 1024

_CHUNK = 64
_HEAD_DIM = 64
_ROPE_DIMS = _HEAD_DIM // 4
_ROPE_THETA = 500000.0
_EXP_PER_GROUP = 8
_EPS = 1e-6
_NEG_INF = -1e30
_Q_SCALE = 1.0 / math.sqrt(_HEAD_DIM)
_EXP_ZERO_BELOW = -104.0
_LOG2_E = math.log2(math.e)

_TOKEN_TILE = 512
_OUTPROJ_TILE = 512
_ATTN_TILE = _MXU_DIM
_ATTN_GROUPS = 2
_MOE_TILE = 1024
_MOE_CHUNK_SLACK = 1.25
_MOE_VMEM_LIMIT_BYTES = 56 * 1024 * 1024


def _dot(a, b):
    return jnp.dot(a, b, preferred_element_type=_F32)


def _split(x):
    hi = x.astype(_BF16)
    lo = (x - hi.astype(_F32)).astype(_BF16)
    return hi, lo


def _dot3(a, b):
    a_hi, a_lo = _split(a)
    b_hi, b_lo = _split(b)
    return _dot(a_hi, b_hi) + (_dot(a_hi, b_lo) + _dot(a_lo, b_hi))


def _silu(x):
    return x / (1.0 + jnp.exp(-x))


def _sigmoid(x):
    return 1.0 / (1.0 + jnp.exp(-x))


def _rms(x, axis=-1):
    return x * lax.rsqrt(jnp.mean(x * x, axis=axis, keepdims=True) + _EPS)


def _mod_rows(mod_ref, k, d, seq, n_batch):
    if n_batch == 1:
        return mod_ref[0, :, k * d:(k + 1) * d]
    rows = [jnp.broadcast_to(mod_ref[b, :, k * d:(k + 1) * d], (seq, d)) for b in range(n_batch)]
    return jnp.concatenate(rows, axis=0)


def _lane_tile(x, n):
    assert n % _LANES == 0
    return jnp.concatenate([x] * (n // _LANES), axis=1) if n > _LANES else x


def _compiler_params(semantics, vmem_limit_bytes=_VMEM_LIMIT_BYTES):
    return pltpu.CompilerParams(dimension_semantics=semantics, vmem_limit_bytes=vmem_limit_bytes)


def _ada_kernel(c_ref, w_ref, b_ref, lam_ref, mod_ref, lam_out_ref, *, lam_init):
    mod_ref[...] = _dot3(_silu(c_ref[...]), w_ref[...]) + b_ref[...]
    lv = lam_ref[...]
    s1 = jnp.sum(lv[0:1] * lv[1:2], axis=-1, keepdims=True)
    s2 = jnp.sum(lv[2:3] * lv[3:4], axis=-1, keepdims=True)
    lam = jnp.exp(s1) - jnp.exp(s2) + lam_init
    lam_out_ref[...] = jnp.broadcast_to(lam, lam_out_ref.shape)


def _ada_call(c_all, w_ada, b_ada, lam_vecs, lam_init):
    rows, d = c_all.shape
    cols = w_ada.shape[1]
    tn = d
    return pl.pallas_call(
        functools.partial(_ada_kernel, lam_init=lam_init),
        grid=(cols // tn,),
        in_specs=[
            pl.BlockSpec((rows, d), lambda j: (0, 0)),
            pl.BlockSpec((d, tn), lambda j: (0, j)),
            pl.BlockSpec((1, tn), lambda j: (0, j)),
            pl.BlockSpec(lam_vecs.shape, lambda j: (0, 0)),
        ],
        out_specs=[
            pl.BlockSpec((rows, tn), lambda j: (0, j)),
            pl.BlockSpec((_SUBLANES, _LANES), lambda j: (0, 0)),
        ],
        out_shape=[
            jax.ShapeDtypeStruct((rows, cols), _F32),
            jax.ShapeDtypeStruct((_SUBLANES, _LANES), _F32),
        ],
        compiler_params=_compiler_params(("arbitrary",)),
        name="ada",
    )(c_all, w_ada, b_ada.reshape(1, cols), lam_vecs)


def _adaln(x, gain, scale, shift):
    return (_rms(x) * gain) * (1.0 + scale) + shift


def _proj_kernel(mod_ref, x_ref, n1_ref, wt_ref, gq_ref, gk_ref, cos_ref, sin_ref,
                 qa_ref, ka_ref, va_ref, vat_ref, qs_ref, ks_ref, vs_ref, ht_s, *, seq, n_batch):
    tm, d = x_ref.shape
    shift = _mod_rows(mod_ref, 0, d, seq, n_batch)
    scale = _mod_rows(mod_ref, 1, d, seq, n_batch)
    ht_s[...] = _adaln(x_ref[...], n1_ref[...], scale, shift).T.astype(_BF16)

    def proj_t(section):
        return _dot(wt_ref[section * d:(section + 1) * d, :], ht_s[...])

    def store_t(out_ref, val):
        if n_batch == 1:
            out_ref[0] = val.astype(out_ref.dtype)
        else:
            for b in range(n_batch):
                out_ref[b] = val[:, b * seq:(b + 1) * seq].astype(out_ref.dtype)

    def head_norm_rope(acc, gain_ref, out_ref, out_scale):
        gain = _lane_tile(gain_ref[...], tm)
        cos, sin = cos_ref[...], sin_ref[...]
        half = _ROPE_DIMS // 2
        parts = []
        for h in range(d // _HEAD_DIM):
            y = _rms(acc[h * _HEAD_DIM:(h + 1) * _HEAD_DIM], axis=0) * gain
            x1, x2 = y[:half], y[half:2 * half]
            parts += [x1 * cos - x2 * sin, x2 * cos + x1 * sin, y[2 * half:]]
        out = jnp.concatenate(parts, axis=0)
        store_t(out_ref, out if out_scale == 1.0 else out * out_scale)

    head_norm_rope(proj_t(0), gq_ref, qa_ref, _Q_SCALE * _LOG2_E)
    head_norm_rope(proj_t(1), gk_ref, ka_ref, 1.0)
    va_t = proj_t(2)
    store_t(vat_ref, va_t)
    va_ref[...] = va_t.T
    store_t(qs_ref, proj_t(3) * _Q_SCALE)
    store_t(ks_ref, proj_t(4))
    store_t(vs_ref, proj_t(5))


def _rope_tables_t(pos):
    half = _ROPE_DIMS // 2
    inv_freq = jnp.exp(-math.log(_ROPE_THETA) * 2.0 * jnp.arange(half, dtype=_F32) / _ROPE_DIMS)
    ang = inv_freq[:, None] * pos.astype(_F32)[None, :]
    return jnp.cos(ang), jnp.sin(ang)


def _token_tiling(n_tokens, seq, tile=_TOKEN_TILE):
    tm = min(tile, n_tokens)
    if tm >= seq:
        assert tm % seq == 0 and n_tokens % tm == 0
        return tm, tm // seq, 1
    assert seq % tm == 0
    return tm, 1, seq // tm


def _mod_spec(width, n_batch, tiles_per_batch, mod_off):
    assert mod_off % n_batch == 0
    first = mod_off // n_batch
    return pl.BlockSpec((n_batch, 1, width), lambda i, *_: (first + i // tiles_per_batch, 0, 0))


def _proj_call(x2, mod3, mod_off, batch, seq, pos, norm1_g, w_qkv_t, gq, gk):
    n, d = x2.shape
    tm, n_batch, tiles_per_batch = _token_tiling(n, seq)
    cos, sin = _rope_tables_t(pos)
    if n_batch > 1:
        cos, sin = jnp.tile(cos, (1, n_batch)), jnp.tile(sin, (1, n_batch))
    lanes_t = tm if n_batch == 1 else seq
    gain_t = lambda g: jnp.broadcast_to(g.reshape(_HEAD_DIM, 1), (_HEAD_DIM, _LANES))
    row_spec = pl.BlockSpec((tm, d), lambda i: (i, 0))
    t_spec = lambda: pl.BlockSpec((n_batch, d, lanes_t), lambda i: (i // tiles_per_batch, 0, i % tiles_per_batch))
    rope_spec = lambda: pl.BlockSpec((_ROPE_DIMS // 2, tm), lambda i: (0, i % tiles_per_batch))
    const = lambda shape: pl.BlockSpec(shape, lambda i: (0, 0))
    t_out = lambda dt: jax.ShapeDtypeStruct((batch, d, seq), dt)
    return pl.pallas_call(
        functools.partial(_proj_kernel, seq=seq, n_batch=n_batch),
        grid=(n // tm,),
        in_specs=[
            _mod_spec(mod3.shape[2], n_batch, tiles_per_batch, mod_off),
            row_spec,
            const((1, d)),
            pl.BlockSpec(w_qkv_t.shape, lambda i: (0, 0), pipeline_mode=pl.Buffered(1)),
            const((_HEAD_DIM, _LANES)), const((_HEAD_DIM, _LANES)),
            rope_spec(), rope_spec(),
        ],
        out_specs=[t_spec(), t_spec(), row_spec, t_spec(), t_spec(), t_spec(), t_spec()],
        out_shape=[t_out(_BF16), t_out(_F32), jax.ShapeDtypeStruct((n, d), _F32), t_out(_BF16),
                   t_out(_BF16), t_out(_F32), t_out(_F32)],
        scratch_shapes=[pltpu.VMEM((d, tm), _BF16)],
        compiler_params=_compiler_params(("arbitrary",)),
        name="proj",
    )(mod3, x2, norm1_g.reshape(1, d), w_qkv_t, gain_t(gq), gain_t(gk), cos, sin)


def _stack_heads_t(qt):
    row = lax.broadcasted_iota(jnp.int32, qt.shape, 0)
    zero = jnp.zeros_like(qt)
    return jnp.concatenate([jnp.where(row < _HEAD_DIM, qt, zero), jnp.where(row >= _HEAD_DIM, qt, zero)], axis=1)


def _diag_offsets(tq):
    assert tq & (tq - 1) == 0
    kpos = lax.broadcasted_iota(jnp.int32, (tq, 2 * tq), 0)
    qpos = lax.broadcasted_iota(jnp.int32, (tq, 2 * tq), 1) & (tq - 1)
    return kpos, qpos


def _load_kv_blocks(groups, tq, n_past, first, kt_ref, vt_ref, past_refs, past_values_token_major, k_s, v_s):
    g = _LANES
    for c in range(groups):
        rows = slice(c * g, (c + 1) * g)
        for jb in range(n_past):
            pk_ref, pv_ref = past_refs
            cols = slice(jb * tq, (jb + 1) * tq)
            k_s[c, first + jb] = pk_ref[0, rows, cols].T.astype(_BF16)
            if past_values_token_major:
                v_s[c, first + jb] = pv_ref[cols, rows].T.astype(_BF16)
            else:
                v_s[c, first + jb] = pv_ref[0, rows, cols].astype(_BF16)
        for jb in range(k_s.shape[1] - n_past - first):
            cols = slice(jb * tq, (jb + 1) * tq)
            k_s[c, first + n_past + jb] = kt_ref[0, rows, cols].T.astype(_BF16)
            v_s[c, first + n_past + jb] = vt_ref[0, rows, cols].astype(_BF16)


def _attn_a_kernel(*refs, tq, n_past, groups, seq, lam_init):
    if n_past:
        lam_ref, subg_ref, qt_ref, kt_ref, vt_ref, pk_ref, pv_ref, o_ref, k_s, v_s, qq_s, s_s, m_s, l_s, acc_s = refs
        past_refs = (pk_ref, pv_ref)
    else:
        lam_ref, subg_ref, qt_ref, kt_ref, vt_ref, o_ref, k_s, v_s, qq_s, s_s, m_s, l_s, acc_s = refs
        past_refs = None
    qi = pl.program_id(2)

    @pl.when(qi == 0)
    def _():
        _load_kv_blocks(groups, tq, n_past, 0, kt_ref, vt_ref, past_refs, True, k_s, v_s)

    for c in range(groups):
        qq_s[c] = _stack_heads_t(qt_ref[0, c * _LANES:(c + 1) * _LANES, :])
    m_s[...] = jnp.full(m_s.shape, _NEG_INF, _F32)
    l_s[...] = jnp.zeros(l_s.shape, _F32)
    acc_s[...] = jnp.zeros(acc_s.shape, _F32)

    def scores(i):
        return [_dot(k_s[c, i], qq_s[c]) for c in range(groups)]

    def store(slot, blocks):
        for c, st in enumerate(blocks):
            s_s[slot, c] = st

    def update(slot, i, mask):
        for c in range(groups):
            st = s_s[slot, c]
            if mask is not None:
                st = jnp.where(mask, st, _NEG_INF)
            m_prev = m_s[c]
            m_next = jnp.maximum(m_prev, jnp.max(st, axis=0, keepdims=True))
            alpha = jnp.exp2(m_prev - m_next)
            pt = jnp.exp2(st - m_next)
            l_s[c] = alpha * l_s[c] + jnp.sum(pt, axis=0, keepdims=True)
            m_s[c] = m_next
            acc_s[c] = alpha * acc_s[c] + _dot(v_s[c, i], pt.astype(_BF16))

    def step(slot, i, nxt_i, mask=None):
        nxt = scores(nxt_i)
        update(slot, i, mask)
        store(1 - slot, nxt)

    last = n_past + qi
    kpos, qpos = _diag_offsets(tq)
    chunk_bits = _CHUNK.bit_length() - 1
    mask = (kpos >> chunk_bits) <= (qpos >> chunk_bits)
    if seq < tq:
        mask = mask & (kpos < seq)
    store(0, scores(last))
    step(0, last, 0, mask)

    def pair(j, carry):
        step(1, 2 * j, 2 * j + 1)
        step(0, 2 * j + 1, jnp.minimum(2 * j + 2, last))
        return carry

    lax.fori_loop(0, last // 2, pair, 0)

    @pl.when((last & 1) == 1)
    def _():
        update(1, last - 1, None)

    gain = _lane_tile(subg_ref[...], tq)
    for c in range(groups):
        out = acc_s[c] / l_s[c]
        o = out[:, :tq] - lam_ref[0:1, 0:1] * out[:, tq:]
        y = (_rms(o, axis=0) * gain) * (1.0 - lam_init)
        o_ref[:, c * _LANES:(c + 1) * _LANES] = y.T.astype(o_ref.dtype)


def _attn_b_kernel(*refs, tq, n_past, groups, seq):
    if n_past:
        qt_ref, kt_ref, vt_ref, pk_ref, pv_ref, o_ref, k_s, v_s, qq_s, s_s, u_s, c_s, acc_s = refs
        past_refs = (pk_ref, pv_ref)
    else:
        qt_ref, kt_ref, vt_ref, o_ref, k_s, v_s, qq_s, s_s, u_s, c_s, acc_s = refs
        past_refs = None
    qi = pl.program_id(2)

    @pl.when(qi == 0)
    def _():
        for c in range(groups):
            k_s[c, 0] = jnp.zeros(k_s.shape[2:], _BF16)
            v_s[c, 0] = jnp.zeros(v_s.shape[2:], _BF16)
        _load_kv_blocks(groups, tq, n_past, 1, kt_ref, vt_ref, past_refs, False, k_s, v_s)

    for c in range(groups):
        qq_s[c] = _stack_heads_t(qt_ref[0, c * _LANES:(c + 1) * _LANES, :])
    s_idx = lax.broadcasted_iota(jnp.int32, (tq, 2 * tq), 0)
    j_idx = lax.broadcasted_iota(jnp.int32, (tq, 2 * tq), 1) & (tq - 1)
    u_s[...] = jnp.where(j_idx >= s_idx, -1.0, 0.0).astype(_BF16)
    c_s[...] = jnp.zeros(c_s.shape, _F32)
    acc_s[...] = jnp.zeros(acc_s.shape, _F32)

    def logits(i):
        return [_dot(k_s[c, i], qq_s[c]) for c in range(groups)]

    def update(slot, i, mask):
        rests = []
        for c in range(groups):
            zt = s_s[slot, c]
            neg_log_rest = jnp.maximum(zt, 0.0) + jnp.log(1.0 + jnp.exp2(jnp.abs(zt) * (-_LOG2_E)))
            if mask is not None:
                neg_log_rest = jnp.where(mask, neg_log_rest, 0.0)
            hi, lo = _split(neg_log_rest)
            rests.append(_dot(u_s[...], jnp.concatenate([hi, lo], axis=0)))
        for c, rest_from_here in enumerate(rests):
            wt = jnp.exp(s_s[slot, c] + rest_from_here + c_s[c])
            if mask is not None:
                wt = jnp.where(mask, wt, 0.0)
            acc_s[c] = acc_s[c] + _dot(v_s[c, i], wt.astype(_BF16))
            c_s[c] = c_s[c] + rest_from_here[0:1, :]

    def carry_max():
        return functools.reduce(jnp.maximum, [jnp.max(c_s[c]) for c in range(groups)])

    top = n_past + qi + 1

    def store(slot, blocks):
        for c, zt in enumerate(blocks):
            s_s[slot, c] = zt

    def step(slot, i, nxt_i, mask=None):
        nxt = logits(nxt_i)
        update(slot, i, mask)
        store(1 - slot, nxt)

    kpos, qpos = _diag_offsets(tq)
    store(0, logits(top))
    step(0, top, top - 1, kpos < qpos)
    step(1, top - 1, jnp.maximum(top - 2, 0))

    def pair(state):
        i = state[0]
        step(0, i, i - 1)
        step(1, i - 1, jnp.maximum(i - 2, 0))
        return i - 2, carry_max()

    lax.while_loop(lambda s: (s[0] >= 1) & (s[1] > _EXP_ZERO_BELOW), pair, (top - 2, carry_max()))

    row = lax.broadcasted_iota(jnp.int32, (_LANES, tq), 0)
    for c in range(groups):
        acc = acc_s[c]
        out = jnp.where(row < _HEAD_DIM, acc[:, :tq], acc[:, tq:])
        o_ref[:, c * _LANES:(c + 1) * _LANES] = out.T.astype(o_ref.dtype)


def _attn_call(kind, qt, kt, vt, past, seq, extra_inputs, **kernel_kwargs):
    batch, d, seq_pad = qt.shape
    tq = min(_ATTN_TILE, seq_pad)
    assert seq_pad % tq == 0 and tq % _LANES == 0
    nq = seq_pad // tq
    groups = _ATTN_GROUPS
    gw = groups * _LANES
    assert d % gw == 0
    const = lambda a: pl.BlockSpec(a.shape, lambda b, g, i: (0,) * a.ndim)
    q_spec = pl.BlockSpec((1, gw, tq), lambda b, g, i: (b, g, i))
    kv_spec = pl.BlockSpec((1, gw, seq_pad), lambda b, g, i: (b, g, 0))
    inputs = list(extra_inputs) + [qt, kt, vt]
    in_specs = [const(a) for a in extra_inputs] + [q_spec, kv_spec, kv_spec]
    n_past = 0
    if past is not None:
        pkt, pv = past
        past_len = pkt.shape[2]
        assert past_len % tq == 0 and past_len % _CHUNK == 0
        n_past = past_len // tq
        inputs += [pkt, pv]
        in_specs.append(pl.BlockSpec((1, gw, past_len), lambda b, g, i: (b, g, 0)))
        if kind == "a":
            in_specs.append(pl.BlockSpec((past_len, gw), lambda b, g, i: (b, g)))
        else:
            in_specs.append(pl.BlockSpec((1, gw, past_len), lambda b, g, i: (b, g, 0)))
    n_blocks = n_past + nq + (1 if kind == "b" else 0)
    scratch = [
        pltpu.VMEM((groups, n_blocks, tq, _LANES), _BF16),
        pltpu.VMEM((groups, n_blocks, _LANES, tq), _BF16),
        pltpu.VMEM((groups, _LANES, 2 * tq), _BF16),
        pltpu.VMEM((2, groups, tq, 2 * tq), _F32),
    ]
    row_state = pltpu.VMEM((groups, 1, 2 * tq), _F32)
    acc_state = pltpu.VMEM((groups, _LANES, 2 * tq), _F32)
    if kind == "a":
        body = functools.partial(_attn_a_kernel, tq=tq, n_past=n_past, groups=groups, seq=seq, **kernel_kwargs)
        scratch += [row_state, row_state, acc_state]
    else:
        body = functools.partial(_attn_b_kernel, tq=tq, n_past=n_past, groups=groups, seq=seq)
        scratch += [pltpu.VMEM((tq, 2 * tq), _BF16), row_state, acc_state]
    return pl.pallas_call(
        body,
        grid=(batch, d // gw, nq),
        in_specs=in_specs,
        out_specs=pl.BlockSpec((tq, gw), lambda b, g, i: (b * nq + i, g)),
        out_shape=jax.ShapeDtypeStruct((batch * seq_pad, d), _BF16),
        scratch_shapes=scratch,
        compiler_params=_compiler_params(("arbitrary", "arbitrary", "arbitrary")),
        name="attn_" + kind,
    )(*inputs)


def _router_gate(logits, n_experts, n_groups):
    lane_i = lax.broadcasted_iota(jnp.int32, logits.shape, 1)
    lane = lane_i.astype(_F32)
    group_of_lane = (lane_i >> (_EXP_PER_GROUP.bit_length() - 1)).astype(_F32)
    big = float(4 * _LANES)
    row_max = lambda v: jnp.max(v, axis=1, keepdims=True)
    row_min = lambda v: jnp.min(v, axis=1, keepdims=True)
    row_sum = lambda v: jnp.sum(v, axis=1, keepdims=True)

    is_group = (lane_i >= n_experts) & (lane_i < n_experts + n_groups)
    lg = jnp.where(is_group, logits, _NEG_INF)
    eg = jnp.where(is_group, jnp.exp(lg - row_max(lg)), 0.0)
    pg = eg / row_sum(eg)
    pg_sel = row_max(pg)
    g_sel = row_min(jnp.where(is_group & (pg == pg_sel), lane - n_experts, big))

    in_group = (lane_i < n_experts) & (group_of_lane == g_sel)
    le = jnp.where(in_group, logits, _NEG_INF)
    ee = jnp.where(in_group, jnp.exp(le - row_max(le)), 0.0)
    pe = ee / row_sum(ee)
    p1 = row_max(pe)
    i1 = row_min(jnp.where(in_group & (pe == p1), lane, big))
    rest = in_group & (lane != i1)
    p2 = row_max(jnp.where(rest, pe, -1.0))
    i2 = row_min(jnp.where(rest & (pe == p2), lane, big))
    total = p1 + p2
    gate = jnp.where(lane == i1, p1 / total * pg_sel, 0.0) + jnp.where(lane == i2, p2 / total * pg_sel, 0.0)
    return jnp.where(lane_i == n_experts, g_sel, gate)


def _outproj_kernel(mod_ref, x_ref, oa_ref, ob_ref, n1_ref, n2_ref, wg_ref, woa_ref, wob_ref, wout_ref,
                    wr_hi_ref, wr_lo_ref, br_ref, x1_ref, h2_ref, gate_ref, *, seq, n_batch, n_experts, n_groups):
    d = x_ref.shape[1]
    rows = lambda k: _mod_rows(mod_ref, k, d, seq, n_batch)
    x = x_ref[...]
    h = _adaln(x, n1_ref[...], rows(1), rows(0)).astype(_BF16)
    gate_a = _sigmoid(_dot(h, wg_ref[:, :d]))
    gate_b = _sigmoid(_dot(h, wg_ref[:, d:]))
    mix = gate_a * _dot(oa_ref[...], woa_ref[...]) + gate_b * _dot(ob_ref[...], wob_ref[...])
    x1 = x + rows(2) * _dot(mix.astype(_BF16), wout_ref[...])
    x1_ref[...] = x1
    h2 = _adaln(x1, n2_ref[...], rows(4), rows(3))
    h2_ref[...] = h2.astype(_BF16)
    h2_hi, h2_lo = _split(h2)
    logits = _dot(h2_hi, wr_hi_ref[...]) + (_dot(h2_hi, wr_lo_ref[...]) + _dot(h2_lo, wr_hi_ref[...])) + br_ref[...]
    gate_ref[...] = _router_gate(logits, n_experts, n_groups)


def _outproj_call(x2, oa, ob, mod3, mod_off, seq, norm1_g, norm2_g, w_gate_bf, w_oa_bf, w_ob_bf, w_out_bf,
                  w_router, b_router, n_experts, n_groups):
    n, d = x2.shape
    tm, n_batch, tiles_per_batch = _token_tiling(n, seq, _OUTPROJ_TILE)
    wr_hi, wr_lo = _split(w_router)
    row_spec = lambda: pl.BlockSpec((tm, d), lambda i: (i, 0))
    full = lambda a: pl.BlockSpec(a.shape, lambda i: (0,) * a.ndim, pipeline_mode=pl.Buffered(1))
    return pl.pallas_call(
        functools.partial(_outproj_kernel, seq=seq, n_batch=n_batch, n_experts=n_experts, n_groups=n_groups),
        grid=(n // tm,),
        in_specs=[
            _mod_spec(mod3.shape[2], n_batch, tiles_per_batch, mod_off),
            row_spec(), row_spec(), row_spec(),
            pl.BlockSpec((1, d), lambda i: (0, 0)), pl.BlockSpec((1, d), lambda i: (0, 0)),
            full(w_gate_bf), full(w_oa_bf), full(w_ob_bf), full(w_out_bf), full(wr_hi), full(wr_lo),
            pl.BlockSpec((1, _LANES), lambda i: (0, 0)),
        ],
        out_specs=[row_spec(), row_spec(), pl.BlockSpec((tm, _LANES), lambda i: (i, 0))],
        out_shape=[
            jax.ShapeDtypeStruct((n, d), _F32),
            jax.ShapeDtypeStruct((n, d), _BF16),
            jax.ShapeDtypeStruct((n, _LANES), _F32),
        ],
        compiler_params=_compiler_params(("arbitrary",)),
        name="outproj",
    )(mod3, x2, oa, ob, norm1_g.reshape(1, d), norm2_g.reshape(1, d), w_gate_bf, w_oa_bf, w_ob_bf, w_out_bf,
      wr_hi, wr_lo, b_router)


def _split3(x):
    hi = x.astype(_BF16)
    r = x - hi.astype(_F32)
    mid = r.astype(_BF16)
    return hi, mid, (r - mid.astype(_F32)).astype(_BF16)


def _moe_chunk_rows(tm, n_groups):
    return -(-int(tm / n_groups * _MOE_CHUNK_SLACK) // 32) * 32


def _moe_kernel(mod_ref, x1_ref, h2_ref, gate_ref, w1_ref, w3_ref, w2_ref, y_ref,
                gpack_s, rank_col_s, rank_row_s, count_s, *, seq, n_batch, n_experts, n_groups):
    g = pl.program_id(1)
    tm, d = x1_ref.shape
    chunk = _moe_chunk_rows(tm, n_groups)
    lane = lax.broadcasted_iota(jnp.int32, (tm, _LANES), 1)
    assert 3 * n_experts <= _LANES and n_experts & (n_experts - 1) == 0

    @pl.when(g == 0)
    def _():
        y_ref[...] = jnp.zeros(y_ref.shape, _F32)
        gate = gate_ref[...]
        hi, mid, lo = [p.astype(_F32) for p in _split3(jnp.where(lane < n_experts, gate, 0.0))]
        gpack_s[...] = (hi + pltpu.roll(mid, n_experts, 1) + pltpu.roll(lo, 2 * n_experts, 1)).astype(_BF16)
        group_id = gate[:, n_experts:n_experts + 1]
        member = jnp.where((lane < n_groups) & (lane.astype(_F32) == group_id), 1.0, 0.0)
        t_idx = lax.broadcasted_iota(jnp.int32, (tm, tm), 0)
        u_idx = lax.broadcasted_iota(jnp.int32, (tm, tm), 1)
        earlier = jnp.where(u_idx < t_idx, 1.0, 0.0).astype(_BF16)
        rank_col = jnp.where(member > 0.0, _dot(earlier, member.astype(_BF16)), -1.0)
        rank_col_s[...] = rank_col
        rank_row = rank_col.T
        for k in range(n_groups):
            rank_row_s[k] = jnp.broadcast_to(rank_row[k:k + 1, :], (_SUBLANES, tm))
        count_s[...] = jnp.sum(member, axis=0, keepdims=True)

    lane_row = lax.broadcasted_iota(jnp.int32, (1, _LANES), 1)
    count = jnp.sum(jnp.where(lane_row == g, count_s[...], 0.0)).astype(jnp.int32)
    n_chunks = (count + (chunk - 1)) // chunk
    rank_c = jnp.sum(jnp.where(lane == g, rank_col_s[...], 0.0), axis=1, keepdims=True)
    rank_r = jnp.concatenate([rank_row_s[g]] * (chunk // _SUBLANES), axis=0)
    slot_r = lax.broadcasted_iota(jnp.int32, (chunk, tm), 0).astype(_F32)
    slot_c = lax.broadcasted_iota(jnp.int32, (tm, chunk), 1).astype(_F32)
    lane_c = lax.broadcasted_iota(jnp.int32, (chunk, _LANES), 1)
    experts_per_group = w1_ref.shape[0]

    def chunk_body(k, carry):
        base = (k * chunk).astype(_F32)
        pick = jnp.where(rank_r == slot_r + base, 1.0, 0.0).astype(_BF16)
        put = jnp.where(rank_c == slot_c + base, 1.0, 0.0).astype(_BF16)
        xc = _dot(pick, h2_ref[...]).astype(_BF16)
        gc = _dot(pick, gpack_s[...])
        acc = jnp.zeros((chunk, d), _F32)
        for e in range(experts_per_group):
            mine = (lane_c & (n_experts - 1)) == g * experts_per_group + e
            ge = jnp.sum(jnp.where(mine, gc, 0.0), axis=1, keepdims=True)
            hid = _silu(_dot(xc, w1_ref[e])) * _dot(xc, w3_ref[e])
            acc = acc + _dot((hid * ge).astype(_BF16), w2_ref[e])
        y_ref[...] += _dot(put, acc.astype(_BF16))
        return carry

    lax.fori_loop(0, n_chunks, chunk_body, 0)

    @pl.when(g == pl.num_programs(1) - 1)
    def _():
        y_ref[...] = x1_ref[...] + _mod_rows(mod_ref, 5, d, seq, n_batch) * y_ref[...]


def _moe_call(x1, h2, gate, mod3, mod_off, seq, w1_bf, w3_bf, w2_bf, n_groups):
    n, d = x1.shape
    tm, n_batch, tiles_per_batch = _token_tiling(n, seq, _MOE_TILE)
    n_experts, _, d_expert = w1_bf.shape
    eg = n_experts // n_groups
    assert eg == _EXP_PER_GROUP
    row_spec = lambda w, **kw: pl.BlockSpec((tm, w), lambda i, g: (i, 0), **kw)
    return pl.pallas_call(
        functools.partial(_moe_kernel, seq=seq, n_batch=n_batch, n_experts=n_experts, n_groups=n_groups),
        grid=(n // tm, n_groups),
        in_specs=[
            _mod_spec(mod3.shape[2], n_batch, tiles_per_batch, mod_off),
            row_spec(d, pipeline_mode=pl.Buffered(1)), row_spec(d), row_spec(_LANES),
            pl.BlockSpec((eg, d, d_expert), lambda i, g: (g, 0, 0)),
            pl.BlockSpec((eg, d, d_expert), lambda i, g: (g, 0, 0)),
            pl.BlockSpec((eg, d_expert, d), lambda i, g: (g, 0, 0)),
        ],
        out_specs=row_spec(d),
        out_shape=jax.ShapeDtypeStruct((n, d), _F32),
        scratch_shapes=[
            pltpu.VMEM((tm, _LANES), _BF16),
            pltpu.VMEM((tm, _LANES), _F32),
            pltpu.VMEM((n_groups, _SUBLANES, tm), _F32),
            pltpu.VMEM((1, _LANES), _F32),
        ],
        compiler_params=_compiler_params(("arbitrary", "arbitrary"), _MOE_VMEM_LIMIT_BYTES),
        name="moe",
    )(mod3, x1, h2, gate, w1_bf, w3_bf, w2_bf)


def _feature_major(c):
    b, t, h, dh = c.shape
    return jnp.transpose(c, (0, 2, 3, 1)).reshape(b, h * dh, t)


def _token_major(ct, head_dim):
    b, d, t = ct.shape
    return jnp.transpose(ct.reshape(b, d // head_dim, head_dim, t), (0, 3, 1, 2))


def _layer(x, mod3, mod_off, lam, lam_init, past, p):
    batch, seq, d = x.shape
    x2 = x.reshape(batch * seq, d)
    past_len = 0 if past is None else past[0].shape[1]
    pos = past_len + jnp.arange(seq)
    qa_t, ka_t, va, va_t, qs_t, ks_t, vs_t = _proj_call(
        x2, mod3, mod_off, batch, seq, pos, p["norm1_g"], p["w_qkv_t"], p["gq"], p["gk"])
    seq_pad = -(-seq // _LANES) * _LANES
    pad = lambda a: a if seq_pad == seq else jnp.pad(a, ((0, 0), (0, 0), (0, seq_pad - seq)))
    past_a = past_b = None
    if past is not None:
        past_a = (_feature_major(past[0]), past[1].reshape(batch * past_len, d))
        past_b = (_feature_major(past[2]), _feature_major(past[3]))
    oa = _attn_call("a", pad(qa_t), pad(ka_t), pad(va_t), past_a, seq, [lam, p["subln_g"]], lam_init=lam_init)
    ob = _attn_call("b", pad(qs_t), pad(ks_t), pad(vs_t), past_b, seq, [])
    if seq_pad != seq:
        unpad = lambda o: o.reshape(batch, seq_pad, d)[:, :seq].reshape(batch * seq, d)
        oa, ob = unpad(oa), unpad(ob)
    x1, h2, gate = _outproj_call(x2, oa, ob, mod3, mod_off, seq, p["norm1_g"], p["norm2_g"], p["w_gate"],
                                 p["w_oa"], p["w_ob"], p["w_out"], p["w_router"], p["b_router"],
                                 p["n_experts"], p["n_groups"])
    y = _moe_call(x1, h2, gate, mod3, mod_off, seq, p["w1"], p["w3"], p["w2"], p["n_groups"])
    new = (_token_major(ka_t, _HEAD_DIM), va.reshape(batch, seq, -1, 2 * _HEAD_DIM),
           _token_major(ks_t, _HEAD_DIM), _token_major(vs_t, _HEAD_DIM))
    return y.reshape(batch, seq, d), new


def kernel(x_prompt, x_sample, cache_a_k, cache_a_v, cache_b_k, cache_b_v, c_prompt, c_sample, norm1_g, norm2_g, w_ada, b_ada, w_in, a_qnorm_g, a_knorm_g, a_lam_q1, a_lam_k1, a_lam_q2, a_lam_k2, a_subln_g, w_oa, w_ob, w_out, w_rg, b_rg, w_re, b_re, w1, w3, w2):
    depth, d = norm1_g.shape
    n_groups, n_experts = w_rg.shape[2], w_re.shape[2]
    assert w_in.shape[2] == 8 * d and n_experts + n_groups <= _LANES
    batch_p = x_prompt.shape[0]
    xp, xs = x_prompt, x_sample
    c_all = jnp.concatenate([c_prompt, c_sample], axis=0)
    rows_p, rows_s = [], []
    for l in range(depth):
        lam_init = 0.8 - 0.6 * math.exp(-0.3 * l)
        lam_vecs = jnp.stack([a_lam_q1[l], a_lam_k1[l], a_lam_q2[l], a_lam_k2[l]])
        mod, lam = _ada_call(c_all, w_ada[l], b_ada[l], lam_vecs, lam_init)
        mod3 = mod.reshape(mod.shape[0], 1, mod.shape[1])
        pad = _LANES - n_experts - n_groups
        params = dict(
            norm1_g=norm1_g[l], norm2_g=norm2_g[l],
            w_qkv_t=w_in[l][:, :6 * d].T.astype(_BF16), w_gate=w_in[l][:, 6 * d:].astype(_BF16),
            gq=a_qnorm_g[l], gk=a_knorm_g[l],
            subln_g=jnp.broadcast_to(a_subln_g[l].reshape(-1, 1), (a_subln_g.shape[1], _LANES)),
            w_oa=w_oa[l].astype(_BF16), w_ob=w_ob[l].astype(_BF16), w_out=w_out[l].astype(_BF16),
            w_router=jnp.pad(jnp.concatenate([w_re[l], w_rg[l]], axis=1), ((0, 0), (0, pad))),
            b_router=jnp.pad(jnp.concatenate([b_re[l], b_rg[l]]), (0, pad)).reshape(1, _LANES),
            w1=w1[l].astype(_BF16), w3=w3[l].astype(_BF16), w2=w2[l].astype(_BF16),
            n_experts=n_experts, n_groups=n_groups,
        )
        xp, new_p = _layer(xp, mod3, 0, lam, lam_init, None, params)
        past = (cache_a_k[l], cache_a_v[l], cache_b_k[l], cache_b_v[l])
        xs, new_s = _layer(xs, mod3, batch_p, lam, lam_init, past, params)
        rows_p.append(new_p)
        rows_s.append(new_s)
    stack = lambda rows, k: jnp.stack([r[k] for r in rows])
    return (xp, xs, stack(rows_p, 0), stack(rows_p, 1), stack(rows_p, 2), stack(rows_p, 3),
            stack(rows_s, 0), stack(rows_s, 1), stack(rows_s, 2), stack(rows_s, 3))
```

```python
import functools
import math

import jax
import jax.numpy as jnp
from jax import lax
from jax.experimental import pallas as pl
from jax.experimental.pallas import tpu as pltpu

_F32 = jnp.float32
_BF16 = jnp.bfloat16

_LANES = 128
_SUBLANES = 8
_MXU_DIM = 256
_VMEM_LIMIT_BYTES = 48 * 1024 * 1024

_CHUNK = 64
_HEAD_DIM = 64
_ROPE_DIMS = _HEAD_DIM // 4
_ROPE_THETA = 500000.0
_EXP_PER_GROUP = 8
_EPS = 1e-6
_NEG_INF = -1e30
_Q_SCALE = 1.0 / math.sqrt(_HEAD_DIM)
_EXP_ZERO_BELOW = -104.0
_LOG2_E = math.log2(math.e)

_TOKEN_TILE = 512
_OUTPROJ_TILE = 512
_ATTN_TILE = _MXU_DIM
_ATTN_GROUPS = 2
_ONES_ROWS = 16
_MOE_TILE = 1024
_MOE_CHUNK_SLACK = 1.25
_MOE_VMEM_LIMIT_BYTES = 56 * 1024 * 1024


def _dot(a, b):
    return jnp.dot(a, b, preferred_element_type=_F32)


def _split(x):
    hi = x.astype(_BF16)
    lo = (x - hi.astype(_F32)).astype(_BF16)
    return hi, lo


def _dot3(a, b):
    a_hi, a_lo = _split(a)
    b_hi, b_lo = _split(b)
    return _dot(a_hi, b_hi) + (_dot(a_hi, b_lo) + _dot(a_lo, b_hi))


def _silu(x):
    return x / (1.0 + jnp.exp(-x))


def _sigmoid(x):
    return 1.0 / (1.0 + jnp.exp(-x))


def _rms(x, axis=-1):
    return x * lax.rsqrt(jnp.mean(x * x, axis=axis, keepdims=True) + _EPS)


def _mod_rows(mod_ref, k, d, seq, n_batch):
    if n_batch == 1:
        return mod_ref[0, :, k * d:(k + 1) * d]
    rows = [jnp.broadcast_to(mod_ref[b, :, k * d:(k + 1) * d], (seq, d)) for b in range(n_batch)]
    return jnp.concatenate(rows, axis=0)


def _lane_tile(x, n):
    assert n % _LANES == 0
    return jnp.concatenate([x] * (n // _LANES), axis=1) if n > _LANES else x


def _compiler_params(semantics, vmem_limit_bytes=_VMEM_LIMIT_BYTES):
    return pltpu.CompilerParams(dimension_semantics=semantics, vmem_limit_bytes=vmem_limit_bytes)


def _ada_kernel(c_ref, w_ref, b_ref, lam_ref, mod_ref, lam_out_ref, *, lam_init):
    mod_ref[...] = _dot3(_silu(c_ref[...]), w_ref[...]) + b_ref[...]
    lv = lam_ref[...]
    s1 = jnp.sum(lv[0:1] * lv[1:2], axis=-1, keepdims=True)
    s2 = jnp.sum(lv[2:3] * lv[3:4], axis=-1, keepdims=True)
    lam = jnp.exp(s1) - jnp.exp(s2) + lam_init
    lam_out_ref[...] = jnp.broadcast_to(lam, lam_out_ref.shape)


def _ada_call(c_all, w_ada, b_ada, lam_vecs, lam_init):
    rows, d = c_all.shape
    cols = w_ada.shape[1]
    tn = d
    return pl.pallas_call(
        functools.partial(_ada_kernel, lam_init=lam_init),
        grid=(cols // tn,),
        in_specs=[
            pl.BlockSpec((rows, d), lambda j: (0, 0)),
            pl.BlockSpec((d, tn), lambda j: (0, j)),
            pl.BlockSpec((1, tn), lambda j: (0, j)),
            pl.BlockSpec(lam_vecs.shape, lambda j: (0, 0)),
        ],
        out_specs=[
            pl.BlockSpec((rows, tn), lambda j: (0, j)),
            pl.BlockSpec((_SUBLANES, _LANES), lambda j: (0, 0)),
        ],
        out_shape=[
            jax.ShapeDtypeStruct((rows, cols), _F32),
            jax.ShapeDtypeStruct((_SUBLANES, _LANES), _F32),
        ],
        compiler_params=_compiler_params(("arbitrary",)),
        name="ada",
    )(c_all, w_ada, b_ada.reshape(1, cols), lam_vecs)


def _adaln(x, gain, scale, shift):
    return (_rms(x) * gain) * (1.0 + scale) + shift


def _proj_kernel(mod_ref, x_ref, n1_ref, wt_ref, gq_ref, gk_ref, cos_ref, sin_ref,
                 qa_ref, ka_ref, va_ref, vat_ref, qs_ref, ks_ref, vs_ref, ht_s, *, seq, n_batch):
    tm, d = x_ref.shape
    shift = _mod_rows(mod_ref, 0, d, seq, n_batch)
    scale = _mod_rows(mod_ref, 1, d, seq, n_batch)
    ht_s[...] = _adaln(x_ref[...], n1_ref[...], scale, shift).T.astype(_BF16)

    def proj_t(section):
        return _dot(wt_ref[section * d:(section + 1) * d, :], ht_s[...])

    def store_t(out_ref, val):
        if n_batch == 1:
            out_ref[0] = val.astype(out_ref.dtype)
        else:
            for b in range(n_batch):
                out_ref[b] = val[:, b * seq:(b + 1) * seq].astype(out_ref.dtype)

    def head_norm_rope(acc, gain_ref, out_ref, out_scale):
        gain = _lane_tile(gain_ref[...], tm)
        cos, sin = cos_ref[...], sin_ref[...]
        half = _ROPE_DIMS // 2
        parts = []
        for h in range(d // _HEAD_DIM):
            y = _rms(acc[h * _HEAD_DIM:(h + 1) * _HEAD_DIM], axis=0) * gain
            x1, x2 = y[:half], y[half:2 * half]
            parts += [x1 * cos - x2 * sin, x2 * cos + x1 * sin, y[2 * half:]]
        out = jnp.concatenate(parts, axis=0)
        store_t(out_ref, out if out_scale == 1.0 else out * out_scale)

    head_norm_rope(proj_t(0), gq_ref, qa_ref, _Q_SCALE * _LOG2_E)
    head_norm_rope(proj_t(1), gk_ref, ka_ref, 1.0)
    va_t = proj_t(2)
    store_t(vat_ref, va_t)
    va_ref[...] = va_t.T
    store_t(qs_ref, proj_t(3) * _Q_SCALE)
    store_t(ks_ref, proj_t(4))
    store_t(vs_ref, proj_t(5))


def _rope_tables_t(pos):
    half = _ROPE_DIMS // 2
    inv_freq = jnp.exp(-math.log(_ROPE_THETA) * 2.0 * jnp.arange(half, dtype=_F32) / _ROPE_DIMS)
    ang = inv_freq[:, None] * pos.astype(_F32)[None, :]
    return jnp.cos(ang), jnp.sin(ang)


def _token_tiling(n_tokens, seq, tile=_TOKEN_TILE):
    tm = min(tile, n_tokens)
    if tm >= seq:
        assert tm % seq == 0 and n_tokens % tm == 0
        return tm, tm // seq, 1
    assert seq % tm == 0
    return tm, 1, seq // tm


def _mod_spec(width, n_batch, tiles_per_batch, mod_off):
    assert mod_off % n_batch == 0
    first = mod_off // n_batch
    return pl.BlockSpec((n_batch, 1, width), lambda i, *_: (first + i // tiles_per_batch, 0, 0))


def _proj_call(x2, mod3, mod_off, batch, seq, pos, norm1_g, w_qkv_t, gq, gk):
    n, d = x2.shape
    tm, n_batch, tiles_per_batch = _token_tiling(n, seq)
    cos, sin = _rope_tables_t(pos)
    if n_batch > 1:
        cos, sin = jnp.tile(cos, (1, n_batch)), jnp.tile(sin, (1, n_batch))
    lanes_t = tm if n_batch == 1 else seq
    gain_t = lambda g: jnp.broadcast_to(g.reshape(_HEAD_DIM, 1), (_HEAD_DIM, _LANES))
    row_spec = pl.BlockSpec((tm, d), lambda i: (i, 0))
    t_spec = lambda: pl.BlockSpec((n_batch, d, lanes_t), lambda i: (i // tiles_per_batch, 0, i % tiles_per_batch))
    rope_spec = lambda: pl.BlockSpec((_ROPE_DIMS // 2, tm), lambda i: (0, i % tiles_per_batch))
    const = lambda shape: pl.BlockSpec(shape, lambda i: (0, 0))
    t_out = lambda dt: jax.ShapeDtypeStruct((batch, d, seq), dt)
    return pl.pallas_call(
        functools.partial(_proj_kernel, seq=seq, n_batch=n_batch),
        grid=(n // tm,),
        in_specs=[
            _mod_spec(mod3.shape[2], n_batch, tiles_per_batch, mod_off),
            row_spec,
            const((1, d)),
            pl.BlockSpec(w_qkv_t.shape, lambda i: (0, 0), pipeline_mode=pl.Buffered(1)),
            const((_HEAD_DIM, _LANES)), const((_HEAD_DIM, _LANES)),
            rope_spec(), rope_spec(),
        ],
        out_specs=[t_spec(), t_spec(), row_spec, t_spec(), t_spec(), t_spec(), t_spec()],
        out_shape=[t_out(_BF16), t_out(_F32), jax.ShapeDtypeStruct((n, d), _F32), t_out(_BF16),
                   t_out(_BF16), t_out(_F32), t_out(_F32)],
        scratch_shapes=[pltpu.VMEM((d, tm), _BF16)],
        compiler_params=_compiler_params(("arbitrary",)),
        name="proj",
    )(mod3, x2, norm1_g.reshape(1, d), w_qkv_t, gain_t(gq), gain_t(gk), cos, sin)


def _stack_heads_t(qt):
    row = lax.broadcasted_iota(jnp.int32, qt.shape, 0)
    zero = jnp.zeros_like(qt)
    return jnp.concatenate([jnp.where(row < _HEAD_DIM, qt, zero), jnp.where(row >= _HEAD_DIM, qt, zero)], axis=1)


def _diag_offsets(tq):
    assert tq & (tq - 1) == 0
    kpos = lax.broadcasted_iota(jnp.int32, (tq, 2 * tq), 0)
    qpos = lax.broadcasted_iota(jnp.int32, (tq, 2 * tq), 1) & (tq - 1)
    return kpos, qpos


def _load_kv_blocks(groups, tq, n_past, first, kt_ref, vt_ref, past_refs, past_values_token_major, k_s, v_s):
    g = _LANES
    for c in range(groups):
        rows = slice(c * g, (c + 1) * g)
        for jb in range(n_past):
            pk_ref, pv_ref = past_refs
            cols = slice(jb * tq, (jb + 1) * tq)
            k_s[c, first + jb] = pk_ref[0, rows, cols].T.astype(_BF16)
            if past_values_token_major:
                v_s[c, first + jb, :g, :] = pv_ref[cols, rows].T.astype(_BF16)
            else:
                v_s[c, first + jb, :g, :] = pv_ref[0, rows, cols].astype(_BF16)
        for jb in range(k_s.shape[1] - n_past - first):
            cols = slice(jb * tq, (jb + 1) * tq)
            k_s[c, first + n_past + jb] = kt_ref[0, rows, cols].T.astype(_BF16)
            v_s[c, first + n_past + jb, :g, :] = vt_ref[0, rows, cols].astype(_BF16)


def _attn_a_kernel(*refs, tq, n_past, groups, seq, lam_init):
    if n_past:
        lam_ref, subg_ref, qt_ref, kt_ref, vt_ref, pk_ref, pv_ref, o_ref, k_s, v_s, qq_s, s_s, m_s, acc_s = refs
        past_refs = (pk_ref, pv_ref)
    else:
        lam_ref, subg_ref, qt_ref, kt_ref, vt_ref, o_ref, k_s, v_s, qq_s, s_s, m_s, acc_s = refs
        past_refs = None
    qi = pl.program_id(2)

    @pl.when(qi == 0)
    def _():
        _load_kv_blocks(groups, tq, n_past, 0, kt_ref, vt_ref, past_refs, True, k_s, v_s)
        row = lax.broadcasted_iota(jnp.int32, (_ONES_ROWS, tq), 0)
        ones_rows = jnp.where(row == 0, 1.0, 0.0).astype(_BF16)
        for c in range(groups):
            for jb in range(v_s.shape[1]):
                v_s[c, jb, _LANES:, :] = ones_rows

    for c in range(groups):
        qq_s[c] = _stack_heads_t(qt_ref[0, c * _LANES:(c + 1) * _LANES, :])
    m_s[...] = jnp.full(m_s.shape, _NEG_INF, _F32)
    acc_s[...] = jnp.zeros(acc_s.shape, _F32)

    def scores(i):
        return [_dot(k_s[c, i], qq_s[c]) for c in range(groups)]

    def store(slot, blocks):
        for c, st in enumerate(blocks):
            s_s[slot, c] = st

    def update(slot, i, mask):
        for c in range(groups):
            st = s_s[slot, c]
            if mask is not None:
                st = jnp.where(mask, st, _NEG_INF)
            m_prev = m_s[c]
            m_next = jnp.maximum(m_prev, jnp.max(st, axis=0, keepdims=True))
            alpha = jnp.exp2(m_prev - m_next)
            pt = jnp.exp2(st - m_next)
            m_s[c] = m_next
            acc_s[c] = alpha * acc_s[c] + _dot(v_s[c, i], pt.astype(_BF16))

    def step(slot, i, nxt_i, mask=None):
        nxt = scores(nxt_i)
        update(slot, i, mask)
        store(1 - slot, nxt)

    last = n_past + qi
    kpos, qpos = _diag_offsets(tq)
    chunk_bits = _CHUNK.bit_length() - 1
    mask = (kpos >> chunk_bits) <= (qpos >> chunk_bits)
    if seq < tq:
        mask = mask & (kpos < seq)
    store(0, scores(last))
    step(0, last, 0, mask)

    def pair(j, carry):
        step(1, 2 * j, 2 * j + 1)
        step(0, 2 * j + 1, jnp.minimum(2 * j + 2, last))
        return carry

    lax.fori_loop(0, last // 2, pair, 0)

    @pl.when((last & 1) == 1)
    def _():
        update(1, last - 1, None)

    gain = _lane_tile(subg_ref[...], tq)
    for c in range(groups):
        acc = acc_s[c]
        out = acc[:_LANES] / acc[_LANES:_LANES + 1]
        o = out[:, :tq] - lam_ref[0:1, 0:1] * out[:, tq:]
        y = (_rms(o, axis=0) * gain) * (1.0 - lam_init)
        o_ref[:, c * _LANES:(c + 1) * _LANES] = y.T.astype(o_ref.dtype)


def _attn_b_kernel(*refs, tq, n_past, groups, seq):
    if n_past:
        qt_ref, kt_ref, vt_ref, pk_ref, pv_ref, o_ref, k_s, v_s, qq_s, s_s, u_s, c_s, acc_s = refs
        past_refs = (pk_ref, pv_ref)
    else:
        qt_ref, kt_ref, vt_ref, o_ref, k_s, v_s, qq_s, s_s, u_s, c_s, acc_s = refs
        past_refs = None
    qi = pl.program_id(2)

    @pl.when(qi == 0)
    def _():
        for c in range(groups):
            k_s[c, 0] = jnp.zeros(k_s.shape[2:], _BF16)
            v_s[c, 0] = jnp.zeros(v_s.shape[2:], _BF16)
        _load_kv_blocks(groups, tq, n_past, 1, kt_ref, vt_ref, past_refs, False, k_s, v_s)

    for c in range(groups):
        qq_s[c] = _stack_heads_t(qt_ref[0, c * _LANES:(c + 1) * _LANES, :])
    s_idx = lax.broadcasted_iota(jnp.int32, (tq, 2 * tq), 0)
    j_idx = lax.broadcasted_iota(jnp.int32, (tq, 2 * tq), 1) & (tq - 1)
    u_s[...] = jnp.where(j_idx >= s_idx, -1.0, 0.0).astype(_BF16)
    c_s[...] = jnp.zeros(c_s.shape, _F32)
    acc_s[...] = jnp.zeros(acc_s.shape, _F32)

    def logits(i):
        return [_dot(k_s[c, i], qq_s[c]) for c in range(groups)]

    def update(slot, i, mask):
        rests = []
        for c in range(groups):
            zt = s_s[slot, c]
            neg_log_rest = jnp.maximum(zt, 0.0) + jnp.log(1.0 + jnp.exp2(jnp.abs(zt) * (-_LOG2_E)))
            if mask is not None:
                neg_log_rest = jnp.where(mask, neg_log_rest, 0.0)
            hi, lo = _split(neg_log_rest)
            rests.append(_dot(u_s[...], jnp.concatenate([hi, lo], axis=0)))
        for c, rest_from_here in enumerate(rests):
            wt = jnp.exp(s_s[slot, c] + rest_from_here + c_s[c])
            if mask is not None:
                wt = jnp.where(mask, wt, 0.0)
            acc_s[c] = acc_s[c] + _dot(v_s[c, i], wt.astype(_BF16))
            c_s[c] = c_s[c] + rest_from_here[0:1, :]

    def carry_max():
        return functools.reduce(jnp.maximum, [jnp.max(c_s[c]) for c in range(groups)])

    top = n_past + qi + 1

    def store(slot, blocks):
        for c, zt in enumerate(blocks):
            s_s[slot, c] = zt

    def step(slot, i, nxt_i, mask=None):
        nxt = logits(nxt_i)
        update(slot, i, mask)
        store(1 - slot, nxt)

    kpos, qpos = _diag_offsets(tq)
    store(0, logits(top))
    step(0, top, top - 1, kpos < qpos)
    step(1, top - 1, jnp.maximum(top - 2, 0))

    def pair(state):
        i = state[0]
        step(0, i, i - 1)
        step(1, i - 1, jnp.maximum(i - 2, 0))
        return i - 2, carry_max()

    lax.while_loop(lambda s: (s[0] >= 1) & (s[1] > _EXP_ZERO_BELOW), pair, (top - 2, carry_max()))

    row = lax.broadcasted_iota(jnp.int32, (_LANES, tq), 0)
    for c in range(groups):
        acc = acc_s[c]
        out = jnp.where(row < _HEAD_DIM, acc[:, :tq], acc[:, tq:])
        o_ref[:, c * _LANES:(c + 1) * _LANES] = out.T.astype(o_ref.dtype)


def _attn_call(kind, qt, kt, vt, past, seq, extra_inputs, **kernel_kwargs):
    batch, d, seq_pad = qt.shape
    tq = min(_ATTN_TILE, seq_pad)
    assert seq_pad % tq == 0 and tq % _LANES == 0
    nq = seq_pad // tq
    groups = _ATTN_GROUPS * (2 if nq == 1 else 1)
    gw = groups * _LANES
    assert d % gw == 0
    const = lambda a: pl.BlockSpec(a.shape, lambda b, g, i: (0,) * a.ndim)
    q_spec = pl.BlockSpec((1, gw, tq), lambda b, g, i: (b, g, i))
    kv_spec = pl.BlockSpec((1, gw, seq_pad), lambda b, g, i: (b, g, 0))
    inputs = list(extra_inputs) + [qt, kt, vt]
    in_specs = [const(a) for a in extra_inputs] + [q_spec, kv_spec, kv_spec]
    n_past = 0
    if past is not None:
        pkt, pv = past
        past_len = pkt.shape[2]
        assert past_len % tq == 0 and past_len % _CHUNK == 0
        n_past = past_len // tq
        inputs += [pkt, pv]
        in_specs.append(pl.BlockSpec((1, gw, past_len), lambda b, g, i: (b, g, 0)))
        if kind == "a":
            in_specs.append(pl.BlockSpec((past_len, gw), lambda b, g, i: (b, g)))
        else:
            in_specs.append(pl.BlockSpec((1, gw, past_len), lambda b, g, i: (b, g, 0)))
    n_blocks = n_past + nq + (1 if kind == "b" else 0)
    v_rows = _LANES + (_ONES_ROWS if kind == "a" else 0)
    scratch = [
        pltpu.VMEM((groups, n_blocks, tq, _LANES), _BF16),
        pltpu.VMEM((groups, n_blocks, v_rows, tq), _BF16),
        pltpu.VMEM((groups, _LANES, 2 * tq), _BF16),
        pltpu.VMEM((2, groups, tq, 2 * tq), _F32),
    ]
    row_state = pltpu.VMEM((groups, 1, 2 * tq), _F32)
    acc_state = pltpu.VMEM((groups, v_rows, 2 * tq), _F32)
    if kind == "a":
        body = functools.partial(_attn_a_kernel, tq=tq, n_past=n_past, groups=groups, seq=seq, **kernel_kwargs)
        scratch += [row_state, acc_state]
    else:
        body = functools.partial(_attn_b_kernel, tq=tq, n_past=n_past, groups=groups, seq=seq)
        scratch += [pltpu.VMEM((tq, 2 * tq), _BF16), row_state, acc_state]
    return pl.pallas_call(
        body,
        grid=(batch, d // gw, nq),
        in_specs=in_specs,
        out_specs=pl.BlockSpec((tq, gw), lambda b, g, i: (b * nq + i, g)),
        out_shape=jax.ShapeDtypeStruct((batch * seq_pad, d), _BF16),
        scratch_shapes=scratch,
        compiler_params=_compiler_params(("arbitrary", "arbitrary", "arbitrary")),
        name="attn_" + kind,
    )(*inputs)


def _router_gate(logits, n_experts, n_groups):
    lane_i = lax.broadcasted_iota(jnp.int32, logits.shape, 1)
    lane = lane_i.astype(_F32)
    group_of_lane = (lane_i >> (_EXP_PER_GROUP.bit_length() - 1)).astype(_F32)
    big = float(4 * _LANES)
    row_max = lambda v: jnp.max(v, axis=1, keepdims=True)
    row_min = lambda v: jnp.min(v, axis=1, keepdims=True)
    row_sum = lambda v: jnp.sum(v, axis=1, keepdims=True)

    is_group = (lane_i >= n_experts) & (lane_i < n_experts + n_groups)
    lg = jnp.where(is_group, logits, _NEG_INF)
    eg = jnp.where(is_group, jnp.exp(lg - row_max(lg)), 0.0)
    pg = eg / row_sum(eg)
    pg_sel = row_max(pg)
    g_sel = row_min(jnp.where(is_group & (pg == pg_sel), lane - n_experts, big))

    in_group = (lane_i < n_experts) & (group_of_lane == g_sel)
    le = jnp.where(in_group, logits, _NEG_INF)
    ee = jnp.where(in_group, jnp.exp(le - row_max(le)), 0.0)
    pe = ee / row_sum(ee)
    p1 = row_max(pe)
    i1 = row_min(jnp.where(in_group & (pe == p1), lane, big))
    rest = in_group & (lane != i1)
    p2 = row_max(jnp.where(rest, pe, -1.0))
    i2 = row_min(jnp.where(rest & (pe == p2), lane, big))
    total = p1 + p2
    gate = jnp.where(lane == i1, p1 / total * pg_sel, 0.0) + jnp.where(lane == i2, p2 / total * pg_sel, 0.0)
    return jnp.where(lane_i == n_experts, g_sel, gate)


def _outproj_kernel(mod_ref, x_ref, oa_ref, ob_ref, n1_ref, n2_ref, wg_ref, woa_ref, wob_ref, wout_ref,
                    wr_hi_ref, wr_lo_ref, br_ref, x1_ref, h2_ref, gate_ref, *, seq, n_batch, n_experts, n_groups):
    d = x_ref.shape[1]
    rows = lambda k: _mod_rows(mod_ref, k, d, seq, n_batch)
    x = x_ref[...]
    h = _adaln(x, n1_ref[...], rows(1), rows(0)).astype(_BF16)
    gate_a = _sigmoid(_dot(h, wg_ref[:, :d]))
    gate_b = _sigmoid(_dot(h, wg_ref[:, d:]))
    mix = gate_a * _dot(oa_ref[...], woa_ref[...]) + gate_b * _dot(ob_ref[...], wob_ref[...])
    x1 = x + rows(2) * _dot(mix.astype(_BF16), wout_ref[...])
    x1_ref[...] = x1
    h2 = _adaln(x1, n2_ref[...], rows(4), rows(3))
    h2_ref[...] = h2.astype(_BF16)
    h2_hi, h2_lo = _split(h2)
    logits = _dot(h2_hi, wr_hi_ref[...]) + (_dot(h2_hi, wr_lo_ref[...]) + _dot(h2_lo, wr_hi_ref[...])) + br_ref[...]
    gate_ref[...] = _router_gate(logits, n_experts, n_groups)


def _outproj_call(x2, oa, ob, mod3, mod_off, seq, norm1_g, norm2_g, w_gate_bf, w_oa_bf, w_ob_bf, w_out_bf,
                  w_router, b_router, n_experts, n_groups):
    n, d = x2.shape
    tm, n_batch, tiles_per_batch = _token_tiling(n, seq, _OUTPROJ_TILE)
    wr_hi, wr_lo = _split(w_router)
    row_spec = lambda: pl.BlockSpec((tm, d), lambda i: (i, 0))
    full = lambda a: pl.BlockSpec(a.shape, lambda i: (0,) * a.ndim, pipeline_mode=pl.Buffered(1))
    return pl.pallas_call(
        functools.partial(_outproj_kernel, seq=seq, n_batch=n_batch, n_experts=n_experts, n_groups=n_groups),
        grid=(n // tm,),
        in_specs=[
            _mod_spec(mod3.shape[2], n_batch, tiles_per_batch, mod_off),
            row_spec(), row_spec(), row_spec(),
            pl.BlockSpec((1, d), lambda i: (0, 0)), pl.BlockSpec((1, d), lambda i: (0, 0)),
            full(w_gate_bf), full(w_oa_bf), full(w_ob_bf), full(w_out_bf), full(wr_hi), full(wr_lo),
            pl.BlockSpec((1, _LANES), lambda i: (0, 0)),
        ],
        out_specs=[row_spec(), row_spec(), pl.BlockSpec((tm, _LANES), lambda i: (i, 0))],
        out_shape=[
            jax.ShapeDtypeStruct((n, d), _F32),
            jax.ShapeDtypeStruct((n, d), _BF16),
            jax.ShapeDtypeStruct((n, _LANES), _F32),
        ],
        compiler_params=_compiler_params(("arbitrary",)),
        name="outproj",
    )(mod3, x2, oa, ob, norm1_g.reshape(1, d), norm2_g.reshape(1, d), w_gate_bf, w_oa_bf, w_ob_bf, w_out_bf,
      wr_hi, wr_lo, b_router)


def _split3(x):
    hi = x.astype(_BF16)
    r = x - hi.astype(_F32)
    mid = r.astype(_BF16)
    return hi, mid, (r - mid.astype(_F32)).astype(_BF16)


def _moe_chunk_rows(tm, n_groups):
    return -(-int(tm / n_groups * _MOE_CHUNK_SLACK) // 32) * 32


def _moe_kernel(mod_ref, x1_ref, h2_ref, gate_ref, w1_ref, w3_ref, w2_ref, y_ref,
                gpack_s, rank_col_s, rank_row_s, count_s, *, seq, n_batch, n_experts, n_groups):
    g = pl.program_id(1)
    tm, d = x1_ref.shape
    chunk = _moe_chunk_rows(tm, n_groups)
    lane = lax.broadcasted_iota(jnp.int32, (tm, _LANES), 1)
    assert 3 * n_experts <= _LANES and n_experts & (n_experts - 1) == 0

    @pl.when(g == 0)
    def _():
        y_ref[...] = jnp.zeros(y_ref.shape, _F32)
        gate = gate_ref[...]
        hi, mid, lo = [p.astype(_F32) for p in _split3(jnp.where(lane < n_experts, gate, 0.0))]
        gpack_s[...] = (hi + pltpu.roll(mid, n_experts, 1) + pltpu.roll(lo, 2 * n_experts, 1)).astype(_BF16)
        group_id = gate[:, n_experts:n_experts + 1]
        member = jnp.where((lane < n_groups) & (lane.astype(_F32) == group_id), 1.0, 0.0)
        t_idx = lax.broadcasted_iota(jnp.int32, (tm, tm), 0)
        u_idx = lax.broadcasted_iota(jnp.int32, (tm, tm), 1)
        earlier = jnp.where(u_idx < t_idx, 1.0, 0.0).astype(_BF16)
        rank_col = jnp.where(member > 0.0, _dot(earlier, member.astype(_BF16)), -1.0)
        rank_col_s[...] = rank_col
        rank_row = rank_col.T
        for k in range(n_groups):
            rank_row_s[k] = jnp.broadcast_to(rank_row[k:k + 1, :], (_SUBLANES, tm))
        count_s[...] = jnp.sum(member, axis=0, keepdims=True)

    lane_row = lax.broadcasted_iota(jnp.int32, (1, _LANES), 1)
    count = jnp.sum(jnp.where(lane_row == g, count_s[...], 0.0)).astype(jnp.int32)
    n_chunks = (count + (chunk - 1)) // chunk
    rank_c = jnp.sum(jnp.where(lane == g, rank_col_s[...], 0.0), axis=1, keepdims=True)
    rank_r = jnp.concatenate([rank_row_s[g]] * (chunk // _SUBLANES), axis=0)
    slot_r = lax.broadcasted_iota(jnp.int32, (chunk, tm), 0).astype(_F32)
    slot_c = lax.broadcasted_iota(jnp.int32, (tm, chunk), 1).astype(_F32)
    lane_c = lax.broadcasted_iota(jnp.int32, (chunk, _LANES), 1)
    experts_per_group = w1_ref.shape[0]

    def chunk_body(k, carry):
        base = (k * chunk).astype(_F32)
        pick = jnp.where(rank_r == slot_r + base, 1.0, 0.0).astype(_BF16)
        put = jnp.where(rank_c == slot_c + base, 1.0, 0.0).astype(_BF16)
        xc = _dot(pick, h2_ref[...]).astype(_BF16)
        gc = _dot(pick, gpack_s[...])
        acc = jnp.zeros((chunk, d), _F32)
        for e in range(experts_per_group):
            mine = (lane_c & (n_experts - 1)) == g * experts_per_group + e
            ge = jnp.sum(jnp.where(mine, gc, 0.0), axis=1, keepdims=True)
            hid = _silu(_dot(xc, w1_ref[e])) * _dot(xc, w3_ref[e])
            acc = acc + _dot((hid * ge).astype(_BF16), w2_ref[e])
        y_ref[...] += _dot(put, acc.astype(_BF16))
        return carry

    lax.fori_loop(0, n_chunks, chunk_body, 0)

    @pl.when(g == pl.num_programs(1) - 1)
    def _():
        y_ref[...] = x1_ref[...] + _mod_rows(mod_ref, 5, d, seq, n_batch) * y_ref[...]


def _moe_call(x1, h2, gate, mod3, mod_off, seq, w1_bf, w3_bf, w2_bf, n_groups):
    n, d = x1.shape
    tm, n_batch, tiles_per_batch = _token_tiling(n, seq, _MOE_TILE)
    n_experts, _, d_expert = w1_bf.shape
    eg = n_experts // n_groups
    assert eg == _EXP_PER_GROUP
    row_spec = lambda w, **kw: pl.BlockSpec((tm, w), lambda i, g: (i, 0), **kw)
    return pl.pallas_call(
        functools.partial(_moe_kernel, seq=seq, n_batch=n_batch, n_experts=n_experts, n_groups=n_groups),
        grid=(n // tm, n_groups),
        in_specs=[
            _mod_spec(mod3.shape[2], n_batch, tiles_per_batch, mod_off),
            row_spec(d, pipeline_mode=pl.Buffered(1)), row_spec(d), row_spec(_LANES),
            pl.BlockSpec((eg, d, d_expert), lambda i, g: (g, 0, 0)),
            pl.BlockSpec((eg, d, d_expert), lambda i, g: (g, 0, 0)),
            pl.BlockSpec((eg, d_expert, d), lambda i, g: (g, 0, 0)),
        ],
        out_specs=row_spec(d),
        out_shape=jax.ShapeDtypeStruct((n, d), _F32),
        scratch_shapes=[
            pltpu.VMEM((tm, _LANES), _BF16),
            pltpu.VMEM((tm, _LANES), _F32),
            pltpu.VMEM((n_groups, _SUBLANES, tm), _F32),
            pltpu.VMEM((1, _LANES), _F32),
        ],
        compiler_params=_compiler_params(("arbitrary", "arbitrary"), _MOE_VMEM_LIMIT_BYTES),
        name="moe",
    )(mod3, x1, h2, gate, w1_bf, w3_bf, w2_bf)


def _feature_major(c):
    b, t, h, dh = c.shape
    return jnp.transpose(c, (0, 2, 3, 1)).reshape(b, h * dh, t)


def _token_major(ct, head_dim):
    b, d, t = ct.shape
    return jnp.transpose(ct.reshape(b, d // head_dim, head_dim, t), (0, 3, 1, 2))


def _layer(x, mod3, mod_off, lam, lam_init, past, p):
    batch, seq, d = x.shape
    x2 = x.reshape(batch * seq, d)
    past_len = 0 if past is None else past[0].shape[1]
    pos = past_len + jnp.arange(seq)
    qa_t, ka_t, va, va_t, qs_t, ks_t, vs_t = _proj_call(
        x2, mod3, mod_off, batch, seq, pos, p["norm1_g"], p["w_qkv_t"], p["gq"], p["gk"])
    seq_pad = -(-seq // _LANES) * _LANES
    pad = lambda a: a if seq_pad == seq else jnp.pad(a, ((0, 0), (0, 0), (0, seq_pad - seq)))
    past_a = past_b = None
    if past is not None:
        past_a = (_feature_major(past[0]), past[1].reshape(batch * past_len, d))
        past_b = (_feature_major(past[2]), _feature_major(past[3]))
    oa = _attn_call("a", pad(qa_t), pad(ka_t), pad(va_t), past_a, seq, [lam, p["subln_g"]], lam_init=lam_init)
    ob = _attn_call("b", pad(qs_t), pad(ks_t), pad(vs_t), past_b, seq, [])
    if seq_pad != seq:
        unpad = lambda o: o.reshape(batch, seq_pad, d)[:, :seq].reshape(batch * seq, d)
        oa, ob = unpad(oa), unpad(ob)
    x1, h2, gate = _outproj_call(x2, oa, ob, mod3, mod_off, seq, p["norm1_g"], p["norm2_g"], p["w_gate"],
                                 p["w_oa"], p["w_ob"], p["w_out"], p["w_router"], p["b_router"],
                                 p["n_experts"], p["n_groups"])
    y = _moe_call(x1, h2, gate, mod3, mod_off, seq, p["w1"], p["w3"], p["w2"], p["n_groups"])
    new = (_token_major(ka_t, _HEAD_DIM), va.reshape(batch, seq, -1, 2 * _HEAD_DIM),
           _token_major(ks_t, _HEAD_DIM), _token_major(vs_t, _HEAD_DIM))
    return y.reshape(batch, seq, d), new


def kernel(x_prompt, x_sample, cache_a_k, cache_a_v, cache_b_k, cache_b_v, c_prompt, c_sample, norm1_g, norm2_g, w_ada, b_ada, w_in, a_qnorm_g, a_knorm_g, a_lam_q1, a_lam_k1, a_lam_q2, a_lam_k2, a_subln_g, w_oa, w_ob, w_out, w_rg, b_rg, w_re, b_re, w1, w3, w2):
    depth, d = norm1_g.shape
    n_groups, n_experts = w_rg.shape[2], w_re.shape[2]
    assert w_in.shape[2] == 8 * d and n_experts + n_groups <= _LANES
    batch_p = x_prompt.shape[0]
    xp, xs = x_prompt, x_sample
    c_all = jnp.concatenate([c_prompt, c_sample], axis=0)
    rows_p, rows_s = [], []
    for l in range(depth):
        lam_init = 0.8 - 0.6 * math.exp(-0.3 * l)
        lam_vecs = jnp.stack([a_lam_q1[l], a_lam_k1[l], a_lam_q2[l], a_lam_k2[l]])
        mod, lam = _ada_call(c_all, w_ada[l], b_ada[l], lam_vecs, lam_init)
        mod3 = mod.reshape(mod.shape[0], 1, mod.shape[1])
        pad = _LANES - n_experts - n_groups
        params = dict(
            norm1_g=norm1_g[l], norm2_g=norm2_g[l],
            w_qkv_t=w_in[l][:, :6 * d].T.astype(_BF16), w_gate=w_in[l][:, 6 * d:].astype(_BF16),
            gq=a_qnorm_g[l], gk=a_knorm_g[l],
            subln_g=jnp.broadcast_to(a_subln_g[l].reshape(-1, 1), (a_subln_g.shape[1], _LANES)),
            w_oa=w_oa[l].astype(_BF16), w_ob=w_ob[l].astype(_BF16), w_out=w_out[l].astype(_BF16),
            w_router=jnp.pad(jnp.concatenate([w_re[l], w_rg[l]], axis=1), ((0, 0), (0, pad))),
            b_router=jnp.pad(jnp.concatenate([b_re[l], b_rg[l]]), (0, pad)).reshape(1, _LANES),
            w1=w1[l].astype(_BF16), w3=w3[l].astype(_BF16), w2=w2[l].astype(_BF16),
            n_experts=n_experts, n_groups=n_groups,
        )
        xp, new_p = _layer(xp, mod3, 0, lam, lam_init, None, params)
        past = (cache_a_k[l], cache_a_v[l], cache_b_k[l], cache_b_v[l])
        xs, new_s = _layer(xs, mod3, batch_p, lam, lam_init, past, params)
        rows_p.append(new_p)
        rows_s.append(new_s)
    stack = lambda rows, k: jnp.stack([r[k] for r in rows])
    return (xp, xs, stack(rows_p, 0), stack(rows_p, 1), stack(rows_p, 2), stack(rows_p, 3),
            stack(rows_s, 0), stack(rows_s, 1), stack(rows_s, 2), stack(rows_s, 3))
```

```python
import functools
import math

import jax
import jax.numpy as jnp
from jax import lax
from jax.experimental import pallas as pl
from jax.experimental.pallas import tpu as pltpu

_F32 = jnp.float32
_BF16 = jnp.bfloat16

_LANES = 128
_SUBLANES = 8
_MXU_DIM = 256
_VMEM_LIMIT_BYTES = 48 * 1024 * 1024

_CHUNK = 64
_HEAD_DIM = 64
_ROPE_DIMS = _HEAD_DIM // 4
_ROPE_THETA = 500000.0
_EXP_PER_GROUP = 8
_EPS = 1e-6
_NEG_INF = -1e30
_Q_SCALE = 1.0 / math.sqrt(_HEAD_DIM)
_EXP_ZERO_BELOW = -104.0
_LOG2_E = math.log2(math.e)

_TOKEN_TILE = 512
_OUTPROJ_TILE = 512
_ATTN_TILE = _MXU_DIM
_ATTN_GROUPS = 4
_ONES_ROWS = 16
_MOE_TILE = 1024
_MOE_CHUNK_SLACK = 0.75
_MOE_VMEM_LIMIT_BYTES = 56 * 1024 * 1024


def _dot(a, b):
    return jnp.dot(a, b, preferred_element_type=_F32)


def _split(x):
    hi = x.astype(_BF16)
    lo = (x - hi.astype(_F32)).astype(_BF16)
    return hi, lo


def _dot3(a, b):
    a_hi, a_lo = _split(a)
    b_hi, b_lo = _split(b)
    return _dot(a_hi, b_hi) + (_dot(a_hi, b_lo) + _dot(a_lo, b_hi))


def _silu(x):
    return x / (1.0 + jnp.exp(-x))


def _sigmoid(x):
    return 1.0 / (1.0 + jnp.exp(-x))


def _rms(x, axis=-1):
    return x * lax.rsqrt(jnp.mean(x * x, axis=axis, keepdims=True) + _EPS)


def _mod_rows(mod_ref, k, d, seq, n_batch):
    if n_batch == 1:
        return mod_ref[0, :, k * d:(k + 1) * d]
    rows = [jnp.broadcast_to(mod_ref[b, :, k * d:(k + 1) * d], (seq, d)) for b in range(n_batch)]
    return jnp.concatenate(rows, axis=0)


def _lane_tile(x, n):
    assert n % _LANES == 0
    return jnp.concatenate([x] * (n // _LANES), axis=1) if n > _LANES else x


def _compiler_params(semantics, vmem_limit_bytes=_VMEM_LIMIT_BYTES):
    return pltpu.CompilerParams(dimension_semantics=semantics, vmem_limit_bytes=vmem_limit_bytes)


def _ada_kernel(c_ref, w_ref, b_ref, lam_ref, mod_ref, lam_out_ref, *, lam_init):
    mod_ref[...] = _dot3(_silu(c_ref[...]), w_ref[...]) + b_ref[...]
    lv = lam_ref[...]
    s1 = jnp.sum(lv[0:1] * lv[1:2], axis=-1, keepdims=True)
    s2 = jnp.sum(lv[2:3] * lv[3:4], axis=-1, keepdims=True)
    lam = jnp.exp(s1) - jnp.exp(s2) + lam_init
    lam_out_ref[...] = jnp.broadcast_to(lam, lam_out_ref.shape)


def _ada_call(c_all, w_ada, b_ada, lam_vecs, lam_init):
    rows, d = c_all.shape
    cols = w_ada.shape[1]
    tn = d
    return pl.pallas_call(
        functools.partial(_ada_kernel, lam_init=lam_init),
        grid=(cols // tn,),
        in_specs=[
            pl.BlockSpec((rows, d), lambda j: (0, 0)),
            pl.BlockSpec((d, tn), lambda j: (0, j)),
            pl.BlockSpec((1, tn), lambda j: (0, j)),
            pl.BlockSpec(lam_vecs.shape, lambda j: (0, 0)),
        ],
        out_specs=[
            pl.BlockSpec((rows, tn), lambda j: (0, j)),
            pl.BlockSpec((_SUBLANES, _LANES), lambda j: (0, 0)),
        ],
        out_shape=[
            jax.ShapeDtypeStruct((rows, cols), _F32),
            jax.ShapeDtypeStruct((_SUBLANES, _LANES), _F32),
        ],
        compiler_params=_compiler_params(("arbitrary",)),
        name="ada",
    )(c_all, w_ada, b_ada.reshape(1, cols), lam_vecs)


def _adaln(x, gain, scale, shift):
    return (_rms(x) * gain) * (1.0 + scale) + shift


def _proj_kernel(mod_ref, x_ref, n1_ref, wt_ref, gq_ref, gk_ref, cos_ref, sin_ref,
                 qa_ref, ka_ref, va_ref, vat_ref, qs_ref, ks_ref, vs_ref, ht_s, *, seq, n_batch):
    tm, d = x_ref.shape
    shift = _mod_rows(mod_ref, 0, d, seq, n_batch)
    scale = _mod_rows(mod_ref, 1, d, seq, n_batch)
    ht_s[...] = _adaln(x_ref[...], n1_ref[...], scale, shift).T.astype(_BF16)

    def proj_t(section):
        return _dot(wt_ref[section * d:(section + 1) * d, :], ht_s[...])

    def store_t(out_ref, val):
        if n_batch == 1:
            out_ref[0] = val.astype(out_ref.dtype)
        else:
            for b in range(n_batch):
                out_ref[b] = val[:, b * seq:(b + 1) * seq].astype(out_ref.dtype)

    def head_norm_rope(acc, gain_ref, out_ref, out_scale):
        gain = _lane_tile(gain_ref[...], tm)
        cos, sin = cos_ref[...], sin_ref[...]
        half = _ROPE_DIMS // 2
        parts = []
        for h in range(d // _HEAD_DIM):
            y = _rms(acc[h * _HEAD_DIM:(h + 1) * _HEAD_DIM], axis=0) * gain
            x1, x2 = y[:half], y[half:2 * half]
            parts += [x1 * cos - x2 * sin, x2 * cos + x1 * sin, y[2 * half:]]
        out = jnp.concatenate(parts, axis=0)
        store_t(out_ref, out if out_scale == 1.0 else out * out_scale)

    head_norm_rope(proj_t(0), gq_ref, qa_ref, _Q_SCALE * _LOG2_E)
    head_norm_rope(proj_t(1), gk_ref, ka_ref, 1.0)
    va_t = proj_t(2)
    store_t(vat_ref, va_t)
    va_ref[...] = va_t.T
    store_t(qs_ref, proj_t(3) * _Q_SCALE)
    store_t(ks_ref, proj_t(4))
    store_t(vs_ref, proj_t(5))


def _rope_tables_t(pos):
    half = _ROPE_DIMS // 2
    inv_freq = jnp.exp(-math.log(_ROPE_THETA) * 2.0 * jnp.arange(half, dtype=_F32) / _ROPE_DIMS)
    ang = inv_freq[:, None] * pos.astype(_F32)[None, :]
    return jnp.cos(ang), jnp.sin(ang)


def _token_tiling(n_tokens, seq, tile=_TOKEN_TILE):
    tm = min(tile, n_tokens)
    if tm >= seq:
        assert tm % seq == 0 and n_tokens % tm == 0
        return tm, tm // seq, 1
    assert seq % tm == 0
    return tm, 1, seq // tm


def _mod_spec(width, n_batch, tiles_per_batch, mod_off):
    assert mod_off % n_batch == 0
    first = mod_off // n_batch
    return pl.BlockSpec((n_batch, 1, width), lambda i, *_: (first + i // tiles_per_batch, 0, 0))


def _proj_call(x2, mod3, mod_off, batch, seq, pos, norm1_g, w_qkv_t, gq, gk):
    n, d = x2.shape
    tm, n_batch, tiles_per_batch = _token_tiling(n, seq)
    cos, sin = _rope_tables_t(pos)
    if n_batch > 1:
        cos, sin = jnp.tile(cos, (1, n_batch)), jnp.tile(sin, (1, n_batch))
    lanes_t = tm if n_batch == 1 else seq
    gain_t = lambda g: jnp.broadcast_to(g.reshape(_HEAD_DIM, 1), (_HEAD_DIM, _LANES))
    row_spec = pl.BlockSpec((tm, d), lambda i: (i, 0))
    t_spec = lambda: pl.BlockSpec((n_batch, d, lanes_t), lambda i: (i // tiles_per_batch, 0, i % tiles_per_batch))
    rope_spec = lambda: pl.BlockSpec((_ROPE_DIMS // 2, tm), lambda i: (0, i % tiles_per_batch))
    const = lambda shape: pl.BlockSpec(shape, lambda i: (0, 0))
    t_out = lambda dt: jax.ShapeDtypeStruct((batch, d, seq), dt)
    return pl.pallas_call(
        functools.partial(_proj_kernel, seq=seq, n_batch=n_batch),
        grid=(n // tm,),
        in_specs=[
            _mod_spec(mod3.shape[2], n_batch, tiles_per_batch, mod_off),
            row_spec,
            const((1, d)),
            pl.BlockSpec(w_qkv_t.shape, lambda i: (0, 0), pipeline_mode=pl.Buffered(1)),
            const((_HEAD_DIM, _LANES)), const((_HEAD_DIM, _LANES)),
            rope_spec(), rope_spec(),
        ],
        out_specs=[t_spec(), t_spec(), row_spec, t_spec(), t_spec(), t_spec(), t_spec()],
        out_shape=[t_out(_BF16), t_out(_F32), jax.ShapeDtypeStruct((n, d), _F32), t_out(_BF16),
                   t_out(_BF16), t_out(_F32), t_out(_F32)],
        scratch_shapes=[pltpu.VMEM((d, tm), _BF16)],
        compiler_params=_compiler_params(("arbitrary",)),
        name="proj",
    )(mod3, x2, norm1_g.reshape(1, d), w_qkv_t, gain_t(gq), gain_t(gk), cos, sin)


def _stack_heads_t(qt):
    row = lax.broadcasted_iota(jnp.int32, qt.shape, 0)
    zero = jnp.zeros_like(qt)
    return jnp.concatenate([jnp.where(row < _HEAD_DIM, qt, zero), jnp.where(row >= _HEAD_DIM, qt, zero)], axis=1)


def _diag_offsets(tq):
    assert tq & (tq - 1) == 0
    kpos = lax.broadcasted_iota(jnp.int32, (tq, 2 * tq), 0)
    qpos = lax.broadcasted_iota(jnp.int32, (tq, 2 * tq), 1) & (tq - 1)
    return kpos, qpos


def _load_kv_blocks(groups, tq, n_past, first, kt_ref, vt_ref, past_refs, past_values_token_major, k_s, v_s):
    g = _LANES
    for c in range(groups):
        rows = slice(c * g, (c + 1) * g)
        for jb in range(n_past):
            pk_ref, pv_ref = past_refs
            cols = slice(jb * tq, (jb + 1) * tq)
            k_s[c, first + jb] = pk_ref[0, rows, cols].T.astype(_BF16)
            if past_values_token_major:
                v_s[c, first + jb, :g, :] = pv_ref[cols, rows].T.astype(_BF16)
            else:
                v_s[c, first + jb, :g, :] = pv_ref[0, rows, cols].astype(_BF16)
        for jb in range(k_s.shape[1] - n_past - first):
            cols = slice(jb * tq, (jb + 1) * tq)
            k_s[c, first + n_past + jb] = kt_ref[0, rows, cols].T.astype(_BF16)
            v_s[c, first + n_past + jb, :g, :] = vt_ref[0, rows, cols].astype(_BF16)


def _attn_a_kernel(*refs, tq, n_past, groups, seq, lam_init):
    if n_past:
        lam_ref, subg_ref, qt_ref, kt_ref, vt_ref, pk_ref, pv_ref, o_ref, k_s, v_s, qq_s, s_s, m_s, acc_s = refs
        past_refs = (pk_ref, pv_ref)
    else:
        lam_ref, subg_ref, qt_ref, kt_ref, vt_ref, o_ref, k_s, v_s, qq_s, s_s, m_s, acc_s = refs
        past_refs = None
    qi = pl.program_id(2)

    @pl.when(qi == 0)
    def _():
        _load_kv_blocks(groups, tq, n_past, 0, kt_ref, vt_ref, past_refs, True, k_s, v_s)
        row = lax.broadcasted_iota(jnp.int32, (_ONES_ROWS, tq), 0)
        ones_rows = jnp.where(row == 0, 1.0, 0.0).astype(_BF16)
        for c in range(groups):
            for jb in range(v_s.shape[1]):
                v_s[c, jb, _LANES:, :] = ones_rows

    for c in range(groups):
        qq_s[c] = _stack_heads_t(qt_ref[0, c * _LANES:(c + 1) * _LANES, :])
    m_s[...] = jnp.full(m_s.shape, _NEG_INF, _F32)
    acc_s[...] = jnp.zeros(acc_s.shape, _F32)

    def scores(i):
        return [_dot(k_s[c, i], qq_s[c]) for c in range(groups)]

    def store(slot, blocks):
        for c, st in enumerate(blocks):
            s_s[slot, c] = st

    def update(slot, i, mask):
        for c in range(groups):
            st = s_s[slot, c]
            if mask is not None:
                st = jnp.where(mask, st, _NEG_INF)
            m_prev = m_s[c]
            m_next = jnp.maximum(m_prev, jnp.max(st, axis=0, keepdims=True))
            alpha = jnp.exp2(m_prev - m_next)
            pt = jnp.exp2(st - m_next)
            m_s[c] = m_next
            acc_s[c] = alpha * acc_s[c] + _dot(v_s[c, i], pt.astype(_BF16))

    def step(slot, i, nxt_i, mask=None):
        nxt = scores(nxt_i)
        update(slot, i, mask)
        store(1 - slot, nxt)

    last = n_past + qi
    kpos, qpos = _diag_offsets(tq)
    chunk_bits = _CHUNK.bit_length() - 1
    mask = (kpos >> chunk_bits) <= (qpos >> chunk_bits)
    if seq < tq:
        mask = mask & (kpos < seq)
    store(0, scores(last))
    step(0, last, 0, mask)

    def pair(j, carry):
        step(1, 2 * j, 2 * j + 1)
        step(0, 2 * j + 1, jnp.minimum(2 * j + 2, last))
        return carry

    lax.fori_loop(0, last // 2, pair, 0)

    @pl.when((last & 1) == 1)
    def _():
        update(1, last - 1, None)

    gain = _lane_tile(subg_ref[...], tq)
    for c in range(groups):
        acc = acc_s[c]
        out = acc[:_LANES] / acc[_LANES:_LANES + 1]
        o = out[:, :tq] - lam_ref[0:1, 0:1] * out[:, tq:]
        y = (_rms(o, axis=0) * gain) * (1.0 - lam_init)
        o_ref[:, c * _LANES:(c + 1) * _LANES] = y.T.astype(o_ref.dtype)


def _attn_b_kernel(*refs, tq, n_past, groups, seq):
    if n_past:
        qt_ref, kt_ref, vt_ref, pk_ref, pv_ref, o_ref, k_s, v_s, qq_s, s_s, u_s, c_s, acc_s = refs
        past_refs = (pk_ref, pv_ref)
    else:
        qt_ref, kt_ref, vt_ref, o_ref, k_s, v_s, qq_s, s_s, u_s, c_s, acc_s = refs
        past_refs = None
    qi = pl.program_id(2)

    @pl.when(qi == 0)
    def _():
        for c in range(groups):
            k_s[c, 0] = jnp.zeros(k_s.shape[2:], _BF16)
            v_s[c, 0] = jnp.zeros(v_s.shape[2:], _BF16)
        _load_kv_blocks(groups, tq, n_past, 1, kt_ref, vt_ref, past_refs, False, k_s, v_s)

    for c in range(groups):
        qq_s[c] = _stack_heads_t(qt_ref[0, c * _LANES:(c + 1) * _LANES, :])
    s_idx = lax.broadcasted_iota(jnp.int32, (tq, 2 * tq), 0)
    j_idx = lax.broadcasted_iota(jnp.int32, (tq, 2 * tq), 1) & (tq - 1)
    u_s[...] = jnp.where(j_idx >= s_idx, -1.0, 0.0).astype(_BF16)
    c_s[...] = jnp.zeros(c_s.shape, _F32)
    acc_s[...] = jnp.zeros(acc_s.shape, _F32)

    def logits(i):
        return [_dot(k_s[c, i], qq_s[c]) for c in range(groups)]

    def update(slot, i, mask):
        rests = []
        for c in range(groups):
            zt = s_s[slot, c]
            neg_log_rest = jnp.maximum(zt, 0.0) + jnp.log(1.0 + jnp.exp2(jnp.abs(zt) * (-_LOG2_E)))
            if mask is not None:
                neg_log_rest = jnp.where(mask, neg_log_rest, 0.0)
            hi, lo = _split(neg_log_rest)
            rests.append(_dot(u_s[...], jnp.concatenate([hi, lo], axis=0)))
        for c, rest_from_here in enumerate(rests):
            wt = jnp.exp(s_s[slot, c] + rest_from_here + c_s[c])
            if mask is not None:
                wt = jnp.where(mask, wt, 0.0)
            acc_s[c] = acc_s[c] + _dot(v_s[c, i], wt.astype(_BF16))
            c_s[c] = c_s[c] + rest_from_here[0:1, :]

    def carry_max():
        return functools.reduce(jnp.maximum, [jnp.max(c_s[c]) for c in range(groups)])

    top = n_past + qi + 1

    def store(slot, blocks):
        for c, zt in enumerate(blocks):
            s_s[slot, c] = zt

    def step(slot, i, nxt_i, mask=None):
        nxt = logits(nxt_i)
        update(slot, i, mask)
        store(1 - slot, nxt)

    kpos, qpos = _diag_offsets(tq)
    store(0, logits(top))
    step(0, top, top - 1, kpos < qpos)
    step(1, top - 1, jnp.maximum(top - 2, 0))

    def pair(state):
        i = state[0]
        step(0, i, i - 1)
        step(1, i - 1, jnp.maximum(i - 2, 0))
        return i - 2, carry_max()

    lax.while_loop(lambda s: (s[0] >= 1) & (s[1] > _EXP_ZERO_BELOW), pair, (top - 2, carry_max()))

    row = lax.broadcasted_iota(jnp.int32, (_LANES, tq), 0)
    for c in range(groups):
        acc = acc_s[c]
        out = jnp.where(row < _HEAD_DIM, acc[:, :tq], acc[:, tq:])
        o_ref[:, c * _LANES:(c + 1) * _LANES] = out.T.astype(o_ref.dtype)


def _attn_call(kind, qt, kt, vt, past, seq, extra_inputs, **kernel_kwargs):
    batch, d, seq_pad = qt.shape
    tq = min(_ATTN_TILE, seq_pad)
    assert seq_pad % tq == 0 and tq % _LANES == 0
    nq = seq_pad // tq
    groups = _ATTN_GROUPS
    gw = groups * _LANES
    assert d % gw == 0
    const = lambda a: pl.BlockSpec(a.shape, lambda b, g, i: (0,) * a.ndim)
    q_spec = pl.BlockSpec((1, gw, tq), lambda b, g, i: (b, g, i))
    kv_spec = pl.BlockSpec((1, gw, seq_pad), lambda b, g, i: (b, g, 0))
    inputs = list(extra_inputs) + [qt, kt, vt]
    in_specs = [const(a) for a in extra_inputs] + [q_spec, kv_spec, kv_spec]
    n_past = 0
    if past is not None:
        pkt, pv = past
        past_len = pkt.shape[2]
        assert past_len % tq == 0 and past_len % _CHUNK == 0
        n_past = past_len // tq
        inputs += [pkt, pv]
        in_specs.append(pl.BlockSpec((1, gw, past_len), lambda b, g, i: (b, g, 0)))
        if kind == "a":
            in_specs.append(pl.BlockSpec((past_len, gw), lambda b, g, i: (b, g)))
        else:
            in_specs.append(pl.BlockSpec((1, gw, past_len), lambda b, g, i: (b, g, 0)))
    n_blocks = n_past + nq + (1 if kind == "b" else 0)
    v_rows = _LANES + (_ONES_ROWS if kind == "a" else 0)
    scratch = [
        pltpu.VMEM((groups, n_blocks, tq, _LANES), _BF16),
        pltpu.VMEM((groups, n_blocks, v_rows, tq), _BF16),
        pltpu.VMEM((groups, _LANES, 2 * tq), _BF16),
        pltpu.VMEM((2, groups, tq, 2 * tq), _F32),
    ]
    row_state = pltpu.VMEM((groups, 1, 2 * tq), _F32)
    acc_state = pltpu.VMEM((groups, v_rows, 2 * tq), _F32)
    if kind == "a":
        body = functools.partial(_attn_a_kernel, tq=tq, n_past=n_past, groups=groups, seq=seq, **kernel_kwargs)
        scratch += [row_state, acc_state]
    else:
        body = functools.partial(_attn_b_kernel, tq=tq, n_past=n_past, groups=groups, seq=seq)
        scratch += [pltpu.VMEM((tq, 2 * tq), _BF16), row_state, acc_state]
    return pl.pallas_call(
        body,
        grid=(batch, d // gw, nq),
        in_specs=in_specs,
        out_specs=pl.BlockSpec((tq, gw), lambda b, g, i: (b * nq + i, g)),
        out_shape=jax.ShapeDtypeStruct((batch * seq_pad, d), _BF16),
        scratch_shapes=scratch,
        compiler_params=_compiler_params(("arbitrary", "arbitrary", "arbitrary")),
        name="attn_" + kind,
    )(*inputs)


def _router_gate(logits, n_experts, n_groups):
    lane_i = lax.broadcasted_iota(jnp.int32, logits.shape, 1)
    lane = lane_i.astype(_F32)
    group_of_lane = (lane_i >> (_EXP_PER_GROUP.bit_length() - 1)).astype(_F32)
    big = float(4 * _LANES)
    row_max = lambda v: jnp.max(v, axis=1, keepdims=True)
    row_min = lambda v: jnp.min(v, axis=1, keepdims=True)
    row_sum = lambda v: jnp.sum(v, axis=1, keepdims=True)

    is_group = (lane_i >= n_experts) & (lane_i < n_experts + n_groups)
    lg = jnp.where(is_group, logits, _NEG_INF)
    eg = jnp.where(is_group, jnp.exp(lg - row_max(lg)), 0.0)
    pg = eg / row_sum(eg)
    pg_sel = row_max(pg)
    g_sel = row_min(jnp.where(is_group & (pg == pg_sel), lane - n_experts, big))

    in_group = (lane_i < n_experts) & (group_of_lane == g_sel)
    le = jnp.where(in_group, logits, _NEG_INF)
    ee = jnp.where(in_group, jnp.exp(le - row_max(le)), 0.0)
    pe = ee / row_sum(ee)
    p1 = row_max(pe)
    i1 = row_min(jnp.where(in_group & (pe == p1), lane, big))
    rest = in_group & (lane != i1)
    p2 = row_max(jnp.where(rest, pe, -1.0))
    i2 = row_min(jnp.where(rest & (pe == p2), lane, big))
    total = p1 + p2
    gate = jnp.where(lane == i1, p1 / total * pg_sel, 0.0) + jnp.where(lane == i2, p2 / total * pg_sel, 0.0)
    return jnp.where(lane_i == n_experts, g_sel, gate)


def _outproj_kernel(mod_ref, x_ref, oa_ref, ob_ref, n1_ref, n2_ref, wg_ref, woa_ref, wob_ref, wout_ref,
                    wr_hi_ref, wr_lo_ref, br_ref, x1_ref, h2_ref, gate_ref, *, seq, n_batch, n_experts, n_groups):
    d = x_ref.shape[1]
    rows = lambda k: _mod_rows(mod_ref, k, d, seq, n_batch)
    x = x_ref[...]
    h = _adaln(x, n1_ref[...], rows(1), rows(0)).astype(_BF16)
    gate_a = _sigmoid(_dot(h, wg_ref[:, :d]))
    gate_b = _sigmoid(_dot(h, wg_ref[:, d:]))
    mix = gate_a * _dot(oa_ref[...], woa_ref[...]) + gate_b * _dot(ob_ref[...], wob_ref[...])
    x1 = x + rows(2) * _dot(mix.astype(_BF16), wout_ref[...])
    x1_ref[...] = x1
    h2 = _adaln(x1, n2_ref[...], rows(4), rows(3))
    h2_ref[...] = h2.astype(_BF16)
    h2_hi, h2_lo = _split(h2)
    logits = _dot(h2_hi, wr_hi_ref[...]) + (_dot(h2_hi, wr_lo_ref[...]) + _dot(h2_lo, wr_hi_ref[...])) + br_ref[...]
    gate_ref[...] = _router_gate(logits, n_experts, n_groups)


def _outproj_call(x2, oa, ob, mod3, mod_off, seq, norm1_g, norm2_g, w_gate_bf, w_oa_bf, w_ob_bf, w_out_bf,
                  w_router, b_router, n_experts, n_groups):
    n, d = x2.shape
    tm, n_batch, tiles_per_batch = _token_tiling(n, seq, _OUTPROJ_TILE)
    wr_hi, wr_lo = _split(w_router)
    row_spec = lambda: pl.BlockSpec((tm, d), lambda i: (i, 0))
    full = lambda a: pl.BlockSpec(a.shape, lambda i: (0,) * a.ndim, pipeline_mode=pl.Buffered(1))
    return pl.pallas_call(
        functools.partial(_outproj_kernel, seq=seq, n_batch=n_batch, n_experts=n_experts, n_groups=n_groups),
        grid=(n // tm,),
        in_specs=[
            _mod_spec(mod3.shape[2], n_batch, tiles_per_batch, mod_off),
            row_spec(), row_spec(), row_spec(),
            pl.BlockSpec((1, d), lambda i: (0, 0)), pl.BlockSpec((1, d), lambda i: (0, 0)),
            full(w_gate_bf), full(w_oa_bf), full(w_ob_bf), full(w_out_bf), full(wr_hi), full(wr_lo),
            pl.BlockSpec((1, _LANES), lambda i: (0, 0)),
        ],
        out_specs=[row_spec(), row_spec(), pl.BlockSpec((tm, _LANES), lambda i: (i, 0))],
        out_shape=[
            jax.ShapeDtypeStruct((n, d), _F32),
            jax.ShapeDtypeStruct((n, d), _BF16),
            jax.ShapeDtypeStruct((n, _LANES), _F32),
        ],
        compiler_params=_compiler_params(("arbitrary",)),
        name="outproj",
    )(mod3, x2, oa, ob, norm1_g.reshape(1, d), norm2_g.reshape(1, d), w_gate_bf, w_oa_bf, w_ob_bf, w_out_bf,
      wr_hi, wr_lo, b_router)


def _split3(x):
    hi = x.astype(_BF16)
    r = x - hi.astype(_F32)
    mid = r.astype(_BF16)
    return hi, mid, (r - mid.astype(_F32)).astype(_BF16)


def _moe_chunk_rows(tm, n_groups):
    return -(-int(tm / n_groups * _MOE_CHUNK_SLACK) // 32) * 32


def _moe_kernel(mod_ref, x1_ref, h2_ref, gate_ref, w1_ref, w3_ref, w2_ref, y_ref,
                gpack_s, rank_col_s, rank_row_s, count_s, *, seq, n_batch, n_experts, n_groups):
    g = pl.program_id(1)
    tm, d = x1_ref.shape
    chunk = _moe_chunk_rows(tm, n_groups)
    lane = lax.broadcasted_iota(jnp.int32, (tm, _LANES), 1)
    assert 3 * n_experts <= _LANES and n_experts & (n_experts - 1) == 0

    @pl.when(g == 0)
    def _():
        y_ref[...] = jnp.zeros(y_ref.shape, _F32)
        gate = gate_ref[...]
        hi, mid, lo = [p.astype(_F32) for p in _split3(jnp.where(lane < n_experts, gate, 0.0))]
        gpack_s[...] = (hi + pltpu.roll(mid, n_experts, 1) + pltpu.roll(lo, 2 * n_experts, 1)).astype(_BF16)
        group_id = gate[:, n_experts:n_experts + 1]
        member = jnp.where((lane < n_groups) & (lane.astype(_F32) == group_id), 1.0, 0.0)
        t_idx = lax.broadcasted_iota(jnp.int32, (tm, tm), 0)
        u_idx = lax.broadcasted_iota(jnp.int32, (tm, tm), 1)
        earlier = jnp.where(u_idx < t_idx, 1.0, 0.0).astype(_BF16)
        rank_col = jnp.where(member > 0.0, _dot(earlier, member.astype(_BF16)), -1.0)
        rank_col_s[...] = rank_col
        rank_row = rank_col.T
        for k in range(n_groups):
            rank_row_s[k] = jnp.broadcast_to(rank_row[k:k + 1, :], (_SUBLANES, tm))
        count_s[...] = jnp.sum(member, axis=0, keepdims=True)

    lane_row = lax.broadcasted_iota(jnp.int32, (1, _LANES), 1)
    count = jnp.sum(jnp.where(lane_row == g, count_s[...], 0.0)).astype(jnp.int32)
    n_chunks = (count + (chunk - 1)) // chunk
    rank_c = jnp.sum(jnp.where(lane == g, rank_col_s[...], 0.0), axis=1, keepdims=True)
    rank_r = jnp.concatenate([rank_row_s[g]] * (chunk // _SUBLANES), axis=0)
    slot_r = lax.broadcasted_iota(jnp.int32, (chunk, tm), 0).astype(_F32)
    slot_c = lax.broadcasted_iota(jnp.int32, (tm, chunk), 1).astype(_F32)
    lane_c = lax.broadcasted_iota(jnp.int32, (chunk, _LANES), 1)
    experts_per_group = w1_ref.shape[0]

    def chunk_body(k, carry):
        base = (k * chunk).astype(_F32)
        pick = jnp.where(rank_r == slot_r + base, 1.0, 0.0).astype(_BF16)
        put = jnp.where(rank_c == slot_c + base, 1.0, 0.0).astype(_BF16)
        xc = _dot(pick, h2_ref[...]).astype(_BF16)
        gc = _dot(pick, gpack_s[...])
        acc = jnp.zeros((chunk, d), _F32)
        for e in range(experts_per_group):
            mine = (lane_c & (n_experts - 1)) == g * experts_per_group + e
            ge = jnp.sum(jnp.where(mine, gc, 0.0), axis=1, keepdims=True)
            hid = _silu(_dot(xc, w1_ref[e])) * _dot(xc, w3_ref[e])
            acc = acc + _dot((hid * ge).astype(_BF16), w2_ref[e])
        y_ref[...] += _dot(put, acc.astype(_BF16))
        return carry

    lax.fori_loop(0, n_chunks, chunk_body, 0)

    @pl.when(g == pl.num_programs(1) - 1)
    def _():
        y_ref[...] = x1_ref[...] + _mod_rows(mod_ref, 5, d, seq, n_batch) * y_ref[...]


def _moe_call(x1, h2, gate, mod3, mod_off, seq, w1_bf, w3_bf, w2_bf, n_groups):
    n, d = x1.shape
    tm, n_batch, tiles_per_batch = _token_tiling(n, seq, _MOE_TILE)
    n_experts, _, d_expert = w1_bf.shape
    eg = n_experts // n_groups
    assert eg == _EXP_PER_GROUP
    row_spec = lambda w, **kw: pl.BlockSpec((tm, w), lambda i, g: (i, 0), **kw)
    return pl.pallas_call(
        functools.partial(_moe_kernel, seq=seq, n_batch=n_batch, n_experts=n_experts, n_groups=n_groups),
        grid=(n // tm, n_groups),
        in_specs=[
            _mod_spec(mod3.shape[2], n_batch, tiles_per_batch, mod_off),
            row_spec(d, pipeline_mode=pl.Buffered(1)), row_spec(d), row_spec(_LANES),
            pl.BlockSpec((eg, d, d_expert), lambda i, g: (g, 0, 0)),
            pl.BlockSpec((eg, d, d_expert), lambda i, g: (g, 0, 0)),
            pl.BlockSpec((eg, d_expert, d), lambda i, g: (g, 0, 0)),
        ],
        out_specs=row_spec(d),
        out_shape=jax.ShapeDtypeStruct((n, d), _F32),
        scratch_shapes=[
            pltpu.VMEM((tm, _LANES), _BF16),
            pltpu.VMEM((tm, _LANES), _F32),
            pltpu.VMEM((n_groups, _SUBLANES, tm), _F32),
            pltpu.VMEM((1, _LANES), _F32),
        ],
        compiler_params=_compiler_params(("arbitrary", "arbitrary"), _MOE_VMEM_LIMIT_BYTES),
        name="moe",
    )(mod3, x1, h2, gate, w1_bf, w3_bf, w2_bf)


def _feature_major(c):
    b, t, h, dh = c.shape
    return jnp.transpose(c, (0, 2, 3, 1)).reshape(b, h * dh, t)


def _token_major(ct, head_dim):
    b, d, t = ct.shape
    return jnp.transpose(ct.reshape(b, d // head_dim, head_dim, t), (0, 3, 1, 2))


def _layer(x, mod3, mod_off, lam, lam_init, past, p):
    batch, seq, d = x.shape
    x2 = x.reshape(batch * seq, d)
    past_len = 0 if past is None else past[0].shape[1]
    pos = past_len + jnp.arange(seq)
    qa_t, ka_t, va, va_t, qs_t, ks_t, vs_t = _proj_call(
        x2, mod3, mod_off, batch, seq, pos, p["norm1_g"], p["w_qkv_t"], p["gq"], p["gk"])
    seq_pad = -(-seq // _LANES) * _LANES
    pad = lambda a: a if seq_pad == seq else jnp.pad(a, ((0, 0), (0, 0), (0, seq_pad - seq)))
    past_a = past_b = None
    if past is not None:
        past_a = (_feature_major(past[0]), past[1].reshape(batch * past_len, d))
        past_b = (_feature_major(past[2]), _feature_major(past[3]))
    oa = _attn_call("a", pad(qa_t), pad(ka_t), pad(va_t), past_a, seq, [lam, p["subln_g"]], lam_init=lam_init)
    ob = _attn_call("b", pad(qs_t), pad(ks_t), pad(vs_t), past_b, seq, [])
    if seq_pad != seq:
        unpad = lambda o: o.reshape(batch, seq_pad, d)[:, :seq].reshape(batch * seq, d)
        oa, ob = unpad(oa), unpad(ob)
    x1, h2, gate = _outproj_call(x2, oa, ob, mod3, mod_off, seq, p["norm1_g"], p["norm2_g"], p["w_gate"],
                                 p["w_oa"], p["w_ob"], p["w_out"], p["w_router"], p["b_router"],
                                 p["n_experts"], p["n_groups"])
    y = _moe_call(x1, h2, gate, mod3, mod_off, seq, p["w1"], p["w3"], p["w2"], p["n_groups"])
    new = (_token_major(ka_t, _HEAD_DIM), va.reshape(batch, seq, -1, 2 * _HEAD_DIM),
           _token_major(ks_t, _HEAD_DIM), _token_major(vs_t, _HEAD_DIM))
    return y.reshape(batch, seq, d), new


def kernel(x_prompt, x_sample, cache_a_k, cache_a_v, cache_b_k, cache_b_v, c_prompt, c_sample, norm1_g, norm2_g, w_ada, b_ada, w_in, a_qnorm_g, a_knorm_g, a_lam_q1, a_lam_k1, a_lam_q2, a_lam_k2, a_subln_g, w_oa, w_ob, w_out, w_rg, b_rg, w_re, b_re, w1, w3, w2):
    depth, d = norm1_g.shape
    n_groups, n_experts = w_rg.shape[2], w_re.shape[2]
    assert w_in.shape[2] == 8 * d and n_experts + n_groups <= _LANES
    batch_p = x_prompt.shape[0]
    xp, xs = x_prompt, x_sample
    c_all = jnp.concatenate([c_prompt, c_sample], axis=0)
    rows_p, rows_s = [], []
    for l in range(depth):
        lam_init = 0.8 - 0.6 * math.exp(-0.3 * l)
        lam_vecs = jnp.stack([a_lam_q1[l], a_lam_k1[l], a_lam_q2[l], a_lam_k2[l]])
        mod, lam = _ada_call(c_all, w_ada[l], b_ada[l], lam_vecs, lam_init)
        mod3 = mod.reshape(mod.shape[0], 1, mod.shape[1])
        pad = _LANES - n_experts - n_groups
        params = dict(
            norm1_g=norm1_g[l], norm2_g=norm2_g[l],
            w_qkv_t=w_in[l][:, :6 * d].T.astype(_BF16), w_gate=w_in[l][:, 6 * d:].astype(_BF16),
            gq=a_qnorm_g[l], gk=a_knorm_g[l],
            subln_g=jnp.broadcast_to(a_subln_g[l].reshape(-1, 1), (a_subln_g.shape[1], _LANES)),
            w_oa=w_oa[l].astype(_BF16), w_ob=w_ob[l].astype(_BF16), w_out=w_out[l].astype(_BF16),
            w_router=jnp.pad(jnp.concatenate([w_re[l], w_rg[l]], axis=1), ((0, 0), (0, pad))),
            b_router=jnp.pad(jnp.concatenate([b_re[l], b_rg[l]]), (0, pad)).reshape(1, _LANES),
            w1=w1[l].astype(_BF16), w3=w3[l].astype(_BF16), w2=w2[l].astype(_BF16),
            n_experts=n_experts, n_groups=n_groups,
        )
        xp, new_p = _layer(xp, mod3, 0, lam, lam_init, None, params)
        past = (cache_a_k[l], cache_a_v[l], cache_b_k[l], cache_b_v[l])
        xs, new_s = _layer(xs, mod3, batch_p, lam, lam_init, past, params)
        rows_p.append(new_p)
        rows_s.append(new_s)
    stack = lambda rows, k: jnp.stack([r[k] for r in rows])
    return (xp, xs, stack(rows_p, 0), stack(rows_p, 1), stack(rows_p, 2), stack(rows_p, 3),
            stack(rows_s, 0), stack(rows_s, 1), stack(rows_s, 2), stack(rows_s, 3))
```

```python
import functools
import math

import jax
import jax.numpy as jnp
from jax import lax
from jax.experimental import pallas as pl
from jax.experimental.pallas import tpu as pltpu

_F32 = jnp.float32
_BF16 = jnp.bfloat16

_LANES = 128
_SUBLANES = 8
_MXU_DIM = 256
_VMEM_LIMIT_BYTES = 48 * 1024 * 1024

_CHUNK = 64
_HEAD_DIM = 64
_ROPE_DIMS = _HEAD_DIM // 4
_ROPE_THETA = 500000.0
_EXP_PER_GROUP = 8
_EPS = 1e-6
_NEG_INF = -1e30
_Q_SCALE = 1.0 / math.sqrt(_HEAD_DIM)
_EXP_ZERO_BELOW = -104.0
_LOG2_E = math.log2(math.e)

_TOKEN_TILE = 512
_OUTPROJ_TILE = 512
_ATTN_TILE = _MXU_DIM
_ATTN_GROUPS = 4
_ONES_ROWS = 16
_MOE_TILE = 1024
_MOE_CHUNK_SLACK = 1.25
_MOE_VMEM_LIMIT_BYTES = 56 * 1024 * 1024


def _dot(a, b):
    return jnp.dot(a, b, preferred_element_type=_F32)


def _split(x):
    hi = x.astype(_BF16)
    lo = (x - hi.astype(_F32)).astype(_BF16)
    return hi, lo


def _dot3(a, b):
    a_hi, a_lo = _split(a)
    b_hi, b_lo = _split(b)
    return _dot(a_hi, b_hi) + (_dot(a_hi, b_lo) + _dot(a_lo, b_hi))


def _silu(x):
    return x / (1.0 + jnp.exp(-x))


def _sigmoid(x):
    return 1.0 / (1.0 + jnp.exp(-x))


def _rms(x, axis=-1):
    return x * lax.rsqrt(jnp.mean(x * x, axis=axis, keepdims=True) + _EPS)


def _mod_rows(mod_ref, k, d, seq, n_batch):
    if n_batch == 1:
        return mod_ref[0, :, k * d:(k + 1) * d]
    rows = [jnp.broadcast_to(mod_ref[b, :, k * d:(k + 1) * d], (seq, d)) for b in range(n_batch)]
    return jnp.concatenate(rows, axis=0)


def _lane_tile(x, n):
    assert n % _LANES == 0
    return jnp.concatenate([x] * (n // _LANES), axis=1) if n > _LANES else x


def _compiler_params(semantics, vmem_limit_bytes=_VMEM_LIMIT_BYTES):
    return pltpu.CompilerParams(dimension_semantics=semantics, vmem_limit_bytes=vmem_limit_bytes)


def _ada_kernel(c_ref, w_ref, b_ref, lam_ref, mod_ref, lam_out_ref, *, lam_init):
    mod_ref[...] = _dot3(_silu(c_ref[...]), w_ref[...]) + b_ref[...]
    lv = lam_ref[...]
    s1 = jnp.sum(lv[0:1] * lv[1:2], axis=-1, keepdims=True)
    s2 = jnp.sum(lv[2:3] * lv[3:4], axis=-1, keepdims=True)
    lam = jnp.exp(s1) - jnp.exp(s2) + lam_init
    lam_out_ref[...] = jnp.broadcast_to(lam, lam_out_ref.shape)


def _ada_call(c_all, w_ada, b_ada, lam_vecs, lam_init):
    rows, d = c_all.shape
    cols = w_ada.shape[1]
    tn = d
    return pl.pallas_call(
        functools.partial(_ada_kernel, lam_init=lam_init),
        grid=(cols // tn,),
        in_specs=[
            pl.BlockSpec((rows, d), lambda j: (0, 0)),
            pl.BlockSpec((d, tn), lambda j: (0, j)),
            pl.BlockSpec((1, tn), lambda j: (0, j)),
            pl.BlockSpec(lam_vecs.shape, lambda j: (0, 0)),
        ],
        out_specs=[
            pl.BlockSpec((rows, tn), lambda j: (0, j)),
            pl.BlockSpec((_SUBLANES, _LANES), lambda j: (0, 0)),
        ],
        out_shape=[
            jax.ShapeDtypeStruct((rows, cols), _F32),
            jax.ShapeDtypeStruct((_SUBLANES, _LANES), _F32),
        ],
        compiler_params=_compiler_params(("arbitrary",)),
        name="ada",
    )(c_all, w_ada, b_ada.reshape(1, cols), lam_vecs)


def _adaln(x, gain, scale, shift):
    return (_rms(x) * gain) * (1.0 + scale) + shift


def _proj_kernel(mod_ref, x_ref, n1_ref, w_ref, gq_ref, gk_ref, cos_ref, sin_ref,
                 qa_ref, ka_ref, va_ref, vat_ref, qs_ref, ks_ref, vs_ref, h_s, acc_s, *, seq, n_batch):
    tm, d = x_ref.shape
    shift = _mod_rows(mod_ref, 0, d, seq, n_batch)
    scale = _mod_rows(mod_ref, 1, d, seq, n_batch)
    h_s[...] = _adaln(x_ref[...], n1_ref[...], scale, shift).astype(_BF16)

    def proj(section):
        return _dot(h_s[...], w_ref[:, section * d:(section + 1) * d])

    def transposed(section, acc):
        acc_s[section % 2] = acc
        return acc_s[section % 2].T

    def proj_t(section):
        return transposed(section, proj(section))

    def store_t(out_ref, val):
        if n_batch == 1:
            out_ref[0] = val.astype(out_ref.dtype)
        else:
            pad = out_ref.shape[2] - seq
            for b in range(n_batch):
                out_ref[b, :, :seq] = val[:, b * seq:(b + 1) * seq].astype(out_ref.dtype)
                if pad:
                    out_ref[b, :, seq:] = jnp.zeros((val.shape[0], pad), out_ref.dtype)

    def head_norm_rope(acc, gain_ref, out_ref, out_scale):
        gain = _lane_tile(gain_ref[...], tm)
        cos, sin = cos_ref[...], sin_ref[...]
        half = _ROPE_DIMS // 2
        parts = []
        for h in range(d // _HEAD_DIM):
            y = _rms(acc[h * _HEAD_DIM:(h + 1) * _HEAD_DIM], axis=0) * gain
            x1, x2 = y[:half], y[half:2 * half]
            parts += [x1 * cos - x2 * sin, x2 * cos + x1 * sin, y[2 * half:]]
        out = jnp.concatenate(parts, axis=0)
        store_t(out_ref, out if out_scale == 1.0 else out * out_scale)

    head_norm_rope(proj_t(0), gq_ref, qa_ref, _Q_SCALE * _LOG2_E)
    head_norm_rope(proj_t(1), gk_ref, ka_ref, 1.0)
    va = proj(2)
    va_ref[...] = va
    store_t(vat_ref, transposed(2, va))
    store_t(qs_ref, proj_t(3) * _Q_SCALE)
    store_t(ks_ref, proj_t(4))
    store_t(vs_ref, proj_t(5))


def _rope_tables_t(pos):
    half = _ROPE_DIMS // 2
    inv_freq = jnp.exp(-math.log(_ROPE_THETA) * 2.0 * jnp.arange(half, dtype=_F32) / _ROPE_DIMS)
    ang = inv_freq[:, None] * pos.astype(_F32)[None, :]
    return jnp.cos(ang), jnp.sin(ang)


def _token_tiling(n_tokens, seq, tile=_TOKEN_TILE):
    tm = min(tile, n_tokens)
    if tm >= seq:
        assert tm % seq == 0 and n_tokens % tm == 0
        return tm, tm // seq, 1
    assert seq % tm == 0
    return tm, 1, seq // tm


def _mod_spec(width, n_batch, tiles_per_batch, mod_off):
    assert mod_off % n_batch == 0
    first = mod_off // n_batch
    return pl.BlockSpec((n_batch, 1, width), lambda i, *_: (first + i // tiles_per_batch, 0, 0))


def _proj_call(x2, mod3, mod_off, batch, seq, seq_pad, pos, norm1_g, w_qkv, gq, gk):
    n, d = x2.shape
    tm, n_batch, tiles_per_batch = _token_tiling(n, seq)
    assert seq_pad == seq or n_batch > 1
    cos, sin = _rope_tables_t(pos)
    if n_batch > 1:
        cos, sin = jnp.tile(cos, (1, n_batch)), jnp.tile(sin, (1, n_batch))
    lanes_t = tm if n_batch == 1 else seq_pad
    gain_t = lambda g: jnp.broadcast_to(g.reshape(_HEAD_DIM, 1), (_HEAD_DIM, _LANES))
    row_spec = pl.BlockSpec((tm, d), lambda i: (i, 0))
    t_spec = lambda: pl.BlockSpec((n_batch, d, lanes_t), lambda i: (i // tiles_per_batch, 0, i % tiles_per_batch))
    rope_spec = lambda: pl.BlockSpec((_ROPE_DIMS // 2, tm), lambda i: (0, i % tiles_per_batch))
    const = lambda shape: pl.BlockSpec(shape, lambda i: (0, 0))
    t_out = lambda dt: jax.ShapeDtypeStruct((batch, d, seq_pad), dt)
    return pl.pallas_call(
        functools.partial(_proj_kernel, seq=seq, n_batch=n_batch),
        grid=(n // tm,),
        in_specs=[
            _mod_spec(mod3.shape[2], n_batch, tiles_per_batch, mod_off),
            row_spec,
            const((1, d)),
            pl.BlockSpec(w_qkv.shape, lambda i: (0, 0), pipeline_mode=pl.Buffered(1)),
            const((_HEAD_DIM, _LANES)), const((_HEAD_DIM, _LANES)),
            rope_spec(), rope_spec(),
        ],
        out_specs=[t_spec(), t_spec(), row_spec, t_spec(), t_spec(), t_spec(), t_spec()],
        out_shape=[t_out(_BF16), t_out(_F32), jax.ShapeDtypeStruct((n, d), _F32), t_out(_BF16),
                   t_out(_BF16), t_out(_F32), t_out(_F32)],
        scratch_shapes=[pltpu.VMEM((tm, d), _BF16), pltpu.VMEM((2, tm, d), _F32)],
        compiler_params=_compiler_params(("arbitrary",)),
        name="proj",
    )(mod3, x2, norm1_g.reshape(1, d), w_qkv, gain_t(gq), gain_t(gk), cos, sin)


def _stack_heads_t(qt):
    row = lax.broadcasted_iota(jnp.int32, qt.shape, 0)
    zero = jnp.zeros_like(qt)
    return jnp.concatenate([jnp.where(row < _HEAD_DIM, qt, zero), jnp.where(row >= _HEAD_DIM, qt, zero)], axis=1)


def _diag_offsets(tq):
    assert tq & (tq - 1) == 0
    kpos = lax.broadcasted_iota(jnp.int32, (tq, 2 * tq), 0)
    qpos = lax.broadcasted_iota(jnp.int32, (tq, 2 * tq), 1) & (tq - 1)
    return kpos, qpos


def _load_kv_blocks(groups, tq, n_past, first, kt_ref, vt_ref, past_refs, past_values_token_major, k_s, v_s):
    g = _LANES
    for c in range(groups):
        rows = slice(c * g, (c + 1) * g)
        for jb in range(n_past):
            pk_ref, pv_ref = past_refs
            cols = slice(jb * tq, (jb + 1) * tq)
            k_s[c, first + jb] = pk_ref[0, rows, cols].T.astype(_BF16)
            if past_values_token_major:
                v_s[c, first + jb, :g, :] = pv_ref[cols, rows].T.astype(_BF16)
            else:
                v_s[c, first + jb, :g, :] = pv_ref[0, rows, cols].astype(_BF16)
        for jb in range(k_s.shape[1] - n_past - first):
            cols = slice(jb * tq, (jb + 1) * tq)
            k_s[c, first + n_past + jb] = kt_ref[0, rows, cols].T.astype(_BF16)
            v_s[c, first + n_past + jb, :g, :] = vt_ref[0, rows, cols].astype(_BF16)


def _attn_a_kernel(*refs, tq, n_past, groups, seq, lam_init):
    if n_past:
        lam_ref, subg_ref, qt_ref, kt_ref, vt_ref, pk_ref, pv_ref, o_ref, k_s, v_s, qq_s, s_s, m_s, acc_s = refs
        past_refs = (pk_ref, pv_ref)
    else:
        lam_ref, subg_ref, qt_ref, kt_ref, vt_ref, o_ref, k_s, v_s, qq_s, s_s, m_s, acc_s = refs
        past_refs = None
    qi = pl.program_id(2)

    @pl.when(qi == 0)
    def _():
        _load_kv_blocks(groups, tq, n_past, 0, kt_ref, vt_ref, past_refs, True, k_s, v_s)
        row = lax.broadcasted_iota(jnp.int32, (_ONES_ROWS, tq), 0)
        ones_rows = jnp.where(row == 0, 1.0, 0.0).astype(_BF16)
        for c in range(groups):
            for jb in range(v_s.shape[1]):
                v_s[c, jb, _LANES:, :] = ones_rows

    for c in range(groups):
        qq_s[c] = _stack_heads_t(qt_ref[0, c * _LANES:(c + 1) * _LANES, :])
    m_s[...] = jnp.full(m_s.shape, _NEG_INF, _F32)
    acc_s[...] = jnp.zeros(acc_s.shape, _F32)

    def scores(i):
        return [_dot(k_s[c, i], qq_s[c]) for c in range(groups)]

    def store(slot, blocks):
        for c, st in enumerate(blocks):
            s_s[slot, c] = st

    def update(slot, i, mask):
        for c in range(groups):
            st = s_s[slot, c]
            if mask is not None:
                st = jnp.where(mask, st, _NEG_INF)
            m_prev = m_s[c]
            m_next = jnp.maximum(m_prev, jnp.max(st, axis=0, keepdims=True))
            alpha = jnp.exp2(m_prev - m_next)
            pt = jnp.exp2(st - m_next)
            m_s[c] = m_next
            acc_s[c] = alpha * acc_s[c] + _dot(v_s[c, i], pt.astype(_BF16))

    def step(slot, i, nxt_i, mask=None):
        nxt = scores(nxt_i)
        update(slot, i, mask)
        store(1 - slot, nxt)

    last = n_past + qi
    kpos, qpos = _diag_offsets(tq)
    chunk_bits = _CHUNK.bit_length() - 1
    mask = (kpos >> chunk_bits) <= (qpos >> chunk_bits)
    if seq < tq:
        mask = mask & (kpos < seq)
    store(0, scores(last))
    step(0, last, 0, mask)

    def pair(j, carry):
        step(1, 2 * j, 2 * j + 1)
        step(0, 2 * j + 1, jnp.minimum(2 * j + 2, last))
        return carry

    lax.fori_loop(0, last // 2, pair, 0)

    @pl.when((last & 1) == 1)
    def _():
        update(1, last - 1, None)

    gain = _lane_tile(subg_ref[...], tq)
    for c in range(groups):
        acc = acc_s[c]
        out = acc[:_LANES] / acc[_LANES:_LANES + 1]
        o = out[:, :tq] - lam_ref[0:1, 0:1] * out[:, tq:]
        y = (_rms(o, axis=0) * gain) * (1.0 - lam_init)
        o_ref[:, c * _LANES:(c + 1) * _LANES] = y.T.astype(o_ref.dtype)


def _attn_b_kernel(*refs, tq, n_past, groups, seq):
    if n_past:
        qt_ref, kt_ref, vt_ref, pk_ref, pv_ref, o_ref, k_s, v_s, qq_s, s_s, u_s, c_s, acc_s = refs
        past_refs = (pk_ref, pv_ref)
    else:
        qt_ref, kt_ref, vt_ref, o_ref, k_s, v_s, qq_s, s_s, u_s, c_s, acc_s = refs
        past_refs = None
    qi = pl.program_id(2)

    @pl.when(qi == 0)
    def _():
        for c in range(groups):
            k_s[c, 0] = jnp.zeros(k_s.shape[2:], _BF16)
            v_s[c, 0] = jnp.zeros(v_s.shape[2:], _BF16)
        _load_kv_blocks(groups, tq, n_past, 1, kt_ref, vt_ref, past_refs, False, k_s, v_s)

    for c in range(groups):
        qq_s[c] = _stack_heads_t(qt_ref[0, c * _LANES:(c + 1) * _LANES, :])
    s_idx = lax.broadcasted_iota(jnp.int32, (tq, 2 * tq), 0)
    j_idx = lax.broadcasted_iota(jnp.int32, (tq, 2 * tq), 1) & (tq - 1)
    u_s[...] = jnp.where(j_idx >= s_idx, -1.0, 0.0).astype(_BF16)
    c_s[...] = jnp.zeros(c_s.shape, _F32)
    acc_s[...] = jnp.zeros(acc_s.shape, _F32)

    def logits(i):
        return [_dot(k_s[c, i], qq_s[c]) for c in range(groups)]

    def update(slot, i, mask):
        rests = []
        for c in range(groups):
            zt = s_s[slot, c]
            neg_log_rest = jnp.maximum(zt, 0.0) + jnp.log(1.0 + jnp.exp2(jnp.abs(zt) * (-_LOG2_E)))
            if mask is not None:
                neg_log_rest = jnp.where(mask, neg_log_rest, 0.0)
            hi, lo = _split(neg_log_rest)
            rests.append(_dot(u_s[...], jnp.concatenate([hi, lo], axis=0)))
        for c, rest_from_here in enumerate(rests):
            wt = jnp.exp(s_s[slot, c] + rest_from_here + c_s[c])
            if mask is not None:
                wt = jnp.where(mask, wt, 0.0)
            acc_s[c] = acc_s[c] + _dot(v_s[c, i], wt.astype(_BF16))
            c_s[c] = c_s[c] + rest_from_here[0:1, :]

    def carry_max():
        return functools.reduce(jnp.maximum, [jnp.max(c_s[c]) for c in range(groups)])

    top = n_past + qi + 1

    def store(slot, blocks):
        for c, zt in enumerate(blocks):
            s_s[slot, c] = zt

    def step(slot, i, nxt_i, mask=None):
        nxt = logits(nxt_i)
        update(slot, i, mask)
        store(1 - slot, nxt)

    kpos, qpos = _diag_offsets(tq)
    store(0, logits(top))
    step(0, top, top - 1, kpos < qpos)
    step(1, top - 1, jnp.maximum(top - 2, 0))

    def pair(state):
        i = state[0]
        step(0, i, i - 1)
        step(1, i - 1, jnp.maximum(i - 2, 0))
        return i - 2, carry_max()

    lax.while_loop(lambda s: (s[0] >= 1) & (s[1] > _EXP_ZERO_BELOW), pair, (top - 2, carry_max()))

    row = lax.broadcasted_iota(jnp.int32, (_LANES, tq), 0)
    for c in range(groups):
        acc = acc_s[c]
        out = jnp.where(row < _HEAD_DIM, acc[:, :tq], acc[:, tq:])
        o_ref[:, c * _LANES:(c + 1) * _LANES] = out.T.astype(o_ref.dtype)


def _attn_call(kind, qt, kt, vt, past, seq, extra_inputs, **kernel_kwargs):
    batch, d, seq_pad = qt.shape
    tq = min(_ATTN_TILE, seq_pad)
    assert seq_pad % tq == 0 and tq % _LANES == 0
    nq = seq_pad // tq
    groups = _ATTN_GROUPS
    gw = groups * _LANES
    assert d % gw == 0
    const = lambda a: pl.BlockSpec(a.shape, lambda b, g, i: (0,) * a.ndim)
    q_spec = pl.BlockSpec((1, gw, tq), lambda b, g, i: (b, g, i))
    kv_spec = pl.BlockSpec((1, gw, seq_pad), lambda b, g, i: (b, g, 0))
    inputs = list(extra_inputs) + [qt, kt, vt]
    in_specs = [const(a) for a in extra_inputs] + [q_spec, kv_spec, kv_spec]
    n_past = 0
    if past is not None:
        pkt, pv = past
        past_len = pkt.shape[2]
        assert past_len % tq == 0 and past_len % _CHUNK == 0
        n_past = past_len // tq
        inputs += [pkt, pv]
        in_specs.append(pl.BlockSpec((1, gw, past_len), lambda b, g, i: (b, g, 0)))
        if kind == "a":
            in_specs.append(pl.BlockSpec((past_len, gw), lambda b, g, i: (b, g)))
        else:
            in_specs.append(pl.BlockSpec((1, gw, past_len), lambda b, g, i: (b, g, 0)))
    n_blocks = n_past + nq + (1 if kind == "b" else 0)
    v_rows = _LANES + (_ONES_ROWS if kind == "a" else 0)
    scratch = [
        pltpu.VMEM((groups, n_blocks, tq, _LANES), _BF16),
        pltpu.VMEM((groups, n_blocks, v_rows, tq), _BF16),
        pltpu.VMEM((groups, _LANES, 2 * tq), _BF16),
        pltpu.VMEM((2, groups, tq, 2 * tq), _F32),
    ]
    row_state = pltpu.VMEM((groups, 1, 2 * tq), _F32)
    acc_state = pltpu.VMEM((groups, v_rows, 2 * tq), _F32)
    if kind == "a":
        body = functools.partial(_attn_a_kernel, tq=tq, n_past=n_past, groups=groups, seq=seq, **kernel_kwargs)
        scratch += [row_state, acc_state]
    else:
        body = functools.partial(_attn_b_kernel, tq=tq, n_past=n_past, groups=groups, seq=seq)
        scratch += [pltpu.VMEM((tq, 2 * tq), _BF16), row_state, acc_state]
    return pl.pallas_call(
        body,
        grid=(batch, d // gw, nq),
        in_specs=in_specs,
        out_specs=pl.BlockSpec((tq, gw), lambda b, g, i: (b * nq + i, g)),
        out_shape=jax.ShapeDtypeStruct((batch * seq_pad, d), _BF16),
        scratch_shapes=scratch,
        compiler_params=_compiler_params(("arbitrary", "arbitrary", "arbitrary")),
        name="attn_" + kind,
    )(*inputs)


def _router_gate(logits, n_experts, n_groups):
    lane_i = lax.broadcasted_iota(jnp.int32, logits.shape, 1)
    lane = lane_i.astype(_F32)
    group_of_lane = (lane_i >> (_EXP_PER_GROUP.bit_length() - 1)).astype(_F32)
    big = float(4 * _LANES)
    row_max = lambda v: jnp.max(v, axis=1, keepdims=True)
    row_min = lambda v: jnp.min(v, axis=1, keepdims=True)
    row_sum = lambda v: jnp.sum(v, axis=1, keepdims=True)

    is_group = (lane_i >= n_experts) & (lane_i < n_experts + n_groups)
    lg = jnp.where(is_group, logits, _NEG_INF)
    eg = jnp.where(is_group, jnp.exp(lg - row_max(lg)), 0.0)
    pg = eg / row_sum(eg)
    pg_sel = row_max(pg)
    g_sel = row_min(jnp.where(is_group & (pg == pg_sel), lane - n_experts, big))

    in_group = (lane_i < n_experts) & (group_of_lane == g_sel)
    le = jnp.where(in_group, logits, _NEG_INF)
    ee = jnp.where(in_group, jnp.exp(le - row_max(le)), 0.0)
    pe = ee / row_sum(ee)
    p1 = row_max(pe)
    i1 = row_min(jnp.where(in_group & (pe == p1), lane, big))
    rest = in_group & (lane != i1)
    p2 = row_max(jnp.where(rest, pe, -1.0))
    i2 = row_min(jnp.where(rest & (pe == p2), lane, big))
    total = p1 + p2
    gate = jnp.where(lane == i1, p1 / total * pg_sel, 0.0) + jnp.where(lane == i2, p2 / total * pg_sel, 0.0)
    return jnp.where(lane_i == n_experts, g_sel, gate)


def _outproj_kernel(mod_ref, x_ref, oa_ref, ob_ref, n1_ref, n2_ref, wg_ref, woa_ref, wob_ref, wout_ref,
                    wr_hi_ref, wr_lo_ref, br_ref, x1_ref, h2_ref, gate_ref, *, seq, n_batch, n_experts, n_groups):
    d = x_ref.shape[1]
    rows = lambda k: _mod_rows(mod_ref, k, d, seq, n_batch)
    x = x_ref[...]
    h = _adaln(x, n1_ref[...], rows(1), rows(0)).astype(_BF16)
    gate_a = _sigmoid(_dot(h, wg_ref[:, :d]))
    gate_b = _sigmoid(_dot(h, wg_ref[:, d:]))
    mix = gate_a * _dot(oa_ref[...], woa_ref[...]) + gate_b * _dot(ob_ref[...], wob_ref[...])
    x1 = x + rows(2) * _dot(mix.astype(_BF16), wout_ref[...])
    x1_ref[...] = x1
    h2 = _adaln(x1, n2_ref[...], rows(4), rows(3))
    h2_ref[...] = h2.astype(_BF16)
    h2_hi, h2_lo = _split(h2)
    logits = _dot(h2_hi, wr_hi_ref[...]) + (_dot(h2_hi, wr_lo_ref[...]) + _dot(h2_lo, wr_hi_ref[...])) + br_ref[...]
    gate_ref[...] = _router_gate(logits, n_experts, n_groups)


def _outproj_call(x2, oa, ob, mod3, mod_off, seq, norm1_g, norm2_g, w_gate_bf, w_oa_bf, w_ob_bf, w_out_bf,
                  w_router, b_router, n_experts, n_groups):
    n, d = x2.shape
    tm, n_batch, tiles_per_batch = _token_tiling(n, seq, _OUTPROJ_TILE)
    wr_hi, wr_lo = _split(w_router)
    row_spec = lambda: pl.BlockSpec((tm, d), lambda i: (i, 0))
    full = lambda a: pl.BlockSpec(a.shape, lambda i: (0,) * a.ndim, pipeline_mode=pl.Buffered(1))
    return pl.pallas_call(
        functools.partial(_outproj_kernel, seq=seq, n_batch=n_batch, n_experts=n_experts, n_groups=n_groups),
        grid=(n // tm,),
        in_specs=[
            _mod_spec(mod3.shape[2], n_batch, tiles_per_batch, mod_off),
            row_spec(), row_spec(), row_spec(),
            pl.BlockSpec((1, d), lambda i: (0, 0)), pl.BlockSpec((1, d), lambda i: (0, 0)),
            full(w_gate_bf), full(w_oa_bf), full(w_ob_bf), full(w_out_bf), full(wr_hi), full(wr_lo),
            pl.BlockSpec((1, _LANES), lambda i: (0, 0)),
        ],
        out_specs=[row_spec(), row_spec(), pl.BlockSpec((tm, _LANES), lambda i: (i, 0))],
        out_shape=[
            jax.ShapeDtypeStruct((n, d), _F32),
            jax.ShapeDtypeStruct((n, d), _BF16),
            jax.ShapeDtypeStruct((n, _LANES), _F32),
        ],
        compiler_params=_compiler_params(("arbitrary",)),
        name="outproj",
    )(mod3, x2, oa, ob, norm1_g.reshape(1, d), norm2_g.reshape(1, d), w_gate_bf, w_oa_bf, w_ob_bf, w_out_bf,
      wr_hi, wr_lo, b_router)


def _split3(x):
    hi = x.astype(_BF16)
    r = x - hi.astype(_F32)
    mid = r.astype(_BF16)
    return hi, mid, (r - mid.astype(_F32)).astype(_BF16)


def _moe_chunk_rows(tm, n_groups):
    return -(-int(tm / n_groups * _MOE_CHUNK_SLACK) // 32) * 32


def _moe_kernel(mod_ref, x1_ref, h2_ref, gate_ref, w1_ref, w3_ref, w2_ref, y_ref,
                gpack_s, rank_col_s, rank_row_s, count_s, *, seq, n_batch, n_experts, n_groups):
    g = pl.program_id(1)
    tm, d = x1_ref.shape
    chunk = _moe_chunk_rows(tm, n_groups)
    lane = lax.broadcasted_iota(jnp.int32, (tm, _LANES), 1)
    assert 3 * n_experts <= _LANES and n_experts & (n_experts - 1) == 0

    @pl.when(g == 0)
    def _():
        y_ref[...] = jnp.zeros(y_ref.shape, _F32)
        gate = gate_ref[...]
        hi, mid, lo = [p.astype(_F32) for p in _split3(jnp.where(lane < n_experts, gate, 0.0))]
        gpack_s[...] = (hi + pltpu.roll(mid, n_experts, 1) + pltpu.roll(lo, 2 * n_experts, 1)).astype(_BF16)
        group_id = gate[:, n_experts:n_experts + 1]
        member = jnp.where((lane < n_groups) & (lane.astype(_F32) == group_id), 1.0, 0.0)
        t_idx = lax.broadcasted_iota(jnp.int32, (tm, tm), 0)
        u_idx = lax.broadcasted_iota(jnp.int32, (tm, tm), 1)
        earlier = jnp.where(u_idx < t_idx, 1.0, 0.0).astype(_BF16)
        rank_col = jnp.where(member > 0.0, _dot(earlier, member.astype(_BF16)), -1.0)
        rank_col_s[...] = rank_col
        rank_row = rank_col.T
        for k in range(n_groups):
            rank_row_s[k] = jnp.broadcast_to(rank_row[k:k + 1, :], (_SUBLANES, tm))
        count_s[...] = jnp.sum(member, axis=0, keepdims=True)

    lane_row = lax.broadcasted_iota(jnp.int32, (1, _LANES), 1)
    count = jnp.sum(jnp.where(lane_row == g, count_s[...], 0.0)).astype(jnp.int32)
    n_chunks = (count + (chunk - 1)) // chunk
    rank_c = jnp.sum(jnp.where(lane == g, rank_col_s[...], 0.0), axis=1, keepdims=True)
    rank_r = jnp.concatenate([rank_row_s[g]] * (chunk // _SUBLANES), axis=0)
    slot_r = lax.broadcasted_iota(jnp.int32, (chunk, tm), 0).astype(_F32)
    slot_c = lax.broadcasted_iota(jnp.int32, (tm, chunk), 1).astype(_F32)
    lane_c = lax.broadcasted_iota(jnp.int32, (chunk, _LANES), 1)
    experts_per_group = w1_ref.shape[0]

    def chunk_body(k, carry):
        base = (k * chunk).astype(_F32)
        pick = jnp.where(rank_r == slot_r + base, 1.0, 0.0).astype(_BF16)
        put = jnp.where(rank_c == slot_c + base, 1.0, 0.0).astype(_BF16)
        xc = _dot(pick, h2_ref[...]).astype(_BF16)
        gc = _dot(pick, gpack_s[...])
        acc = jnp.zeros((chunk, d), _F32)
        for e in range(experts_per_group):
            mine = (lane_c & (n_experts - 1)) == g * experts_per_group + e
            ge = jnp.sum(jnp.where(mine, gc, 0.0), axis=1, keepdims=True)
            hid = _silu(_dot(xc, w1_ref[e])) * _dot(xc, w3_ref[e])
            acc = acc + _dot((hid * ge).astype(_BF16), w2_ref[e])
        y_ref[...] += _dot(put, acc.astype(_BF16))
        return carry

    lax.fori_loop(0, n_chunks, chunk_body, 0)

    @pl.when(g == pl.num_programs(1) - 1)
    def _():
        y_ref[...] = x1_ref[...] + _mod_rows(mod_ref, 5, d, seq, n_batch) * y_ref[...]


def _moe_call(x1, h2, gate, mod3, mod_off, seq, w1_bf, w3_bf, w2_bf, n_groups):
    n, d = x1.shape
    tm, n_batch, tiles_per_batch = _token_tiling(n, seq, _MOE_TILE)
    n_experts, _, d_expert = w1_bf.shape
    eg = n_experts // n_groups
    assert eg == _EXP_PER_GROUP
    row_spec = lambda w, **kw: pl.BlockSpec((tm, w), lambda i, g: (i, 0), **kw)
    return pl.pallas_call(
        functools.partial(_moe_kernel, seq=seq, n_batch=n_batch, n_experts=n_experts, n_groups=n_groups),
        grid=(n // tm, n_groups),
        in_specs=[
            _mod_spec(mod3.shape[2], n_batch, tiles_per_batch, mod_off),
            row_spec(d, pipeline_mode=pl.Buffered(1)), row_spec(d), row_spec(_LANES),
            pl.BlockSpec((eg, d, d_expert), lambda i, g: (g, 0, 0)),
            pl.BlockSpec((eg, d, d_expert), lambda i, g: (g, 0, 0)),
            pl.BlockSpec((eg, d_expert, d), lambda i, g: (g, 0, 0)),
        ],
        out_specs=row_spec(d),
        out_shape=jax.ShapeDtypeStruct((n, d), _F32),
        scratch_shapes=[
            pltpu.VMEM((tm, _LANES), _BF16),
            pltpu.VMEM((tm, _LANES), _F32),
            pltpu.VMEM((n_groups, _SUBLANES, tm), _F32),
            pltpu.VMEM((1, _LANES), _F32),
        ],
        compiler_params=_compiler_params(("arbitrary", "arbitrary"), _MOE_VMEM_LIMIT_BYTES),
        name="moe",
    )(mod3, x1, h2, gate, w1_bf, w3_bf, w2_bf)


def _feature_major(c):
    b, t, h, dh = c.shape
    return jnp.transpose(c, (0, 2, 3, 1)).reshape(b, h * dh, t)


def _token_major(ct, head_dim):
    b, d, t = ct.shape
    return jnp.transpose(ct.reshape(b, d // head_dim, head_dim, t), (0, 3, 1, 2))


def _layer(x, mod3, mod_off, lam, lam_init, past, p):
    batch, seq, d = x.shape
    x2 = x.reshape(batch * seq, d)
    past_len = 0 if past is None else past[0].shape[1]
    pos = past_len + jnp.arange(seq)
    seq_pad = -(-seq // _LANES) * _LANES
    qa_t, ka_t, va, va_t, qs_t, ks_t, vs_t = _proj_call(
        x2, mod3, mod_off, batch, seq, seq_pad, pos, p["norm1_g"], p["w_qkv"], p["gq"], p["gk"])
    past_a = past_b = None
    if past is not None:
        past_a = (_feature_major(past[0]), past[1].reshape(batch * past_len, d))
        past_b = (_feature_major(past[2]), _feature_major(past[3]))
    oa = _attn_call("a", qa_t, ka_t, va_t, past_a, seq, [lam, p["subln_g"]], lam_init=lam_init)
    ob = _attn_call("b", qs_t, ks_t, vs_t, past_b, seq, [])
    if seq_pad != seq:
        unpad = lambda o: o.reshape(batch, seq_pad, d)[:, :seq].reshape(batch * seq, d)
        oa, ob = unpad(oa), unpad(ob)
        ka_t, ks_t, vs_t = ka_t[:, :, :seq], ks_t[:, :, :seq], vs_t[:, :, :seq]
    x1, h2, gate = _outproj_call(x2, oa, ob, mod3, mod_off, seq, p["norm1_g"], p["norm2_g"], p["w_gate"],
                                 p["w_oa"], p["w_ob"], p["w_out"], p["w_router"], p["b_router"],
                                 p["n_experts"], p["n_groups"])
    y = _moe_call(x1, h2, gate, mod3, mod_off, seq, p["w1"], p["w3"], p["w2"], p["n_groups"])
    new = (_token_major(ka_t, _HEAD_DIM), va.reshape(batch, seq, -1, 2 * _HEAD_DIM),
           _token_major(ks_t, _HEAD_DIM), _token_major(vs_t, _HEAD_DIM))
    return y.reshape(batch, seq, d), new


def kernel(x_prompt, x_sample, cache_a_k, cache_a_v, cache_b_k, cache_b_v, c_prompt, c_sample, norm1_g, norm2_g, w_ada, b_ada, w_in, a_qnorm_g, a_knorm_g, a_lam_q1, a_lam_k1, a_lam_q2, a_lam_k2, a_subln_g, w_oa, w_ob, w_out, w_rg, b_rg, w_re, b_re, w1, w3, w2):
    depth, d = norm1_g.shape
    n_groups, n_experts = w_rg.shape[2], w_re.shape[2]
    assert w_in.shape[2] == 8 * d and n_experts + n_groups <= _LANES
    batch_p = x_prompt.shape[0]
    xp, xs = x_prompt, x_sample
    c_all = jnp.concatenate([c_prompt, c_sample], axis=0)
    rows_p, rows_s = [], []
    for l in range(depth):
        lam_init = 0.8 - 0.6 * math.exp(-0.3 * l)
        lam_vecs = jnp.stack([a_lam_q1[l], a_lam_k1[l], a_lam_q2[l], a_lam_k2[l]])
        mod, lam = _ada_call(c_all, w_ada[l], b_ada[l], lam_vecs, lam_init)
        mod3 = mod.reshape(mod.shape[0], 1, mod.shape[1])
        pad = _LANES - n_experts - n_groups
        params = dict(
            norm1_g=norm1_g[l], norm2_g=norm2_g[l],
            w_qkv=w_in[l][:, :6 * d].astype(_BF16), w_gate=w_in[l][:, 6 * d:].astype(_BF16),
            gq=a_qnorm_g[l], gk=a_knorm_g[l],
            subln_g=jnp.broadcast_to(a_subln_g[l].reshape(-1, 1), (a_subln_g.shape[1], _LANES)),
            w_oa=w_oa[l].astype(_BF16), w_ob=w_ob[l].astype(_BF16), w_out=w_out[l].astype(_BF16),
            w_router=jnp.pad(jnp.concatenate([w_re[l], w_rg[l]], axis=1), ((0, 0), (0, pad))),
            b_router=jnp.pad(jnp.concatenate([b_re[l], b_rg[l]]), (0, pad)).reshape(1, _LANES),
            w1=w1[l].astype(_BF16), w3=w3[l].astype(_BF16), w2=w2[l].astype(_BF16),
            n_experts=n_experts, n_groups=n_groups,
        )
        xp, new_p = _layer(xp, mod3, 0, lam, lam_init, None, params)
        past = (cache_a_k[l], cache_a_v[l], cache_b_k[l], cache_b_v[l])
        xs, new_s = _layer(xs, mod3, batch_p, lam, lam_init, past, params)
        rows_p.append(new_p)
        rows_s.append(new_s)
    stack = lambda rows, k: jnp.stack([r[k] for r in rows])
    return (xp, xs, stack(rows_p, 0), stack(rows_p, 1), stack(rows_p, 2), stack(rows_p, 3),
            stack(rows_s, 0), stack(rows_s, 1), stack(rows_s, 2), stack(rows_s, 3))
```

```python
import functools
import math

import jax
import jax.numpy as jnp
from jax import lax
from jax.experimental import pallas as pl
from jax.experimental.pallas import tpu as pltpu

_F32 = jnp.float32
_BF16 = jnp.bfloat16

_LANES = 128
_SUBLANES = 8
_MXU_DIM = 256
_VMEM_LIMIT_BYTES = 48 * 1024 * 1024

_CHUNK = 64
_HEAD_DIM = 64
_ROPE_DIMS = _HEAD_DIM // 4
_ROPE_THETA = 500000.0
_EXP_PER_GROUP = 8
_EPS = 1e-6
_NEG_INF = -1e30
_Q_SCALE = 1.0 / math.sqrt(_HEAD_DIM)
_EXP_ZERO_BELOW = -104.0
_LOG2_E = math.log2(math.e)

_TOKEN_TILE = 512
_OUTPROJ_TILE = 512
_ATTN_TILE = _MXU_DIM
_ATTN_GROUPS = 4
_ONES_ROWS = 16
_MOE_TILE = 1024
_MOE_CHUNK_ROWS = (128, 256, 384)
_MOE_VMEM_LIMIT_BYTES = 56 * 1024 * 1024


def _dot(a, b):
    return jnp.dot(a, b, preferred_element_type=_F32)


def _split(x):
    hi = x.astype(_BF16)
    lo = (x - hi.astype(_F32)).astype(_BF16)
    return hi, lo


def _dot3(a, b):
    a_hi, a_lo = _split(a)
    b_hi, b_lo = _split(b)
    return _dot(a_hi, b_hi) + (_dot(a_hi, b_lo) + _dot(a_lo, b_hi))


def _silu(x):
    return x / (1.0 + jnp.exp(-x))


def _sigmoid(x):
    return 1.0 / (1.0 + jnp.exp(-x))


def _rms(x, axis=-1):
    return x * lax.rsqrt(jnp.mean(x * x, axis=axis, keepdims=True) + _EPS)


def _mod_rows(mod_ref, k, d, seq, n_batch):
    if n_batch == 1:
        return mod_ref[0, :, k * d:(k + 1) * d]
    rows = [jnp.broadcast_to(mod_ref[b, :, k * d:(k + 1) * d], (seq, d)) for b in range(n_batch)]
    return jnp.concatenate(rows, axis=0)


def _lane_tile(x, n):
    assert n % _LANES == 0
    return jnp.concatenate([x] * (n // _LANES), axis=1) if n > _LANES else x


def _compiler_params(semantics, vmem_limit_bytes=_VMEM_LIMIT_BYTES):
    return pltpu.CompilerParams(dimension_semantics=semantics, vmem_limit_bytes=vmem_limit_bytes)


def _ada_kernel(c_ref, w_ref, b_ref, lam_ref, mod_ref, lam_out_ref, *, lam_init):
    mod_ref[...] = _dot3(_silu(c_ref[...]), w_ref[...]) + b_ref[...]
    lv = lam_ref[...]
    s1 = jnp.sum(lv[0:1] * lv[1:2], axis=-1, keepdims=True)
    s2 = jnp.sum(lv[2:3] * lv[3:4], axis=-1, keepdims=True)
    lam = jnp.exp(s1) - jnp.exp(s2) + lam_init
    lam_out_ref[...] = jnp.broadcast_to(lam, lam_out_ref.shape)


def _ada_call(c_all, w_ada, b_ada, lam_vecs, lam_init):
    rows, d = c_all.shape
    cols = w_ada.shape[1]
    tn = d
    return pl.pallas_call(
        functools.partial(_ada_kernel, lam_init=lam_init),
        grid=(cols // tn,),
        in_specs=[
            pl.BlockSpec((rows, d), lambda j: (0, 0)),
            pl.BlockSpec((d, tn), lambda j: (0, j)),
            pl.BlockSpec((1, tn), lambda j: (0, j)),
            pl.BlockSpec(lam_vecs.shape, lambda j: (0, 0)),
        ],
        out_specs=[
            pl.BlockSpec((rows, tn), lambda j: (0, j)),
            pl.BlockSpec((_SUBLANES, _LANES), lambda j: (0, 0)),
        ],
        out_shape=[
            jax.ShapeDtypeStruct((rows, cols), _F32),
            jax.ShapeDtypeStruct((_SUBLANES, _LANES), _F32),
        ],
        compiler_params=_compiler_params(("arbitrary",)),
        name="ada",
    )(c_all, w_ada, b_ada.reshape(1, cols), lam_vecs)


def _adaln(x, gain, scale, shift):
    return (_rms(x) * gain) * (1.0 + scale) + shift


def _proj_kernel(mod_ref, x_ref, n1_ref, w_ref, gq_ref, gk_ref, cos_ref, sin_ref,
                 qa_ref, ka_ref, va_ref, vat_ref, qs_ref, ks_ref, vs_ref, h_s, acc_s, *, seq, n_batch):
    tm, d = x_ref.shape
    shift = _mod_rows(mod_ref, 0, d, seq, n_batch)
    scale = _mod_rows(mod_ref, 1, d, seq, n_batch)
    h_s[...] = _adaln(x_ref[...], n1_ref[...], scale, shift).astype(_BF16)

    def proj(section):
        return _dot(h_s[...], w_ref[:, section * d:(section + 1) * d])

    def transposed(section, acc):
        acc_s[section % 2] = acc
        return acc_s[section % 2].T

    def proj_t(section):
        return transposed(section, proj(section))

    def store_t(out_ref, val):
        if n_batch == 1:
            out_ref[0] = val.astype(out_ref.dtype)
        else:
            pad = out_ref.shape[2] - seq
            for b in range(n_batch):
                out_ref[b, :, :seq] = val[:, b * seq:(b + 1) * seq].astype(out_ref.dtype)
                if pad:
                    out_ref[b, :, seq:] = jnp.zeros((val.shape[0], pad), out_ref.dtype)

    def head_norm_rope(acc, gain_ref, out_ref, out_scale):
        gain = _lane_tile(gain_ref[...], tm)
        cos, sin = cos_ref[...], sin_ref[...]
        half = _ROPE_DIMS // 2
        parts = []
        for h in range(d // _HEAD_DIM):
            y = _rms(acc[h * _HEAD_DIM:(h + 1) * _HEAD_DIM], axis=0) * gain
            x1, x2 = y[:half], y[half:2 * half]
            parts += [x1 * cos - x2 * sin, x2 * cos + x1 * sin, y[2 * half:]]
        out = jnp.concatenate(parts, axis=0)
        store_t(out_ref, out if out_scale == 1.0 else out * out_scale)

    head_norm_rope(proj_t(0), gq_ref, qa_ref, _Q_SCALE * _LOG2_E)
    head_norm_rope(proj_t(1), gk_ref, ka_ref, 1.0)
    va = proj(2)
    va_ref[...] = va
    store_t(vat_ref, transposed(2, va))
    store_t(qs_ref, proj_t(3) * _Q_SCALE)
    store_t(ks_ref, proj_t(4))
    store_t(vs_ref, proj_t(5))


def _rope_tables_t(pos):
    half = _ROPE_DIMS // 2
    inv_freq = jnp.exp(-math.log(_ROPE_THETA) * 2.0 * jnp.arange(half, dtype=_F32) / _ROPE_DIMS)
    ang = inv_freq[:, None] * pos.astype(_F32)[None, :]
    return jnp.cos(ang), jnp.sin(ang)


def _token_tiling(n_tokens, seq, tile=_TOKEN_TILE):
    tm = min(tile, n_tokens)
    if tm >= seq:
        assert tm % seq == 0 and n_tokens % tm == 0
        return tm, tm // seq, 1
    assert seq % tm == 0
    return tm, 1, seq // tm


def _mod_spec(width, n_batch, tiles_per_batch, mod_off):
    assert mod_off % n_batch == 0
    first = mod_off // n_batch
    return pl.BlockSpec((n_batch, 1, width), lambda i, *_: (first + i // tiles_per_batch, 0, 0))


def _proj_call(x2, mod3, mod_off, batch, seq, seq_pad, pos, norm1_g, w_qkv, gq, gk):
    n, d = x2.shape
    tm, n_batch, tiles_per_batch = _token_tiling(n, seq)
    assert seq_pad == seq or n_batch > 1
    cos, sin = _rope_tables_t(pos)
    if n_batch > 1:
        cos, sin = jnp.tile(cos, (1, n_batch)), jnp.tile(sin, (1, n_batch))
    lanes_t = tm if n_batch == 1 else seq_pad
    gain_t = lambda g: jnp.broadcast_to(g.reshape(_HEAD_DIM, 1), (_HEAD_DIM, _LANES))
    row_spec = pl.BlockSpec((tm, d), lambda i: (i, 0))
    t_spec = lambda: pl.BlockSpec((n_batch, d, lanes_t), lambda i: (i // tiles_per_batch, 0, i % tiles_per_batch))
    rope_spec = lambda: pl.BlockSpec((_ROPE_DIMS // 2, tm), lambda i: (0, i % tiles_per_batch))
    const = lambda shape: pl.BlockSpec(shape, lambda i: (0, 0))
    t_out = lambda dt: jax.ShapeDtypeStruct((batch, d, seq_pad), dt)
    return pl.pallas_call(
        functools.partial(_proj_kernel, seq=seq, n_batch=n_batch),
        grid=(n // tm,),
        in_specs=[
            _mod_spec(mod3.shape[2], n_batch, tiles_per_batch, mod_off),
            row_spec,
            const((1, d)),
            pl.BlockSpec(w_qkv.shape, lambda i: (0, 0), pipeline_mode=pl.Buffered(1)),
            const((_HEAD_DIM, _LANES)), const((_HEAD_DIM, _LANES)),
            rope_spec(), rope_spec(),
        ],
        out_specs=[t_spec(), t_spec(), row_spec, t_spec(), t_spec(), t_spec(), t_spec()],
        out_shape=[t_out(_BF16), t_out(_F32), jax.ShapeDtypeStruct((n, d), _F32), t_out(_BF16),
                   t_out(_BF16), t_out(_F32), t_out(_F32)],
        scratch_shapes=[pltpu.VMEM((tm, d), _BF16), pltpu.VMEM((2, tm, d), _F32)],
        compiler_params=_compiler_params(("arbitrary",)),
        name="proj",
    )(mod3, x2, norm1_g.reshape(1, d), w_qkv, gain_t(gq), gain_t(gk), cos, sin)


def _stack_heads_t(qt):
    row = lax.broadcasted_iota(jnp.int32, qt.shape, 0)
    zero = jnp.zeros_like(qt)
    return jnp.concatenate([jnp.where(row < _HEAD_DIM, qt, zero), jnp.where(row >= _HEAD_DIM, qt, zero)], axis=1)


def _diag_offsets(tq):
    assert tq & (tq - 1) == 0
    kpos = lax.broadcasted_iota(jnp.int32, (tq, 2 * tq), 0)
    qpos = lax.broadcasted_iota(jnp.int32, (tq, 2 * tq), 1) & (tq - 1)
    return kpos, qpos


def _load_kv_blocks(groups, tq, n_past, first, kt_ref, vt_ref, past_refs, past_values_token_major, k_s, v_s):
    g = _LANES
    for c in range(groups):
        rows = slice(c * g, (c + 1) * g)
        for jb in range(n_past):
            pk_ref, pv_ref = past_refs
            cols = slice(jb * tq, (jb + 1) * tq)
            k_s[c, first + jb] = pk_ref[0, rows, cols].T.astype(_BF16)
            if past_values_token_major:
                v_s[c, first + jb, :g, :] = pv_ref[cols, rows].T.astype(_BF16)
            else:
                v_s[c, first + jb, :g, :] = pv_ref[0, rows, cols].astype(_BF16)
        for jb in range(k_s.shape[1] - n_past - first):
            cols = slice(jb * tq, (jb + 1) * tq)
            k_s[c, first + n_past + jb] = kt_ref[0, rows, cols].T.astype(_BF16)
            v_s[c, first + n_past + jb, :g, :] = vt_ref[0, rows, cols].astype(_BF16)


def _attn_a_kernel(*refs, tq, n_past, groups, seq, lam_init):
    if n_past:
        lam_ref, subg_ref, qt_ref, kt_ref, vt_ref, pk_ref, pv_ref, o_ref, k_s, v_s, qq_s, s_s, m_s, acc_s = refs
        past_refs = (pk_ref, pv_ref)
    else:
        lam_ref, subg_ref, qt_ref, kt_ref, vt_ref, o_ref, k_s, v_s, qq_s, s_s, m_s, acc_s = refs
        past_refs = None
    qi = pl.program_id(2)

    @pl.when(qi == 0)
    def _():
        _load_kv_blocks(groups, tq, n_past, 0, kt_ref, vt_ref, past_refs, True, k_s, v_s)
        row = lax.broadcasted_iota(jnp.int32, (_ONES_ROWS, tq), 0)
        ones_rows = jnp.where(row == 0, 1.0, 0.0).astype(_BF16)
        for c in range(groups):
            for jb in range(v_s.shape[1]):
                v_s[c, jb, _LANES:, :] = ones_rows

    for c in range(groups):
        qq_s[c] = _stack_heads_t(qt_ref[0, c * _LANES:(c + 1) * _LANES, :])
    m_s[...] = jnp.full(m_s.shape, _NEG_INF, _F32)
    acc_s[...] = jnp.zeros(acc_s.shape, _F32)

    def scores(i):
        return [_dot(k_s[c, i], qq_s[c]) for c in range(groups)]

    def store(slot, blocks):
        for c, st in enumerate(blocks):
            s_s[slot, c] = st

    def update(slot, i, mask):
        for c in range(groups):
            st = s_s[slot, c]
            if mask is not None:
                st = jnp.where(mask, st, _NEG_INF)
            m_prev = m_s[c]
            m_next = jnp.maximum(m_prev, jnp.max(st, axis=0, keepdims=True))
            alpha = jnp.exp2(m_prev - m_next)
            pt = jnp.exp2(st - m_next)
            m_s[c] = m_next
            acc_s[c] = alpha * acc_s[c] + _dot(v_s[c, i], pt.astype(_BF16))

    def step(slot, i, nxt_i, mask=None):
        nxt = scores(nxt_i)
        update(slot, i, mask)
        store(1 - slot, nxt)

    last = n_past + qi
    kpos, qpos = _diag_offsets(tq)
    chunk_bits = _CHUNK.bit_length() - 1
    mask = (kpos >> chunk_bits) <= (qpos >> chunk_bits)
    if seq < tq:
        mask = mask & (kpos < seq)
    store(0, scores(last))
    step(0, last, 0, mask)

    def pair(j, carry):
        step(1, 2 * j, 2 * j + 1)
        step(0, 2 * j + 1, jnp.minimum(2 * j + 2, last))
        return carry

    lax.fori_loop(0, last // 2, pair, 0)

    @pl.when((last & 1) == 1)
    def _():
        update(1, last - 1, None)

    gain = _lane_tile(subg_ref[...], tq)
    for c in range(groups):
        acc = acc_s[c]
        out = acc[:_LANES] / acc[_LANES:_LANES + 1]
        o = out[:, :tq] - lam_ref[0:1, 0:1] * out[:, tq:]
        y = (_rms(o, axis=0) * gain) * (1.0 - lam_init)
        o_ref[:, c * _LANES:(c + 1) * _LANES] = y.T[:o_ref.shape[0]].astype(o_ref.dtype)


def _attn_b_kernel(*refs, tq, n_past, groups, seq):
    if n_past:
        qt_ref, kt_ref, vt_ref, pk_ref, pv_ref, o_ref, k_s, v_s, qq_s, s_s, u_s, c_s, acc_s = refs
        past_refs = (pk_ref, pv_ref)
    else:
        qt_ref, kt_ref, vt_ref, o_ref, k_s, v_s, qq_s, s_s, u_s, c_s, acc_s = refs
        past_refs = None
    qi = pl.program_id(2)

    @pl.when(qi == 0)
    def _():
        for c in range(groups):
            k_s[c, 0] = jnp.zeros(k_s.shape[2:], _BF16)
            v_s[c, 0] = jnp.zeros(v_s.shape[2:], _BF16)
        _load_kv_blocks(groups, tq, n_past, 1, kt_ref, vt_ref, past_refs, False, k_s, v_s)

    for c in range(groups):
        qq_s[c] = _stack_heads_t(qt_ref[0, c * _LANES:(c + 1) * _LANES, :])
    s_idx = lax.broadcasted_iota(jnp.int32, (tq, 2 * tq), 0)
    j_idx = lax.broadcasted_iota(jnp.int32, (tq, 2 * tq), 1) & (tq - 1)
    u_s[...] = jnp.where(j_idx >= s_idx, -1.0, 0.0).astype(_BF16)
    c_s[...] = jnp.zeros(c_s.shape, _F32)
    acc_s[...] = jnp.zeros(acc_s.shape, _F32)

    def logits(i):
        return [_dot(k_s[c, i], qq_s[c]) for c in range(groups)]

    def update(slot, i, mask):
        rests = []
        for c in range(groups):
            zt = s_s[slot, c]
            neg_log_rest = jnp.maximum(zt, 0.0) + jnp.log(1.0 + jnp.exp2(jnp.abs(zt) * (-_LOG2_E)))
            if mask is not None:
                neg_log_rest = jnp.where(mask, neg_log_rest, 0.0)
            hi, lo = _split(neg_log_rest)
            rests.append(_dot(u_s[...], jnp.concatenate([hi, lo], axis=0)))
        for c, rest_from_here in enumerate(rests):
            wt = jnp.exp(s_s[slot, c] + rest_from_here + c_s[c])
            if mask is not None:
                wt = jnp.where(mask, wt, 0.0)
            acc_s[c] = acc_s[c] + _dot(v_s[c, i], wt.astype(_BF16))
            c_s[c] = c_s[c] + rest_from_here[0:1, :]

    def carry_max():
        return functools.reduce(jnp.maximum, [jnp.max(c_s[c]) for c in range(groups)])

    top = n_past + qi + 1

    def store(slot, blocks):
        for c, zt in enumerate(blocks):
            s_s[slot, c] = zt

    def step(slot, i, nxt_i, mask=None):
        nxt = logits(nxt_i)
        update(slot, i, mask)
        store(1 - slot, nxt)

    kpos, qpos = _diag_offsets(tq)
    store(0, logits(top))
    step(0, top, top - 1, kpos < qpos)
    step(1, top - 1, jnp.maximum(top - 2, 0))

    def pair(state):
        i = state[0]
        step(0, i, i - 1)
        step(1, i - 1, jnp.maximum(i - 2, 0))
        return i - 2, carry_max()

    lax.while_loop(lambda s: (s[0] >= 1) & (s[1] > _EXP_ZERO_BELOW), pair, (top - 2, carry_max()))

    row = lax.broadcasted_iota(jnp.int32, (_LANES, tq), 0)
    for c in range(groups):
        acc = acc_s[c]
        out = jnp.where(row < _HEAD_DIM, acc[:, :tq], acc[:, tq:])
        o_ref[:, c * _LANES:(c + 1) * _LANES] = out.T[:o_ref.shape[0]].astype(o_ref.dtype)


def _attn_call(kind, qt, kt, vt, past, seq, extra_inputs, **kernel_kwargs):
    batch, d, seq_pad = qt.shape
    tq = min(_ATTN_TILE, seq_pad)
    assert seq_pad % tq == 0 and tq % _LANES == 0
    nq = seq_pad // tq
    groups = _ATTN_GROUPS
    gw = groups * _LANES
    assert d % gw == 0
    const = lambda a: pl.BlockSpec(a.shape, lambda b, g, i: (0,) * a.ndim)
    q_spec = pl.BlockSpec((1, gw, tq), lambda b, g, i: (b, g, i))
    kv_spec = pl.BlockSpec((1, gw, seq_pad), lambda b, g, i: (b, g, 0))
    inputs = list(extra_inputs) + [qt, kt, vt]
    in_specs = [const(a) for a in extra_inputs] + [q_spec, kv_spec, kv_spec]
    n_past = 0
    if past is not None:
        pkt, pv = past
        past_len = pkt.shape[2]
        assert past_len % tq == 0 and past_len % _CHUNK == 0
        n_past = past_len // tq
        inputs += [pkt, pv]
        in_specs.append(pl.BlockSpec((1, gw, past_len), lambda b, g, i: (b, g, 0)))
        if kind == "a":
            in_specs.append(pl.BlockSpec((past_len, gw), lambda b, g, i: (b, g)))
        else:
            in_specs.append(pl.BlockSpec((1, gw, past_len), lambda b, g, i: (b, g, 0)))
    n_blocks = n_past + nq + (1 if kind == "b" else 0)
    v_rows = _LANES + (_ONES_ROWS if kind == "a" else 0)
    scratch = [
        pltpu.VMEM((groups, n_blocks, tq, _LANES), _BF16),
        pltpu.VMEM((groups, n_blocks, v_rows, tq), _BF16),
        pltpu.VMEM((groups, _LANES, 2 * tq), _BF16),
        pltpu.VMEM((2, groups, tq, 2 * tq), _F32),
    ]
    row_state = pltpu.VMEM((groups, 1, 2 * tq), _F32)
    acc_state = pltpu.VMEM((groups, v_rows, 2 * tq), _F32)
    if kind == "a":
        body = functools.partial(_attn_a_kernel, tq=tq, n_past=n_past, groups=groups, seq=seq, **kernel_kwargs)
        scratch += [row_state, acc_state]
    else:
        body = functools.partial(_attn_b_kernel, tq=tq, n_past=n_past, groups=groups, seq=seq)
        scratch += [pltpu.VMEM((tq, 2 * tq), _BF16), row_state, acc_state]
    return pl.pallas_call(
        body,
        grid=(batch, d // gw, nq),
        in_specs=in_specs,
        out_specs=pl.BlockSpec((min(tq, seq), gw), lambda b, g, i: (b * nq + i, g)),
        out_shape=jax.ShapeDtypeStruct((batch * seq, d), _BF16),
        scratch_shapes=scratch,
        compiler_params=_compiler_params(("arbitrary", "arbitrary", "arbitrary")),
        name="attn_" + kind,
    )(*inputs)


def _router_gate(logits, n_experts, n_groups):
    lane_i = lax.broadcasted_iota(jnp.int32, logits.shape, 1)
    lane = lane_i.astype(_F32)
    group_of_lane = (lane_i >> (_EXP_PER_GROUP.bit_length() - 1)).astype(_F32)
    big = float(4 * _LANES)
    row_max = lambda v: jnp.max(v, axis=1, keepdims=True)
    row_min = lambda v: jnp.min(v, axis=1, keepdims=True)
    row_sum = lambda v: jnp.sum(v, axis=1, keepdims=True)

    is_group = (lane_i >= n_experts) & (lane_i < n_experts + n_groups)
    lg = jnp.where(is_group, logits, _NEG_INF)
    eg = jnp.where(is_group, jnp.exp(lg - row_max(lg)), 0.0)
    pg = eg / row_sum(eg)
    pg_sel = row_max(pg)
    g_sel = row_min(jnp.where(is_group & (pg == pg_sel), lane - n_experts, big))

    in_group = (lane_i < n_experts) & (group_of_lane == g_sel)
    le = jnp.where(in_group, logits, _NEG_INF)
    ee = jnp.where(in_group, jnp.exp(le - row_max(le)), 0.0)
    pe = ee / row_sum(ee)
    p1 = row_max(pe)
    i1 = row_min(jnp.where(in_group & (pe == p1), lane, big))
    rest = in_group & (lane != i1)
    p2 = row_max(jnp.where(rest, pe, -1.0))
    i2 = row_min(jnp.where(rest & (pe == p2), lane, big))
    total = p1 + p2
    gate = jnp.where(lane == i1, p1 / total * pg_sel, 0.0) + jnp.where(lane == i2, p2 / total * pg_sel, 0.0)
    return jnp.where(lane_i == n_experts, g_sel, gate)


def _outproj_kernel(mod_ref, x_ref, oa_ref, ob_ref, n1_ref, n2_ref, wg_ref, woa_ref, wob_ref, wout_ref,
                    wr_hi_ref, wr_lo_ref, br_ref, x1_ref, h2_ref, gate_ref, *, seq, n_batch, n_experts, n_groups):
    tm, d = x_ref.shape
    halves = [slice(0, tm // 2), slice(tm // 2, tm)] if tm >= 2 * _MXU_DIM else [slice(0, tm)]

    def rows(k, sl):
        r = _mod_rows(mod_ref, k, d, seq, n_batch)
        return r if r.shape[0] == 1 else r[sl]

    h2_halves = []
    for sl in halves:
        x = x_ref[sl, :]
        h = _adaln(x, n1_ref[...], rows(1, sl), rows(0, sl)).astype(_BF16)
        gate_a = _sigmoid(_dot(h, wg_ref[:, :d]))
        gate_b = _sigmoid(_dot(h, wg_ref[:, d:]))
        mix = gate_a * _dot(oa_ref[sl, :], woa_ref[...]) + gate_b * _dot(ob_ref[sl, :], wob_ref[...])
        x1 = x + rows(2, sl) * _dot(mix.astype(_BF16), wout_ref[...])
        x1_ref[sl, :] = x1
        h2 = _adaln(x1, n2_ref[...], rows(4, sl), rows(3, sl))
        h2_ref[sl, :] = h2.astype(_BF16)
        h2_halves.append(h2)
    for sl, h2 in zip(halves, h2_halves):
        h2_hi, h2_lo = _split(h2)
        logits = (_dot(h2_hi, wr_hi_ref[...]) + (_dot(h2_hi, wr_lo_ref[...]) + _dot(h2_lo, wr_hi_ref[...]))
                  + br_ref[...])
        gate_ref[sl, :] = _router_gate(logits, n_experts, n_groups)


def _outproj_call(x2, oa, ob, mod3, mod_off, seq, norm1_g, norm2_g, w_gate_bf, w_oa_bf, w_ob_bf, w_out_bf,
                  w_router, b_router, n_experts, n_groups):
    n, d = x2.shape
    tm, n_batch, tiles_per_batch = _token_tiling(n, seq, _OUTPROJ_TILE)
    wr_hi, wr_lo = _split(w_router)
    row_spec = lambda: pl.BlockSpec((tm, d), lambda i: (i, 0))
    full = lambda a: pl.BlockSpec(a.shape, lambda i: (0,) * a.ndim, pipeline_mode=pl.Buffered(1))
    return pl.pallas_call(
        functools.partial(_outproj_kernel, seq=seq, n_batch=n_batch, n_experts=n_experts, n_groups=n_groups),
        grid=(n // tm,),
        in_specs=[
            _mod_spec(mod3.shape[2], n_batch, tiles_per_batch, mod_off),
            row_spec(), row_spec(), row_spec(),
            pl.BlockSpec((1, d), lambda i: (0, 0)), pl.BlockSpec((1, d), lambda i: (0, 0)),
            full(w_gate_bf), full(w_oa_bf), full(w_ob_bf), full(w_out_bf), full(wr_hi), full(wr_lo),
            pl.BlockSpec((1, _LANES), lambda i: (0, 0)),
        ],
        out_specs=[row_spec(), row_spec(), pl.BlockSpec((tm, _LANES), lambda i: (i, 0))],
        out_shape=[
            jax.ShapeDtypeStruct((n, d), _F32),
            jax.ShapeDtypeStruct((n, d), _BF16),
            jax.ShapeDtypeStruct((n, _LANES), _F32),
        ],
        compiler_params=_compiler_params(("arbitrary",)),
        name="outproj",
    )(mod3, x2, oa, ob, norm1_g.reshape(1, d), norm2_g.reshape(1, d), w_gate_bf, w_oa_bf, w_ob_bf, w_out_bf,
      wr_hi, wr_lo, b_router)


def _split3(x):
    hi = x.astype(_BF16)
    r = x - hi.astype(_F32)
    mid = r.astype(_BF16)
    return hi, mid, (r - mid.astype(_F32)).astype(_BF16)


def _moe_kernel(mod_ref, x1_ref, h2_ref, gate_ref, w1_ref, w3_ref, w2_ref, y_ref,
                gpack_s, rank_col_s, rank_row_s, count_s, *, seq, n_batch, n_experts, n_groups):
    g = pl.program_id(1)
    tm, d = x1_ref.shape
    lane = lax.broadcasted_iota(jnp.int32, (tm, _LANES), 1)
    assert 3 * n_experts <= _LANES and n_experts & (n_experts - 1) == 0

    @pl.when(g == 0)
    def _():
        y_ref[...] = jnp.zeros(y_ref.shape, _F32)
        gate = gate_ref[...]
        hi, mid, lo = [p.astype(_F32) for p in _split3(jnp.where(lane < n_experts, gate, 0.0))]
        gpack_s[...] = (hi + pltpu.roll(mid, n_experts, 1) + pltpu.roll(lo, 2 * n_experts, 1)).astype(_BF16)
        group_id = gate[:, n_experts:n_experts + 1]
        member = jnp.where((lane < n_groups) & (lane.astype(_F32) == group_id), 1.0, 0.0)
        t_idx = lax.broadcasted_iota(jnp.int32, (tm, tm), 0)
        u_idx = lax.broadcasted_iota(jnp.int32, (tm, tm), 1)
        earlier = jnp.where(u_idx < t_idx, 1.0, 0.0).astype(_BF16)
        rank_col = jnp.where(member > 0.0, _dot(earlier, member.astype(_BF16)), -1.0)
        rank_col_s[...] = rank_col
        rank_row = rank_col.T
        for k in range(n_groups):
            rank_row_s[k] = jnp.broadcast_to(rank_row[k:k + 1, :], (_SUBLANES, tm))
        count_s[...] = jnp.sum(member, axis=0, keepdims=True)

    lane_row = lax.broadcasted_iota(jnp.int32, (1, _LANES), 1)
    count = jnp.sum(jnp.where(lane_row == g, count_s[...], 0.0)).astype(jnp.int32)
    rank_c = jnp.sum(jnp.where(lane == g, rank_col_s[...], 0.0), axis=1, keepdims=True)
    experts_per_group = w1_ref.shape[0]

    def run_chunk(first_rank, rows):
        base = first_rank.astype(_F32)
        rank_r = jnp.concatenate([rank_row_s[g]] * (rows // _SUBLANES), axis=0)
        slot_r = lax.broadcasted_iota(jnp.int32, (rows, tm), 0).astype(_F32)
        slot_c = lax.broadcasted_iota(jnp.int32, (tm, rows), 1).astype(_F32)
        lane_c = lax.broadcasted_iota(jnp.int32, (rows, _LANES), 1)
        pick = jnp.where(rank_r == slot_r + base, 1.0, 0.0).astype(_BF16)
        put = jnp.where(rank_c == slot_c + base, 1.0, 0.0).astype(_BF16)
        xc = _dot(pick, h2_ref[...]).astype(_BF16)
        gc = _dot(pick, gpack_s[...])
        acc = jnp.zeros((rows, d), _F32)
        for e in range(experts_per_group):
            mine = (lane_c & (n_experts - 1)) == g * experts_per_group + e
            ge = jnp.sum(jnp.where(mine, gc, 0.0), axis=1, keepdims=True)
            hid = _silu(_dot(xc, w1_ref[e])) * _dot(xc, w3_ref[e])
            acc = acc + _dot((hid * ge).astype(_BF16), w2_ref[e])
        y_ref[...] += _dot(put, acc.astype(_BF16))

    sizes = [s for s in _MOE_CHUNK_ROWS if s <= tm]
    biggest = sizes[-1]
    n_full = count // biggest

    def full_chunk(k, carry):
        run_chunk(k * biggest, biggest)
        return carry

    lax.fori_loop(0, n_full, full_chunk, 0)
    rest = count - n_full * biggest
    for smaller, rows in zip([0] + sizes[:-1], sizes):

        @pl.when((rest > smaller) & (rest <= rows))
        def _(rows=rows):
            run_chunk(n_full * biggest, rows)

    @pl.when(g == pl.num_programs(1) - 1)
    def _():
        y_ref[...] = x1_ref[...] + _mod_rows(mod_ref, 5, d, seq, n_batch) * y_ref[...]


def _moe_call(x1, h2, gate, mod3, mod_off, seq, w1_bf, w3_bf, w2_bf, n_groups):
    n, d = x1.shape
    tm, n_batch, tiles_per_batch = _token_tiling(n, seq, _MOE_TILE)
    n_experts, _, d_expert = w1_bf.shape
    eg = n_experts // n_groups
    assert eg == _EXP_PER_GROUP
    row_spec = lambda w, **kw: pl.BlockSpec((tm, w), lambda i, g: (i, 0), **kw)
    return pl.pallas_call(
        functools.partial(_moe_kernel, seq=seq, n_batch=n_batch, n_experts=n_experts, n_groups=n_groups),
        grid=(n // tm, n_groups),
        in_specs=[
            _mod_spec(mod3.shape[2], n_batch, tiles_per_batch, mod_off),
            row_spec(d, pipeline_mode=pl.Buffered(1)), row_spec(d), row_spec(_LANES),
            pl.BlockSpec((eg, d, d_expert), lambda i, g: (g, 0, 0)),
            pl.BlockSpec((eg, d, d_expert), lambda i, g: (g, 0, 0)),
            pl.BlockSpec((eg, d_expert, d), lambda i, g: (g, 0, 0)),
        ],
        out_specs=row_spec(d),
        out_shape=jax.ShapeDtypeStruct((n, d), _F32),
        scratch_shapes=[
            pltpu.VMEM((tm, _LANES), _BF16),
            pltpu.VMEM((tm, _LANES), _F32),
            pltpu.VMEM((n_groups, _SUBLANES, tm), _F32),
            pltpu.VMEM((1, _LANES), _F32),
        ],
        compiler_params=_compiler_params(("arbitrary", "arbitrary"), _MOE_VMEM_LIMIT_BYTES),
        name="moe",
    )(mod3, x1, h2, gate, w1_bf, w3_bf, w2_bf)


def _feature_major(c):
    b, t, h, dh = c.shape
    return jnp.transpose(c, (0, 2, 3, 1)).reshape(b, h * dh, t)


def _token_major(ct, head_dim):
    b, d, t = ct.shape
    return jnp.transpose(ct.reshape(b, d // head_dim, head_dim, t), (0, 3, 1, 2))


def _layer(x, mod3, mod_off, lam, lam_init, past, p):
    batch, seq, d = x.shape
    x2 = x.reshape(batch * seq, d)
    past_len = 0 if past is None else past[0].shape[1]
    pos = past_len + jnp.arange(seq)
    seq_pad = -(-seq // _LANES) * _LANES
    qa_t, ka_t, va, va_t, qs_t, ks_t, vs_t = _proj_call(
        x2, mod3, mod_off, batch, seq, seq_pad, pos, p["norm1_g"], p["w_qkv"], p["gq"], p["gk"])
    past_a = past_b = None
    if past is not None:
        past_a = (_feature_major(past[0]), past[1].reshape(batch * past_len, d))
        past_b = (_feature_major(past[2]), _feature_major(past[3]))
    oa = _attn_call("a", qa_t, ka_t, va_t, past_a, seq, [lam, p["subln_g"]], lam_init=lam_init)
    ob = _attn_call("b", qs_t, ks_t, vs_t, past_b, seq, [])
    if seq_pad != seq:
        ka_t, ks_t, vs_t = ka_t[:, :, :seq], ks_t[:, :, :seq], vs_t[:, :, :seq]
    x1, h2, gate = _outproj_call(x2, oa, ob, mod3, mod_off, seq, p["norm1_g"], p["norm2_g"], p["w_gate"],
                                 p["w_oa"], p["w_ob"], p["w_out"], p["w_router"], p["b_router"],
                                 p["n_experts"], p["n_groups"])
    y = _moe_call(x1, h2, gate, mod3, mod_off, seq, p["w1"], p["w3"], p["w2"], p["n_groups"])
    new = (_token_major(ka_t, _HEAD_DIM), va.reshape(batch, seq, -1, 2 * _HEAD_DIM),
           _token_major(ks_t, _HEAD_DIM), _token_major(vs_t, _HEAD_DIM))
    return y.reshape(batch, seq, d), new


def kernel(x_prompt, x_sample, cache_a_k, cache_a_v, cache_b_k, cache_b_v, c_prompt, c_sample, norm1_g, norm2_g, w_ada, b_ada, w_in, a_qnorm_g, a_knorm_g, a_lam_q1, a_lam_k1, a_lam_q2, a_lam_k2, a_subln_g, w_oa, w_ob, w_out, w_rg, b_rg, w_re, b_re, w1, w3, w2):
    depth, d = norm1_g.shape
    n_groups, n_experts = w_rg.shape[2], w_re.shape[2]
    assert w_in.shape[2] == 8 * d and n_experts + n_groups <= _LANES
    batch_p = x_prompt.shape[0]
    xp, xs = x_prompt, x_sample
    c_all = jnp.concatenate([c_prompt, c_sample], axis=0)
    rows_p, rows_s = [], []
    for l in range(depth):
        lam_init = 0.8 - 0.6 * math.exp(-0.3 * l)
        lam_vecs = jnp.stack([a_lam_q1[l], a_lam_k1[l], a_lam_q2[l], a_lam_k2[l]])
        mod, lam = _ada_call(c_all, w_ada[l], b_ada[l], lam_vecs, lam_init)
        mod3 = mod.reshape(mod.shape[0], 1, mod.shape[1])
        pad = _LANES - n_experts - n_groups
        params = dict(
            norm1_g=norm1_g[l], norm2_g=norm2_g[l],
            w_qkv=w_in[l][:, :6 * d].astype(_BF16), w_gate=w_in[l][:, 6 * d:].astype(_BF16),
            gq=a_qnorm_g[l], gk=a_knorm_g[l],
            subln_g=jnp.broadcast_to(a_subln_g[l].reshape(-1, 1), (a_subln_g.shape[1], _LANES)),
            w_oa=w_oa[l].astype(_BF16), w_ob=w_ob[l].astype(_BF16), w_out=w_out[l].astype(_BF16),
            w_router=jnp.pad(jnp.concatenate([w_re[l], w_rg[l]], axis=1), ((0, 0), (0, pad))),
            b_router=jnp.pad(jnp.concatenate([b_re[l], b_rg[l]]), (0, pad)).reshape(1, _LANES),
            w1=w1[l].astype(_BF16), w3=w3[l].astype(_BF16), w2=w2[l].astype(_BF16),
            n_experts=n_experts, n_groups=n_groups,
        )
        xp, new_p = _layer(xp, mod3, 0, lam, lam_init, None, params)
        past = (cache_a_k[l], cache_a_v[l], cache_b_k[l], cache_b_v[l])
        xs, new_s = _layer(xs, mod3, batch_p, lam, lam_init, past, params)
        rows_p.append(new_p)
        rows_s.append(new_s)
    stack = lambda rows, k: jnp.stack([r[k] for r in rows])
    return (xp, xs, stack(rows_p, 0), stack(rows_p, 1), stack(rows_p, 2), stack(rows_p, 3),
            stack(rows_s, 0), stack(rows_s, 1), stack(rows_s, 2), stack(rows_s, 3))
```

```python
import functools
import math

import jax
import jax.numpy as jnp
from jax import lax
from jax.experimental import pallas as pl
from jax.experimental.pallas import tpu as pltpu

_F32 = jnp.float32
_BF16 = jnp.bfloat16

_LANES = 128
_SUBLANES = 8
_MXU_DIM = 256
_VMEM_LIMIT_BYTES = 48 * 1024 * 1024

_CHUNK = 64
_HEAD_DIM = 64
_ROPE_DIMS = _HEAD_DIM // 4
_ROPE_THETA = 500000.0
_EXP_PER_GROUP = 8
_EPS = 1e-6
_NEG_INF = -1e30
_Q_SCALE = 1.0 / math.sqrt(_HEAD_DIM)
_EXP_ZERO_BELOW = -104.0
_LOG2_E = math.log2(math.e)

_TOKEN_TILE = 512
_OUTPROJ_TILE = 512
_ATTN_TILE = _MXU_DIM
_ATTN_GROUPS = 4
_ONES_ROWS = 16
_MOE_TILE = 1024
_MOE_CHUNK_ROWS = (128, 192, 256, 320, 384)
_MOE_VMEM_LIMIT_BYTES = 56 * 1024 * 1024


def _dot(a, b):
    return jnp.dot(a, b, preferred_element_type=_F32)


def _split(x):
    hi = x.astype(_BF16)
    lo = (x - hi.astype(_F32)).astype(_BF16)
    return hi, lo


def _dot3(a, b):
    a_hi, a_lo = _split(a)
    b_hi, b_lo = _split(b)
    return _dot(a_hi, b_hi) + (_dot(a_hi, b_lo) + _dot(a_lo, b_hi))


def _silu(x):
    return x / (1.0 + jnp.exp(-x))


def _sigmoid(x):
    return 1.0 / (1.0 + jnp.exp(-x))


def _rms(x, axis=-1):
    return x * lax.rsqrt(jnp.mean(x * x, axis=axis, keepdims=True) + _EPS)


def _mod_rows(mod_ref, k, d, seq, n_batch):
    if n_batch == 1:
        return mod_ref[0, :, k * d:(k + 1) * d]
    rows = [jnp.broadcast_to(mod_ref[b, :, k * d:(k + 1) * d], (seq, d)) for b in range(n_batch)]
    return jnp.concatenate(rows, axis=0)


def _lane_tile(x, n):
    assert n % _LANES == 0
    return jnp.concatenate([x] * (n // _LANES), axis=1) if n > _LANES else x


def _compiler_params(semantics, vmem_limit_bytes=_VMEM_LIMIT_BYTES):
    return pltpu.CompilerParams(dimension_semantics=semantics, vmem_limit_bytes=vmem_limit_bytes)


def _ada_kernel(c_ref, w_ref, b_ref, lam_ref, mod_ref, lam_out_ref, *, lam_init):
    mod_ref[...] = _dot3(_silu(c_ref[...]), w_ref[...]) + b_ref[...]
    lv = lam_ref[...]
    s1 = jnp.sum(lv[0:1] * lv[1:2], axis=-1, keepdims=True)
    s2 = jnp.sum(lv[2:3] * lv[3:4], axis=-1, keepdims=True)
    lam = jnp.exp(s1) - jnp.exp(s2) + lam_init
    lam_out_ref[...] = jnp.broadcast_to(lam, lam_out_ref.shape)


def _ada_call(c_all, w_ada, b_ada, lam_vecs, lam_init):
    rows, d = c_all.shape
    cols = w_ada.shape[1]
    tn = d
    return pl.pallas_call(
        functools.partial(_ada_kernel, lam_init=lam_init),
        grid=(cols // tn,),
        in_specs=[
            pl.BlockSpec((rows, d), lambda j: (0, 0)),
            pl.BlockSpec((d, tn), lambda j: (0, j)),
            pl.BlockSpec((1, tn), lambda j: (0, j)),
            pl.BlockSpec(lam_vecs.shape, lambda j: (0, 0)),
        ],
        out_specs=[
            pl.BlockSpec((rows, tn), lambda j: (0, j)),
            pl.BlockSpec((_SUBLANES, _LANES), lambda j: (0, 0)),
        ],
        out_shape=[
            jax.ShapeDtypeStruct((rows, cols), _F32),
            jax.ShapeDtypeStruct((_SUBLANES, _LANES), _F32),
        ],
        compiler_params=_compiler_params(("arbitrary",)),
        name="ada",
    )(c_all, w_ada, b_ada.reshape(1, cols), lam_vecs)


def _adaln(x, gain, scale, shift):
    return (_rms(x) * gain) * (1.0 + scale) + shift


def _proj_kernel(mod_ref, x_ref, n1_ref, w_ref, gq_ref, gk_ref, cos_ref, sin_ref,
                 qa_ref, ka_ref, va_ref, vat_ref, qs_ref, ks_ref, vs_ref, h_s, acc_s, *, seq, n_batch):
    tm, d = x_ref.shape
    shift = _mod_rows(mod_ref, 0, d, seq, n_batch)
    scale = _mod_rows(mod_ref, 1, d, seq, n_batch)
    h_s[...] = _adaln(x_ref[...], n1_ref[...], scale, shift).astype(_BF16)

    def proj(section):
        return _dot(h_s[...], w_ref[:, section * d:(section + 1) * d])

    def transposed(section, acc):
        acc_s[section % 2] = acc
        return acc_s[section % 2].T

    def proj_t(section):
        return transposed(section, proj(section))

    def store_t(out_ref, val):
        if tm < seq:
            out_ref[0] = val.astype(out_ref.dtype)
        else:
            pad = out_ref.shape[2] - seq
            for b in range(n_batch):
                out_ref[b, :, :seq] = val[:, b * seq:(b + 1) * seq].astype(out_ref.dtype)
                if pad:
                    out_ref[b, :, seq:] = jnp.zeros((val.shape[0], pad), out_ref.dtype)

    def head_norm_rope(acc, gain_ref, out_ref, out_scale):
        gain = _lane_tile(gain_ref[...], tm)
        cos, sin = cos_ref[...], sin_ref[...]
        half = _ROPE_DIMS // 2
        parts = []
        for h in range(d // _HEAD_DIM):
            y = _rms(acc[h * _HEAD_DIM:(h + 1) * _HEAD_DIM], axis=0) * gain
            x1, x2 = y[:half], y[half:2 * half]
            parts += [x1 * cos - x2 * sin, x2 * cos + x1 * sin, y[2 * half:]]
        out = jnp.concatenate(parts, axis=0)
        store_t(out_ref, out if out_scale == 1.0 else out * out_scale)

    head_norm_rope(proj_t(0), gq_ref, qa_ref, _Q_SCALE * _LOG2_E)
    head_norm_rope(proj_t(1), gk_ref, ka_ref, 1.0)
    va = proj(2)
    va_ref[...] = va
    store_t(vat_ref, transposed(2, va))
    store_t(qs_ref, proj_t(3) * _Q_SCALE)
    store_t(ks_ref, proj_t(4))
    store_t(vs_ref, proj_t(5))


def _rope_tables_t(pos):
    half = _ROPE_DIMS // 2
    inv_freq = jnp.exp(-math.log(_ROPE_THETA) * 2.0 * jnp.arange(half, dtype=_F32) / _ROPE_DIMS)
    ang = inv_freq[:, None] * pos.astype(_F32)[None, :]
    return jnp.cos(ang), jnp.sin(ang)


def _token_tiling(n_tokens, seq, tile=_TOKEN_TILE):
    tm = min(tile, n_tokens)
    if tm >= seq:
        assert tm % seq == 0 and n_tokens % tm == 0
        return tm, tm // seq, 1
    assert seq % tm == 0
    return tm, 1, seq // tm


def _mod_spec(width, n_batch, tiles_per_batch, mod_off):
    assert mod_off % n_batch == 0
    first = mod_off // n_batch
    return pl.BlockSpec((n_batch, 1, width), lambda i, *_: (first + i // tiles_per_batch, 0, 0))


def _proj_call(x2, mod3, mod_off, batch, seq, seq_pad, pos, norm1_g, w_in_bf, gq, gk):
    n, d = x2.shape
    tm, n_batch, tiles_per_batch = _token_tiling(n, seq, min(_TOKEN_TILE, max(n // 2, seq)))
    assert seq_pad == seq or tm >= seq
    cos, sin = _rope_tables_t(pos)
    if n_batch > 1:
        cos, sin = jnp.tile(cos, (1, n_batch)), jnp.tile(sin, (1, n_batch))
    lanes_t = tm if tm < seq else seq_pad
    gain_t = lambda g: jnp.broadcast_to(g.reshape(_HEAD_DIM, 1), (_HEAD_DIM, _LANES))
    row_spec = pl.BlockSpec((tm, d), lambda i: (i, 0))
    t_spec = lambda: pl.BlockSpec((n_batch, d, lanes_t), lambda i: (i // tiles_per_batch, 0, i % tiles_per_batch))
    rope_spec = lambda: pl.BlockSpec((_ROPE_DIMS // 2, tm), lambda i: (0, i % tiles_per_batch))
    const = lambda shape: pl.BlockSpec(shape, lambda i: (0, 0))
    t_out = lambda dt: jax.ShapeDtypeStruct((batch, d, seq_pad), dt)
    return pl.pallas_call(
        functools.partial(_proj_kernel, seq=seq, n_batch=n_batch),
        grid=(n // tm,),
        in_specs=[
            _mod_spec(mod3.shape[2], n_batch, tiles_per_batch, mod_off),
            row_spec,
            const((1, d)),
            pl.BlockSpec((d, 6 * d), lambda i: (0, 0), pipeline_mode=pl.Buffered(1)),
            const((_HEAD_DIM, _LANES)), const((_HEAD_DIM, _LANES)),
            rope_spec(), rope_spec(),
        ],
        out_specs=[t_spec(), t_spec(), row_spec, t_spec(), t_spec(), t_spec(), t_spec()],
        out_shape=[t_out(_BF16), t_out(_F32), jax.ShapeDtypeStruct((n, d), _F32), t_out(_BF16),
                   t_out(_BF16), t_out(_F32), t_out(_F32)],
        scratch_shapes=[pltpu.VMEM((tm, d), _BF16), pltpu.VMEM((2, tm, d), _F32)],
        compiler_params=_compiler_params(("arbitrary",)),
        name="proj",
    )(mod3, x2, norm1_g.reshape(1, d), w_in_bf, gain_t(gq), gain_t(gk), cos, sin)


def _stack_heads_t(qt):
    row = lax.broadcasted_iota(jnp.int32, qt.shape, 0)
    zero = jnp.zeros_like(qt)
    return jnp.concatenate([jnp.where(row < _HEAD_DIM, qt, zero), jnp.where(row >= _HEAD_DIM, qt, zero)], axis=1)


def _diag_offsets(tq):
    assert tq & (tq - 1) == 0
    kpos = lax.broadcasted_iota(jnp.int32, (tq, 2 * tq), 0)
    qpos = lax.broadcasted_iota(jnp.int32, (tq, 2 * tq), 1) & (tq - 1)
    return kpos, qpos


def _load_kv_blocks(groups, tq, n_past, first, kt_ref, vt_ref, past_refs, past_values_token_major, k_s, v_s):
    g = _LANES
    for c in range(groups):
        rows = slice(c * g, (c + 1) * g)
        for jb in range(n_past):
            pk_ref, pv_ref = past_refs
            cols = slice(jb * tq, (jb + 1) * tq)
            k_s[c, first + jb] = pk_ref[0, rows, cols].T.astype(_BF16)
            if past_values_token_major:
                v_s[c, first + jb, :g, :] = pv_ref[cols, rows].T.astype(_BF16)
            else:
                v_s[c, first + jb, :g, :] = pv_ref[0, rows, cols].astype(_BF16)
        for jb in range(k_s.shape[1] - n_past - first):
            cols = slice(jb * tq, (jb + 1) * tq)
            k_s[c, first + n_past + jb] = kt_ref[0, rows, cols].T.astype(_BF16)
            v_s[c, first + n_past + jb, :g, :] = vt_ref[0, rows, cols].astype(_BF16)


def _attn_a_kernel(*refs, tq, n_past, groups, seq, lam_init):
    if n_past:
        lam_ref, subg_ref, qt_ref, kt_ref, vt_ref, pk_ref, pv_ref, o_ref, k_s, v_s, qq_s, s_s, m_s, acc_s = refs
        past_refs = (pk_ref, pv_ref)
    else:
        lam_ref, subg_ref, qt_ref, kt_ref, vt_ref, o_ref, k_s, v_s, qq_s, s_s, m_s, acc_s = refs
        past_refs = None
    qi = pl.program_id(2)

    @pl.when(qi == 0)
    def _():
        _load_kv_blocks(groups, tq, n_past, 0, kt_ref, vt_ref, past_refs, True, k_s, v_s)
        row = lax.broadcasted_iota(jnp.int32, (_ONES_ROWS, tq), 0)
        ones_rows = jnp.where(row == 0, 1.0, 0.0).astype(_BF16)
        for c in range(groups):
            for jb in range(v_s.shape[1]):
                v_s[c, jb, _LANES:, :] = ones_rows

    for c in range(groups):
        qq_s[c] = _stack_heads_t(qt_ref[0, c * _LANES:(c + 1) * _LANES, :])
    m_s[...] = jnp.full(m_s.shape, _NEG_INF, _F32)
    acc_s[...] = jnp.zeros(acc_s.shape, _F32)

    def scores(i):
        return [_dot(k_s[c, i], qq_s[c]) for c in range(groups)]

    def store(slot, blocks):
        for c, st in enumerate(blocks):
            s_s[slot, c] = st

    def update(slot, i, mask):
        for c in range(groups):
            st = s_s[slot, c]
            if mask is not None:
                st = jnp.where(mask, st, _NEG_INF)
            m_prev = m_s[c]
            m_next = jnp.maximum(m_prev, jnp.max(st, axis=0, keepdims=True))
            alpha = jnp.exp2(m_prev - m_next)
            pt = jnp.exp2(st - m_next)
            m_s[c] = m_next
            acc_s[c] = alpha * acc_s[c] + _dot(v_s[c, i], pt.astype(_BF16))

    def step(slot, i, nxt_i, mask=None):
        nxt = scores(nxt_i)
        update(slot, i, mask)
        store(1 - slot, nxt)

    last = n_past + qi
    kpos, qpos = _diag_offsets(tq)
    chunk_bits = _CHUNK.bit_length() - 1
    mask = (kpos >> chunk_bits) <= (qpos >> chunk_bits)
    if seq < tq:
        mask = mask & (kpos < seq)
    store(0, scores(last))
    step(0, last, 0, mask)

    def pair(j, carry):
        step(1, 2 * j, 2 * j + 1)
        step(0, 2 * j + 1, jnp.minimum(2 * j + 2, last))
        return carry

    lax.fori_loop(0, last // 2, pair, 0)

    @pl.when((last & 1) == 1)
    def _():
        update(1, last - 1, None)

    gain = _lane_tile(subg_ref[...], tq)
    for c in range(groups):
        acc = acc_s[c]
        out = acc[:_LANES] / acc[_LANES:_LANES + 1]
        o = out[:, :tq] - lam_ref[0:1, 0:1] * out[:, tq:]
        y = (_rms(o, axis=0) * gain) * (1.0 - lam_init)
        o_ref[:, c * _LANES:(c + 1) * _LANES] = y.T[:o_ref.shape[0]].astype(o_ref.dtype)


def _attn_b_kernel(*refs, tq, n_past, groups, seq):
    if n_past:
        qt_ref, kt_ref, vt_ref, pk_ref, pv_ref, o_ref, k_s, v_s, qq_s, s_s, u_s, c_s, acc_s = refs
        past_refs = (pk_ref, pv_ref)
    else:
        qt_ref, kt_ref, vt_ref, o_ref, k_s, v_s, qq_s, s_s, u_s, c_s, acc_s = refs
        past_refs = None
    qi = pl.program_id(2)

    @pl.when(qi == 0)
    def _():
        for c in range(groups):
            k_s[c, 0] = jnp.zeros(k_s.shape[2:], _BF16)
            v_s[c, 0] = jnp.zeros(v_s.shape[2:], _BF16)
        _load_kv_blocks(groups, tq, n_past, 1, kt_ref, vt_ref, past_refs, False, k_s, v_s)

    for c in range(groups):
        qq_s[c] = _stack_heads_t(qt_ref[0, c * _LANES:(c + 1) * _LANES, :])
    s_idx = lax.broadcasted_iota(jnp.int32, (tq, 2 * tq), 0)
    j_idx = lax.broadcasted_iota(jnp.int32, (tq, 2 * tq), 1) & (tq - 1)
    u_s[...] = jnp.where(j_idx >= s_idx, -1.0, 0.0).astype(_BF16)
    c_s[...] = jnp.zeros(c_s.shape, _F32)
    acc_s[...] = jnp.zeros(acc_s.shape, _F32)

    def logits(i):
        return [_dot(k_s[c, i], qq_s[c]) for c in range(groups)]

    def update(slot, i, mask):
        rests = []
        for c in range(groups):
            zt = s_s[slot, c]
            neg_log_rest = jnp.maximum(zt, 0.0) + jnp.log(1.0 + jnp.exp2(jnp.abs(zt) * (-_LOG2_E)))
            if mask is not None:
                neg_log_rest = jnp.where(mask, neg_log_rest, 0.0)
            hi, lo = _split(neg_log_rest)
            rests.append(_dot(u_s[...], jnp.concatenate([hi, lo], axis=0)))
        for c, rest_from_here in enumerate(rests):
            wt = jnp.exp(s_s[slot, c] + rest_from_here + c_s[c])
            if mask is not None:
                wt = jnp.where(mask, wt, 0.0)
            acc_s[c] = acc_s[c] + _dot(v_s[c, i], wt.astype(_BF16))
            c_s[c] = c_s[c] + rest_from_here[0:1, :]

    def carry_max():
        return functools.reduce(jnp.maximum, [jnp.max(c_s[c]) for c in range(groups)])

    top = n_past + qi + 1

    def store(slot, blocks):
        for c, zt in enumerate(blocks):
            s_s[slot, c] = zt

    def step(slot, i, nxt_i, mask=None):
        nxt = logits(nxt_i)
        update(slot, i, mask)
        store(1 - slot, nxt)

    kpos, qpos = _diag_offsets(tq)
    store(0, logits(top))
    step(0, top, top - 1, kpos < qpos)
    step(1, top - 1, jnp.maximum(top - 2, 0))

    def pair(state):
        i = state[0]
        step(0, i, i - 1)
        step(1, i - 1, jnp.maximum(i - 2, 0))
        return i - 2, carry_max()

    lax.while_loop(lambda s: (s[0] >= 1) & (s[1] > _EXP_ZERO_BELOW), pair, (top - 2, carry_max()))

    row = lax.broadcasted_iota(jnp.int32, (_LANES, tq), 0)
    for c in range(groups):
        acc = acc_s[c]
        out = jnp.where(row < _HEAD_DIM, acc[:, :tq], acc[:, tq:])
        o_ref[:, c * _LANES:(c + 1) * _LANES] = out.T[:o_ref.shape[0]].astype(o_ref.dtype)


def _attn_call(kind, qt, kt, vt, past, seq, extra_inputs, **kernel_kwargs):
    batch, d, seq_pad = qt.shape
    tq = min(_ATTN_TILE, seq_pad)
    assert seq_pad % tq == 0 and tq % _LANES == 0
    nq = seq_pad // tq
    groups = _ATTN_GROUPS
    gw = groups * _LANES
    assert d % gw == 0
    const = lambda a: pl.BlockSpec(a.shape, lambda b, g, i: (0,) * a.ndim)
    q_spec = pl.BlockSpec((1, gw, tq), lambda b, g, i: (b, g, i))
    kv_spec = pl.BlockSpec((1, gw, seq_pad), lambda b, g, i: (b, g, 0))
    inputs = list(extra_inputs) + [qt, kt, vt]
    in_specs = [const(a) for a in extra_inputs] + [q_spec, kv_spec, kv_spec]
    n_past = 0
    if past is not None:
        pkt, pv = past
        past_len = pkt.shape[2]
        assert past_len % tq == 0 and past_len % _CHUNK == 0
        n_past = past_len // tq
        inputs += [pkt, pv]
        in_specs.append(pl.BlockSpec((1, gw, past_len), lambda b, g, i: (b, g, 0)))
        if kind == "a":
            in_specs.append(pl.BlockSpec((past_len, gw), lambda b, g, i: (b, g)))
        else:
            in_specs.append(pl.BlockSpec((1, gw, past_len), lambda b, g, i: (b, g, 0)))
    n_blocks = n_past + nq + (1 if kind == "b" else 0)
    v_rows = _LANES + (_ONES_ROWS if kind == "a" else 0)
    scratch = [
        pltpu.VMEM((groups, n_blocks, tq, _LANES), _BF16),
        pltpu.VMEM((groups, n_blocks, v_rows, tq), _BF16),
        pltpu.VMEM((groups, _LANES, 2 * tq), _BF16),
        pltpu.VMEM((2, groups, tq, 2 * tq), _F32),
    ]
    row_state = pltpu.VMEM((groups, 1, 2 * tq), _F32)
    acc_state = pltpu.VMEM((groups, v_rows, 2 * tq), _F32)
    if kind == "a":
        body = functools.partial(_attn_a_kernel, tq=tq, n_past=n_past, groups=groups, seq=seq, **kernel_kwargs)
        scratch += [row_state, acc_state]
    else:
        body = functools.partial(_attn_b_kernel, tq=tq, n_past=n_past, groups=groups, seq=seq)
        scratch += [pltpu.VMEM((tq, 2 * tq), _BF16), row_state, acc_state]
    return pl.pallas_call(
        body,
        grid=(batch, d // gw, nq),
        in_specs=in_specs,
        out_specs=pl.BlockSpec((min(tq, seq), gw), lambda b, g, i: (b * nq + i, g)),
        out_shape=jax.ShapeDtypeStruct((batch * seq, d), _BF16),
        scratch_shapes=scratch,
        compiler_params=_compiler_params(("arbitrary", "arbitrary", "arbitrary")),
        name="attn_" + kind,
    )(*inputs)


def _router_gate(logits, n_experts, n_groups):
    lane_i = lax.broadcasted_iota(jnp.int32, logits.shape, 1)
    lane = lane_i.astype(_F32)
    group_of_lane = (lane_i >> (_EXP_PER_GROUP.bit_length() - 1)).astype(_F32)
    big = float(4 * _LANES)
    row_max = lambda v: jnp.max(v, axis=1, keepdims=True)
    row_min = lambda v: jnp.min(v, axis=1, keepdims=True)
    row_sum = lambda v: jnp.sum(v, axis=1, keepdims=True)

    is_group = (lane_i >= n_experts) & (lane_i < n_experts + n_groups)
    lg = jnp.where(is_group, logits, _NEG_INF)
    eg = jnp.where(is_group, jnp.exp(lg - row_max(lg)), 0.0)
    pg = eg / row_sum(eg)
    pg_sel = row_max(pg)
    g_sel = row_min(jnp.where(is_group & (pg == pg_sel), lane - n_experts, big))

    in_group = (lane_i < n_experts) & (group_of_lane == g_sel)
    le = jnp.where(in_group, logits, _NEG_INF)
    ee = jnp.where(in_group, jnp.exp(le - row_max(le)), 0.0)
    pe = ee / row_sum(ee)
    p1 = row_max(pe)
    i1 = row_min(jnp.where(in_group & (pe == p1), lane, big))
    rest = in_group & (lane != i1)
    p2 = row_max(jnp.where(rest, pe, -1.0))
    i2 = row_min(jnp.where(rest & (pe == p2), lane, big))
    total = p1 + p2
    gate = jnp.where(lane == i1, p1 / total * pg_sel, 0.0) + jnp.where(lane == i2, p2 / total * pg_sel, 0.0)
    return jnp.where(lane_i == n_experts, g_sel, gate)


def _outproj_kernel(mod_ref, x_ref, oa_ref, ob_ref, n1_ref, n2_ref, wg_ref, woa_ref, wob_ref, wout_ref,
                    wr_hi_ref, wr_lo_ref, br_ref, x1_ref, h2_ref, gate_ref, *, seq, n_batch, n_experts, n_groups):
    tm, d = x_ref.shape
    halves = [slice(0, tm // 2), slice(tm // 2, tm)] if tm >= 2 * _MXU_DIM else [slice(0, tm)]

    def rows(k, sl):
        r = _mod_rows(mod_ref, k, d, seq, n_batch)
        return r if r.shape[0] == 1 else r[sl]

    h2_halves = []
    for sl in halves:
        x = x_ref[sl, :]
        h = _adaln(x, n1_ref[...], rows(1, sl), rows(0, sl)).astype(_BF16)
        gate_a = _sigmoid(_dot(h, wg_ref[:, :d]))
        gate_b = _sigmoid(_dot(h, wg_ref[:, d:]))
        mix = gate_a * _dot(oa_ref[sl, :], woa_ref[...]) + gate_b * _dot(ob_ref[sl, :], wob_ref[...])
        x1 = x + rows(2, sl) * _dot(mix.astype(_BF16), wout_ref[...])
        x1_ref[sl, :] = x1
        h2 = _adaln(x1, n2_ref[...], rows(4, sl), rows(3, sl))
        h2_ref[sl, :] = h2.astype(_BF16)
        h2_halves.append(h2)
    for sl, h2 in zip(halves, h2_halves):
        h2_hi, h2_lo = _split(h2)
        logits = (_dot(h2_hi, wr_hi_ref[...]) + (_dot(h2_hi, wr_lo_ref[...]) + _dot(h2_lo, wr_hi_ref[...]))
                  + br_ref[...])
        gate_ref[sl, :] = _router_gate(logits, n_experts, n_groups)


def _outproj_call(x2, oa, ob, mod3, mod_off, seq, norm1_g, norm2_g, w_in_bf, w_oa_bf, w_ob_bf, w_out_bf,
                  w_router, b_router, n_experts, n_groups):
    n, d = x2.shape
    tm, n_batch, tiles_per_batch = _token_tiling(n, seq, _OUTPROJ_TILE)
    wr_hi, wr_lo = _split(w_router)
    row_spec = lambda: pl.BlockSpec((tm, d), lambda i: (i, 0))
    full = lambda a: pl.BlockSpec(a.shape, lambda i: (0,) * a.ndim, pipeline_mode=pl.Buffered(1))
    return pl.pallas_call(
        functools.partial(_outproj_kernel, seq=seq, n_batch=n_batch, n_experts=n_experts, n_groups=n_groups),
        grid=(n // tm,),
        in_specs=[
            _mod_spec(mod3.shape[2], n_batch, tiles_per_batch, mod_off),
            row_spec(), row_spec(), row_spec(),
            pl.BlockSpec((1, d), lambda i: (0, 0)), pl.BlockSpec((1, d), lambda i: (0, 0)),
            pl.BlockSpec((d, 2 * d), lambda i: (0, w_in_bf.shape[1] // (2 * d) - 1), pipeline_mode=pl.Buffered(1)),
            full(w_oa_bf), full(w_ob_bf), full(w_out_bf), full(wr_hi), full(wr_lo),
            pl.BlockSpec((1, _LANES), lambda i: (0, 0)),
        ],
        out_specs=[row_spec(), row_spec(), pl.BlockSpec((tm, _LANES), lambda i: (i, 0))],
        out_shape=[
            jax.ShapeDtypeStruct((n, d), _F32),
            jax.ShapeDtypeStruct((n, d), _BF16),
            jax.ShapeDtypeStruct((n, _LANES), _F32),
        ],
        compiler_params=_compiler_params(("arbitrary",)),
        name="outproj",
    )(mod3, x2, oa, ob, norm1_g.reshape(1, d), norm2_g.reshape(1, d), w_in_bf, w_oa_bf, w_ob_bf, w_out_bf,
      wr_hi, wr_lo, b_router)


def _split3(x):
    hi = x.astype(_BF16)
    r = x - hi.astype(_F32)
    mid = r.astype(_BF16)
    return hi, mid, (r - mid.astype(_F32)).astype(_BF16)


def _moe_kernel(mod_ref, x1_ref, h2_ref, gate_ref, w1_ref, w3_ref, w2_ref, y_ref,
                gpack_s, rank_col_s, rank_row_s, count_s, *, seq, n_batch, n_experts, n_groups):
    g = pl.program_id(1)
    tm, d = x1_ref.shape
    lane = lax.broadcasted_iota(jnp.int32, (tm, _LANES), 1)
    assert 3 * n_experts <= _LANES and n_experts & (n_experts - 1) == 0

    @pl.when(g == 0)
    def _():
        y_ref[...] = jnp.zeros(y_ref.shape, _F32)
        gate = gate_ref[...]
        hi, mid, lo = [p.astype(_F32) for p in _split3(jnp.where(lane < n_experts, gate, 0.0))]
        gpack_s[...] = (hi + pltpu.roll(mid, n_experts, 1) + pltpu.roll(lo, 2 * n_experts, 1)).astype(_BF16)
        group_id = gate[:, n_experts:n_experts + 1]
        member = jnp.where((lane < n_groups) & (lane.astype(_F32) == group_id), 1.0, 0.0)
        t_idx = lax.broadcasted_iota(jnp.int32, (tm, tm), 0)
        u_idx = lax.broadcasted_iota(jnp.int32, (tm, tm), 1)
        earlier = jnp.where(u_idx < t_idx, 1.0, 0.0).astype(_BF16)
        rank_col = jnp.where(member > 0.0, _dot(earlier, member.astype(_BF16)), -1.0)
        rank_col_s[...] = rank_col
        rank_row = rank_col.T
        for k in range(n_groups):
            rank_row_s[k] = jnp.broadcast_to(rank_row[k:k + 1, :], (_SUBLANES, tm))
        count_s[...] = jnp.sum(member, axis=0, keepdims=True)

    lane_row = lax.broadcasted_iota(jnp.int32, (1, _LANES), 1)
    count = jnp.sum(jnp.where(lane_row == g, count_s[...], 0.0)).astype(jnp.int32)
    rank_c = jnp.sum(jnp.where(lane == g, rank_col_s[...], 0.0), axis=1, keepdims=True)
    experts_per_group = w1_ref.shape[0]

    def run_chunk(first_rank, rows):
        base = first_rank.astype(_F32)
        rank_r = jnp.concatenate([rank_row_s[g]] * (rows // _SUBLANES), axis=0)
        slot_r = lax.broadcasted_iota(jnp.int32, (rows, tm), 0).astype(_F32)
        slot_c = lax.broadcasted_iota(jnp.int32, (tm, rows), 1).astype(_F32)
        lane_c = lax.broadcasted_iota(jnp.int32, (rows, _LANES), 1)
        pick = jnp.where(rank_r == slot_r + base, 1.0, 0.0).astype(_BF16)
        put = jnp.where(rank_c == slot_c + base, 1.0, 0.0).astype(_BF16)
        xc = _dot(pick, h2_ref[...]).astype(_BF16)
        gc = _dot(pick, gpack_s[...])
        acc = jnp.zeros((rows, d), _F32)
        for e in range(experts_per_group):
            mine = (lane_c & (n_experts - 1)) == g * experts_per_group + e
            ge = jnp.sum(jnp.where(mine, gc, 0.0), axis=1, keepdims=True)
            hid = _silu(_dot(xc, w1_ref[e])) * _dot(xc, w3_ref[e])
            acc = acc + _dot((hid * ge).astype(_BF16), w2_ref[e])
        y_ref[...] += _dot(put, acc.astype(_BF16))

    sizes = [s for s in _MOE_CHUNK_ROWS if s <= tm]
    biggest = sizes[-1]
    n_full = count // biggest

    def full_chunk(k, carry):
        run_chunk(k * biggest, biggest)
        return carry

    lax.fori_loop(0, n_full, full_chunk, 0)
    rest = count - n_full * biggest
    for smaller, rows in zip([0] + sizes[:-1], sizes):

        @pl.when((rest > smaller) & (rest <= rows))
        def _(rows=rows):
            run_chunk(n_full * biggest, rows)

    @pl.when(g == pl.num_programs(1) - 1)
    def _():
        y_ref[...] = x1_ref[...] + _mod_rows(mod_ref, 5, d, seq, n_batch) * y_ref[...]


def _moe_call(x1, h2, gate, mod3, mod_off, seq, w1_bf, w3_bf, w2_bf, n_groups):
    n, d = x1.shape
    tm, n_batch, tiles_per_batch = _token_tiling(n, seq, _MOE_TILE)
    n_experts, _, d_expert = w1_bf.shape
    eg = n_experts // n_groups
    assert eg == _EXP_PER_GROUP
    row_spec = lambda w, **kw: pl.BlockSpec((tm, w), lambda i, g: (i, 0), **kw)
    return pl.pallas_call(
        functools.partial(_moe_kernel, seq=seq, n_batch=n_batch, n_experts=n_experts, n_groups=n_groups),
        grid=(n // tm, n_groups),
        in_specs=[
            _mod_spec(mod3.shape[2], n_batch, tiles_per_batch, mod_off),
            row_spec(d, pipeline_mode=pl.Buffered(1)), row_spec(d), row_spec(_LANES),
            pl.BlockSpec((eg, d, d_expert), lambda i, g: (g, 0, 0)),
            pl.BlockSpec((eg, d, d_expert), lambda i, g: (g, 0, 0)),
            pl.BlockSpec((eg, d_expert, d), lambda i, g: (g, 0, 0)),
        ],
        out_specs=row_spec(d),
        out_shape=jax.ShapeDtypeStruct((n, d), _F32),
        scratch_shapes=[
            pltpu.VMEM((tm, _LANES), _BF16),
            pltpu.VMEM((tm, _LANES), _F32),
            pltpu.VMEM((n_groups, _SUBLANES, tm), _F32),
            pltpu.VMEM((1, _LANES), _F32),
        ],
        compiler_params=_compiler_params(("arbitrary", "arbitrary"), _MOE_VMEM_LIMIT_BYTES),
        name="moe",
    )(mod3, x1, h2, gate, w1_bf, w3_bf, w2_bf)


def _feature_major(c):
    b, t, h, dh = c.shape
    return jnp.transpose(c, (0, 2, 3, 1)).reshape(b, h * dh, t)


def _token_major(ct, head_dim):
    b, d, t = ct.shape
    return jnp.transpose(ct.reshape(b, d // head_dim, head_dim, t), (0, 3, 1, 2))


def _layer(x, mod3, mod_off, lam, lam_init, past, p):
    batch, seq, d = x.shape
    x2 = x.reshape(batch * seq, d)
    past_len = 0 if past is None else past[0].shape[1]
    pos = past_len + jnp.arange(seq)
    seq_pad = -(-seq // _LANES) * _LANES
    qa_t, ka_t, va, va_t, qs_t, ks_t, vs_t = _proj_call(
        x2, mod3, mod_off, batch, seq, seq_pad, pos, p["norm1_g"], p["w_in"], p["gq"], p["gk"])
    past_a = past_b = None
    if past is not None:
        past_a = (_feature_major(past[0]), past[1].reshape(batch * past_len, d))
        past_b = (_feature_major(past[2]), _feature_major(past[3]))
    oa = _attn_call("a", qa_t, ka_t, va_t, past_a, seq, [lam, p["subln_g"]], lam_init=lam_init)
    ob = _attn_call("b", qs_t, ks_t, vs_t, past_b, seq, [])
    if seq_pad != seq:
        ka_t, ks_t, vs_t = ka_t[:, :, :seq], ks_t[:, :, :seq], vs_t[:, :, :seq]
    x1, h2, gate = _outproj_call(x2, oa, ob, mod3, mod_off, seq, p["norm1_g"], p["norm2_g"], p["w_in"],
                                 p["w_oa"], p["w_ob"], p["w_out"], p["w_router"], p["b_router"],
                                 p["n_experts"], p["n_groups"])
    y = _moe_call(x1, h2, gate, mod3, mod_off, seq, p["w1"], p["w3"], p["w2"], p["n_groups"])
    new = (_token_major(ka_t, _HEAD_DIM), va.reshape(batch, seq, -1, 2 * _HEAD_DIM),
           _token_major(ks_t, _HEAD_DIM), _token_major(vs_t, _HEAD_DIM))
    return y.reshape(batch, seq, d), new


def kernel(x_prompt, x_sample, cache_a_k, cache_a_v, cache_b_k, cache_b_v, c_prompt, c_sample, norm1_g, norm2_g, w_ada, b_ada, w_in, a_qnorm_g, a_knorm_g, a_lam_q1, a_lam_k1, a_lam_q2, a_lam_k2, a_subln_g, w_oa, w_ob, w_out, w_rg, b_rg, w_re, b_re, w1, w3, w2):
    depth, d = norm1_g.shape
    n_groups, n_experts = w_rg.shape[2], w_re.shape[2]
    assert w_in.shape[2] == 8 * d and n_experts + n_groups <= _LANES
    batch_p = x_prompt.shape[0]
    xp, xs = x_prompt, x_sample
    c_all = jnp.concatenate([c_prompt, c_sample], axis=0)
    rows_p, rows_s = [], []
    for l in range(depth):
        lam_init = 0.8 - 0.6 * math.exp(-0.3 * l)
        lam_vecs = jnp.stack([a_lam_q1[l], a_lam_k1[l], a_lam_q2[l], a_lam_k2[l]])
        mod, lam = _ada_call(c_all, w_ada[l], b_ada[l], lam_vecs, lam_init)
        mod3 = mod.reshape(mod.shape[0], 1, mod.shape[1])
        pad = _LANES - n_experts - n_groups
        params = dict(
            norm1_g=norm1_g[l], norm2_g=norm2_g[l],
            w_in=w_in[l].astype(_BF16),
            gq=a_qnorm_g[l], gk=a_knorm_g[l],
            subln_g=jnp.broadcast_to(a_subln_g[l].reshape(-1, 1), (a_subln_g.shape[1], _LANES)),
            w_oa=w_oa[l].astype(_BF16), w_ob=w_ob[l].astype(_BF16), w_out=w_out[l].astype(_BF16),
            w_router=jnp.pad(jnp.concatenate([w_re[l], w_rg[l]], axis=1), ((0, 0), (0, pad))),
            b_router=jnp.pad(jnp.concatenate([b_re[l], b_rg[l]]), (0, pad)).reshape(1, _LANES),
            w1=w1[l].astype(_BF16), w3=w3[l].astype(_BF16), w2=w2[l].astype(_BF16),
            n_experts=n_experts, n_groups=n_groups,
        )
        xp, new_p = _layer(xp, mod3, 0, lam, lam_init, None, params)
        past = (cache_a_k[l], cache_a_v[l], cache_b_k[l], cache_b_v[l])
        xs, new_s = _layer(xs, mod3, batch_p, lam, lam_init, past, params)
        rows_p.append(new_p)
        rows_s.append(new_s)
    stack = lambda rows, k: jnp.stack([r[k] for r in rows])
    return (xp, xs, stack(rows_p, 0), stack(rows_p, 1), stack(rows_p, 2), stack(rows_p, 3),
            stack(rows_s, 0), stack(rows_s, 1), stack(rows_s, 2), stack(rows_s, 3))
```

```python
import functools
import math

import jax
import jax.numpy as jnp
from jax import lax
from jax.experimental import pallas as pl
from jax.experimental.pallas import tpu as pltpu

_F32 = jnp.float32
_BF16 = jnp.bfloat16

_LANES = 128
_SUBLANES = 8
_MXU_DIM = 256
_VMEM_LIMIT_BYTES = 48 * 1024 * 1024

_CHUNK = 64
_HEAD_DIM = 64
_ROPE_DIMS = _HEAD_DIM // 4
_ROPE_THETA = 500000.0
_EXP_PER_GROUP = 8
_EPS = 1e-6
_NEG_INF = -1e30
_Q_SCALE = 1.0 / math.sqrt(_HEAD_DIM)
_EXP_ZERO_BELOW = -104.0
_LOG2_E = math.log2(math.e)

_TOKEN_TILE = 512
_OUTPROJ_TILE = 512
_ATTN_TILE = _MXU_DIM
_ATTN_GROUPS = 4
_ONES_ROWS = 16
_MOE_TILE = 1024
_MOE_CHUNK_ROWS = (128, 256, 384)
_MOE_VMEM_LIMIT_BYTES = 56 * 1024 * 1024


def _dot(a, b):
    return jnp.dot(a, b, preferred_element_type=_F32)


def _split(x):
    hi = x.astype(_BF16)
    lo = (x - hi.astype(_F32)).astype(_BF16)
    return hi, lo


def _dot3(a, b):
    a_hi, a_lo = _split(a)
    b_hi, b_lo = _split(b)
    return _dot(a_hi, b_hi) + (_dot(a_hi, b_lo) + _dot(a_lo, b_hi))


def _silu(x):
    return x / (1.0 + jnp.exp(-x))


def _sigmoid(x):
    return 1.0 / (1.0 + jnp.exp(-x))


def _rms(x, axis=-1):
    return x * lax.rsqrt(jnp.mean(x * x, axis=axis, keepdims=True) + _EPS)


def _mod_rows(mod_ref, k, d, seq, n_batch):
    if n_batch == 1:
        return mod_ref[0, :, k * d:(k + 1) * d]
    rows = [jnp.broadcast_to(mod_ref[b, :, k * d:(k + 1) * d], (seq, d)) for b in range(n_batch)]
    return jnp.concatenate(rows, axis=0)


def _lane_tile(x, n):
    assert n % _LANES == 0
    return jnp.concatenate([x] * (n // _LANES), axis=1) if n > _LANES else x


def _compiler_params(semantics, vmem_limit_bytes=_VMEM_LIMIT_BYTES):
    return pltpu.CompilerParams(dimension_semantics=semantics, vmem_limit_bytes=vmem_limit_bytes)


def _ada_kernel(c_ref, w_ref, b_ref, lam_ref, mod_ref, lam_out_ref, *, lam_init):
    mod_ref[...] = _dot3(_silu(c_ref[...]), w_ref[...]) + b_ref[...]
    lv = lam_ref[...]
    s1 = jnp.sum(lv[0:1] * lv[1:2], axis=-1, keepdims=True)
    s2 = jnp.sum(lv[2:3] * lv[3:4], axis=-1, keepdims=True)
    lam = jnp.exp(s1) - jnp.exp(s2) + lam_init
    lam_out_ref[...] = jnp.broadcast_to(lam, lam_out_ref.shape)


def _ada_call(c_all, w_ada, b_ada, lam_vecs, lam_init):
    rows, d = c_all.shape
    cols = w_ada.shape[1]
    tn = d
    return pl.pallas_call(
        functools.partial(_ada_kernel, lam_init=lam_init),
        grid=(cols // tn,),
        in_specs=[
            pl.BlockSpec((rows, d), lambda j: (0, 0)),
            pl.BlockSpec((d, tn), lambda j: (0, j)),
            pl.BlockSpec((1, tn), lambda j: (0, j)),
            pl.BlockSpec(lam_vecs.shape, lambda j: (0, 0)),
        ],
        out_specs=[
            pl.BlockSpec((rows, tn), lambda j: (0, j)),
            pl.BlockSpec((_SUBLANES, _LANES), lambda j: (0, 0)),
        ],
        out_shape=[
            jax.ShapeDtypeStruct((rows, cols), _F32),
            jax.ShapeDtypeStruct((_SUBLANES, _LANES), _F32),
        ],
        compiler_params=_compiler_params(("arbitrary",)),
        name="ada",
    )(c_all, w_ada, b_ada.reshape(1, cols), lam_vecs)


def _adaln(x, gain, scale, shift):
    return (_rms(x) * gain) * (1.0 + scale) + shift


def _proj_kernel(mod_ref, x_ref, n1_ref, w_ref, gq_ref, gk_ref, cos_ref, sin_ref,
                 qa_ref, ka_ref, va_ref, vat_ref, qs_ref, ks_ref, vs_ref, h_s, acc_s, *, seq, n_batch):
    tm, d = x_ref.shape
    shift = _mod_rows(mod_ref, 0, d, seq, n_batch)
    scale = _mod_rows(mod_ref, 1, d, seq, n_batch)
    h_s[...] = _adaln(x_ref[...], n1_ref[...], scale, shift).astype(_BF16)

    def proj(section):
        return _dot(h_s[...], w_ref[:, section * d:(section + 1) * d])

    def transposed(section, acc):
        acc_s[section % 2] = acc
        return acc_s[section % 2].T

    def proj_t(section):
        return transposed(section, proj(section))

    def store_t(out_ref, val):
        if tm < seq:
            out_ref[0] = val.astype(out_ref.dtype)
        else:
            pad = out_ref.shape[2] - seq
            for b in range(n_batch):
                out_ref[b, :, :seq] = val[:, b * seq:(b + 1) * seq].astype(out_ref.dtype)
                if pad:
                    out_ref[b, :, seq:] = jnp.zeros((val.shape[0], pad), out_ref.dtype)

    def head_norm_rope(acc, gain_ref, out_ref, out_scale):
        gain = _lane_tile(gain_ref[...], tm)
        cos, sin = cos_ref[...], sin_ref[...]
        half = _ROPE_DIMS // 2
        parts = []
        for h in range(d // _HEAD_DIM):
            y = _rms(acc[h * _HEAD_DIM:(h + 1) * _HEAD_DIM], axis=0) * gain
            x1, x2 = y[:half], y[half:2 * half]
            parts += [x1 * cos - x2 * sin, x2 * cos + x1 * sin, y[2 * half:]]
        out = jnp.concatenate(parts, axis=0)
        store_t(out_ref, out if out_scale == 1.0 else out * out_scale)

    head_norm_rope(proj_t(0), gq_ref, qa_ref, _Q_SCALE * _LOG2_E)
    head_norm_rope(proj_t(1), gk_ref, ka_ref, 1.0)
    va = proj(2)
    va_ref[...] = va
    store_t(vat_ref, transposed(2, va))
    store_t(qs_ref, proj_t(3) * _Q_SCALE)
    store_t(ks_ref, proj_t(4))
    store_t(vs_ref, proj_t(5))


def _rope_tables_t(pos):
    half = _ROPE_DIMS // 2
    inv_freq = jnp.exp(-math.log(_ROPE_THETA) * 2.0 * jnp.arange(half, dtype=_F32) / _ROPE_DIMS)
    ang = inv_freq[:, None] * pos.astype(_F32)[None, :]
    return jnp.cos(ang), jnp.sin(ang)


def _token_tiling(n_tokens, seq, tile=_TOKEN_TILE):
    tm = min(tile, n_tokens)
    if tm >= seq:
        assert tm % seq == 0 and n_tokens % tm == 0
        return tm, tm // seq, 1
    assert seq % tm == 0
    return tm, 1, seq // tm


def _mod_spec(width, n_batch, tiles_per_batch, mod_off):
    assert mod_off % n_batch == 0
    first = mod_off // n_batch
    return pl.BlockSpec((n_batch, 1, width), lambda i, *_: (first + i // tiles_per_batch, 0, 0))


def _proj_call(x2, mod3, mod_off, batch, seq, seq_pad, pos, norm1_g, w_in_bf, gq, gk):
    n, d = x2.shape
    tm, n_batch, tiles_per_batch = _token_tiling(n, seq, min(_TOKEN_TILE, max(n // 2, seq)))
    assert seq_pad == seq or tm >= seq
    cos, sin = _rope_tables_t(pos)
    if n_batch > 1:
        cos, sin = jnp.tile(cos, (1, n_batch)), jnp.tile(sin, (1, n_batch))
    lanes_t = tm if tm < seq else seq_pad
    gain_t = lambda g: jnp.broadcast_to(g.reshape(_HEAD_DIM, 1), (_HEAD_DIM, _LANES))
    row_spec = pl.BlockSpec((tm, d), lambda i: (i, 0))
    t_spec = lambda: pl.BlockSpec((n_batch, d, lanes_t), lambda i: (i // tiles_per_batch, 0, i % tiles_per_batch))
    rope_spec = lambda: pl.BlockSpec((_ROPE_DIMS // 2, tm), lambda i: (0, i % tiles_per_batch))
    const = lambda shape: pl.BlockSpec(shape, lambda i: (0, 0))
    t_out = lambda dt: jax.ShapeDtypeStruct((batch, d, seq_pad), dt)
    return pl.pallas_call(
        functools.partial(_proj_kernel, seq=seq, n_batch=n_batch),
        grid=(n // tm,),
        in_specs=[
            _mod_spec(mod3.shape[2], n_batch, tiles_per_batch, mod_off),
            row_spec,
            const((1, d)),
            pl.BlockSpec((d, 6 * d), lambda i: (0, 0), pipeline_mode=pl.Buffered(1)),
            const((_HEAD_DIM, _LANES)), const((_HEAD_DIM, _LANES)),
            rope_spec(), rope_spec(),
        ],
        out_specs=[t_spec(), t_spec(), row_spec, t_spec(), t_spec(), t_spec(), t_spec()],
        out_shape=[t_out(_BF16), t_out(_F32), jax.ShapeDtypeStruct((n, d), _F32), t_out(_BF16),
                   t_out(_BF16), t_out(_F32), t_out(_F32)],
        scratch_shapes=[pltpu.VMEM((tm, d), _BF16), pltpu.VMEM((2, tm, d), _F32)],
        compiler_params=_compiler_params(("arbitrary",)),
        name="proj",
    )(mod3, x2, norm1_g.reshape(1, d), w_in_bf, gain_t(gq), gain_t(gk), cos, sin)


def _stack_heads_t(qt):
    row = lax.broadcasted_iota(jnp.int32, qt.shape, 0)
    zero = jnp.zeros_like(qt)
    return jnp.concatenate([jnp.where(row < _HEAD_DIM, qt, zero), jnp.where(row >= _HEAD_DIM, qt, zero)], axis=1)


def _diag_offsets(tq):
    assert tq & (tq - 1) == 0
    kpos = lax.broadcasted_iota(jnp.int32, (tq, 2 * tq), 0)
    qpos = lax.broadcasted_iota(jnp.int32, (tq, 2 * tq), 1) & (tq - 1)
    return kpos, qpos


def _load_kv_blocks(groups, tq, n_past, first, kt_ref, vt_ref, past_refs, past_values_token_major, k_s, v_s):
    g = _LANES
    for c in range(groups):
        rows = slice(c * g, (c + 1) * g)
        for jb in range(n_past):
            pk_ref, pv_ref = past_refs
            cols = slice(jb * tq, (jb + 1) * tq)
            k_s[c, first + jb] = pk_ref[0, rows, cols].T.astype(_BF16)
            if past_values_token_major:
                v_s[c, first + jb, :g, :] = pv_ref[cols, rows].T.astype(_BF16)
            else:
                v_s[c, first + jb, :g, :] = pv_ref[0, rows, cols].astype(_BF16)
        for jb in range(k_s.shape[1] - n_past - first):
            cols = slice(jb * tq, (jb + 1) * tq)
            k_s[c, first + n_past + jb] = kt_ref[0, rows, cols].T.astype(_BF16)
            v_s[c, first + n_past + jb, :g, :] = vt_ref[0, rows, cols].astype(_BF16)


def _attn_a_kernel(*refs, tq, n_past, groups, seq, lam_init):
    if n_past:
        lam_ref, subg_ref, qt_ref, kt_ref, vt_ref, pk_ref, pv_ref, o_ref, k_s, v_s, qq_s, s_s, m_s, acc_s = refs
        past_refs = (pk_ref, pv_ref)
    else:
        lam_ref, subg_ref, qt_ref, kt_ref, vt_ref, o_ref, k_s, v_s, qq_s, s_s, m_s, acc_s = refs
        past_refs = None
    qi = pl.program_id(2)

    @pl.when(qi == 0)
    def _():
        _load_kv_blocks(groups, tq, n_past, 0, kt_ref, vt_ref, past_refs, True, k_s, v_s)
        row = lax.broadcasted_iota(jnp.int32, (_ONES_ROWS, tq), 0)
        ones_rows = jnp.where(row == 0, 1.0, 0.0).astype(_BF16)
        for c in range(groups):
            for jb in range(v_s.shape[1]):
                v_s[c, jb, _LANES:, :] = ones_rows

    for c in range(groups):
        qq_s[c] = _stack_heads_t(qt_ref[0, c * _LANES:(c + 1) * _LANES, :])
    m_s[...] = jnp.full(m_s.shape, _NEG_INF, _F32)
    acc_s[...] = jnp.zeros(acc_s.shape, _F32)

    def scores(i):
        return [_dot(k_s[c, i], qq_s[c]) for c in range(groups)]

    def store(slot, blocks):
        for c, st in enumerate(blocks):
            s_s[slot, c] = st

    def update(slot, i, mask):
        for c in range(groups):
            st = s_s[slot, c]
            if mask is not None:
                st = jnp.where(mask, st, _NEG_INF)
            m_prev = m_s[c]
            m_next = jnp.maximum(m_prev, jnp.max(st, axis=0, keepdims=True))
            alpha = jnp.exp2(m_prev - m_next)
            pt = jnp.exp2(st - m_next)
            m_s[c] = m_next
            acc_s[c] = alpha * acc_s[c] + _dot(v_s[c, i], pt.astype(_BF16))

    def step(slot, i, nxt_i, mask=None):
        nxt = scores(nxt_i)
        update(slot, i, mask)
        store(1 - slot, nxt)

    last = n_past + qi
    kpos, qpos = _diag_offsets(tq)
    chunk_bits = _CHUNK.bit_length() - 1
    mask = (kpos >> chunk_bits) <= (qpos >> chunk_bits)
    if seq < tq:
        mask = mask & (kpos < seq)
    store(0, scores(last))
    step(0, last, 0, mask)

    def pair(j, carry):
        step(1, 2 * j, 2 * j + 1)
        step(0, 2 * j + 1, jnp.minimum(2 * j + 2, last))
        return carry

    lax.fori_loop(0, last // 2, pair, 0)

    @pl.when((last & 1) == 1)
    def _():
        update(1, last - 1, None)

    gain = _lane_tile(subg_ref[...], tq)
    for c in range(groups):
        acc = acc_s[c]
        out = acc[:_LANES] / acc[_LANES:_LANES + 1]
        o = out[:, :tq] - lam_ref[0:1, 0:1] * out[:, tq:]
        y = (_rms(o, axis=0) * gain) * (1.0 - lam_init)
        o_ref[:, c * _LANES:(c + 1) * _LANES] = y.T[:o_ref.shape[0]].astype(o_ref.dtype)


def _attn_b_kernel(*refs, tq, n_past, groups, seq):
    if n_past:
        qt_ref, kt_ref, vt_ref, pk_ref, pv_ref, o_ref, k_s, v_s, qq_s, s_s, u_s, c_s, acc_s = refs
        past_refs = (pk_ref, pv_ref)
    else:
        qt_ref, kt_ref, vt_ref, o_ref, k_s, v_s, qq_s, s_s, u_s, c_s, acc_s = refs
        past_refs = None
    qi = pl.program_id(2)

    @pl.when(qi == 0)
    def _():
        for c in range(groups):
            k_s[c, 0] = jnp.zeros(k_s.shape[2:], _BF16)
            v_s[c, 0] = jnp.zeros(v_s.shape[2:], _BF16)
        _load_kv_blocks(groups, tq, n_past, 1, kt_ref, vt_ref, past_refs, False, k_s, v_s)

    for c in range(groups):
        qq_s[c] = _stack_heads_t(qt_ref[0, c * _LANES:(c + 1) * _LANES, :])
    s_idx = lax.broadcasted_iota(jnp.int32, (tq, 2 * tq), 0)
    j_idx = lax.broadcasted_iota(jnp.int32, (tq, 2 * tq), 1) & (tq - 1)
    u_s[...] = jnp.where(j_idx >= s_idx, -1.0, 0.0).astype(_BF16)
    c_s[...] = jnp.zeros(c_s.shape, _F32)
    acc_s[...] = jnp.zeros(acc_s.shape, _F32)

    def logits(i):
        return [_dot(k_s[c, i], qq_s[c]) for c in range(groups)]

    def update(slot, i, mask):
        rests = []
        for c in range(groups):
            zt = s_s[slot, c]
            neg_log_rest = jnp.maximum(zt, 0.0) + jnp.log(1.0 + jnp.exp2(jnp.abs(zt) * (-_LOG2_E)))
            if mask is not None:
                neg_log_rest = jnp.where(mask, neg_log_rest, 0.0)
            hi, lo = _split(neg_log_rest)
            rests.append(_dot(u_s[...], jnp.concatenate([hi, lo], axis=0)))
        for c, rest_from_here in enumerate(rests):
            wt = jnp.exp(s_s[slot, c] + rest_from_here + c_s[c])
            if mask is not None:
                wt = jnp.where(mask, wt, 0.0)
            acc_s[c] = acc_s[c] + _dot(v_s[c, i], wt.astype(_BF16))
            c_s[c] = c_s[c] + rest_from_here[0:1, :]

    def carry_max():
        return functools.reduce(jnp.maximum, [jnp.max(c_s[c]) for c in range(groups)])

    top = n_past + qi + 1

    def store(slot, blocks):
        for c, zt in enumerate(blocks):
            s_s[slot, c] = zt

    def step(slot, i, nxt_i, mask=None):
        nxt = logits(nxt_i)
        update(slot, i, mask)
        store(1 - slot, nxt)

    kpos, qpos = _diag_offsets(tq)
    store(0, logits(top))
    step(0, top, top - 1, kpos < qpos)
    step(1, top - 1, jnp.maximum(top - 2, 0))

    def pair(state):
        i = state[0]
        step(0, i, i - 1)
        step(1, i - 1, jnp.maximum(i - 2, 0))
        return i - 2, carry_max()

    lax.while_loop(lambda s: (s[0] >= 1) & (s[1] > _EXP_ZERO_BELOW), pair, (top - 2, carry_max()))

    row = lax.broadcasted_iota(jnp.int32, (_LANES, tq), 0)
    for c in range(groups):
        acc = acc_s[c]
        out = jnp.where(row < _HEAD_DIM, acc[:, :tq], acc[:, tq:])
        o_ref[:, c * _LANES:(c + 1) * _LANES] = out.T[:o_ref.shape[0]].astype(o_ref.dtype)


def _attn_call(kind, qt, kt, vt, past, seq, extra_inputs, **kernel_kwargs):
    batch, d, seq_pad = qt.shape
    tq = min(_ATTN_TILE, seq_pad)
    assert seq_pad % tq == 0 and tq % _LANES == 0
    nq = seq_pad // tq
    groups = _ATTN_GROUPS
    gw = groups * _LANES
    assert d % gw == 0
    const = lambda a: pl.BlockSpec(a.shape, lambda b, g, i: (0,) * a.ndim)
    q_spec = pl.BlockSpec((1, gw, tq), lambda b, g, i: (b, g, i))
    kv_spec = pl.BlockSpec((1, gw, seq_pad), lambda b, g, i: (b, g, 0))
    inputs = list(extra_inputs) + [qt, kt, vt]
    in_specs = [const(a) for a in extra_inputs] + [q_spec, kv_spec, kv_spec]
    n_past = 0
    if past is not None:
        pkt, pv = past
        past_len = pkt.shape[2]
        assert past_len % tq == 0 and past_len % _CHUNK == 0
        n_past = past_len // tq
        inputs += [pkt, pv]
        in_specs.append(pl.BlockSpec((1, gw, past_len), lambda b, g, i: (b, g, 0)))
        if kind == "a":
            in_specs.append(pl.BlockSpec((past_len, gw), lambda b, g, i: (b, g)))
        else:
            in_specs.append(pl.BlockSpec((1, gw, past_len), lambda b, g, i: (b, g, 0)))
    n_blocks = n_past + nq + (1 if kind == "b" else 0)
    v_rows = _LANES + (_ONES_ROWS if kind == "a" else 0)
    scratch = [
        pltpu.VMEM((groups, n_blocks, tq, _LANES), _BF16),
        pltpu.VMEM((groups, n_blocks, v_rows, tq), _BF16),
        pltpu.VMEM((groups, _LANES, 2 * tq), _BF16),
        pltpu.VMEM((2, groups, tq, 2 * tq), _F32),
    ]
    row_state = pltpu.VMEM((groups, 1, 2 * tq), _F32)
    acc_state = pltpu.VMEM((groups, v_rows, 2 * tq), _F32)
    if kind == "a":
        body = functools.partial(_attn_a_kernel, tq=tq, n_past=n_past, groups=groups, seq=seq, **kernel_kwargs)
        scratch += [row_state, acc_state]
    else:
        body = functools.partial(_attn_b_kernel, tq=tq, n_past=n_past, groups=groups, seq=seq)
        scratch += [pltpu.VMEM((tq, 2 * tq), _BF16), row_state, acc_state]
    return pl.pallas_call(
        body,
        grid=(batch, d // gw, nq),
        in_specs=in_specs,
        out_specs=pl.BlockSpec((min(tq, seq), gw), lambda b, g, i: (b * nq + i, g)),
        out_shape=jax.ShapeDtypeStruct((batch * seq, d), _BF16),
        scratch_shapes=scratch,
        compiler_params=_compiler_params(("arbitrary", "arbitrary", "arbitrary")),
        name="attn_" + kind,
    )(*inputs)


def _router_gate(logits, n_experts, n_groups):
    lane_i = lax.broadcasted_iota(jnp.int32, logits.shape, 1)
    lane = lane_i.astype(_F32)
    group_of_lane = (lane_i >> (_EXP_PER_GROUP.bit_length() - 1)).astype(_F32)
    big = float(4 * _LANES)
    row_max = lambda v: jnp.max(v, axis=1, keepdims=True)
    row_min = lambda v: jnp.min(v, axis=1, keepdims=True)
    row_sum = lambda v: jnp.sum(v, axis=1, keepdims=True)

    is_group = (lane_i >= n_experts) & (lane_i < n_experts + n_groups)
    lg = jnp.where(is_group, logits, _NEG_INF)
    eg = jnp.where(is_group, jnp.exp(lg - row_max(lg)), 0.0)
    pg = eg / row_sum(eg)
    pg_sel = row_max(pg)
    g_sel = row_min(jnp.where(is_group & (pg == pg_sel), lane - n_experts, big))

    in_group = (lane_i < n_experts) & (group_of_lane == g_sel)
    le = jnp.where(in_group, logits, _NEG_INF)
    ee = jnp.where(in_group, jnp.exp(le - row_max(le)), 0.0)
    pe = ee / row_sum(ee)
    p1 = row_max(pe)
    i1 = row_min(jnp.where(in_group & (pe == p1), lane, big))
    rest = in_group & (lane != i1)
    p2 = row_max(jnp.where(rest, pe, -1.0))
    i2 = row_min(jnp.where(rest & (pe == p2), lane, big))
    total = p1 + p2
    gate = jnp.where(lane == i1, p1 / total * pg_sel, 0.0) + jnp.where(lane == i2, p2 / total * pg_sel, 0.0)
    return jnp.where(lane_i == n_experts, g_sel, gate)


def _outproj_kernel(mod_ref, x_ref, oa_ref, ob_ref, n1_ref, n2_ref, wg_ref, woa_ref, wob_ref, wout_ref,
                    wr_hi_ref, wr_lo_ref, br_ref, x1_ref, h2_ref, gate_ref, *, seq, n_batch, n_experts, n_groups):
    d = x_ref.shape[1]
    rows = lambda k: _mod_rows(mod_ref, k, d, seq, n_batch)
    x = x_ref[...]
    h = _adaln(x, n1_ref[...], rows(1), rows(0)).astype(_BF16)
    gate_a = _sigmoid(_dot(h, wg_ref[:, :d]))
    gate_b = _sigmoid(_dot(h, wg_ref[:, d:]))
    mix = gate_a * _dot(oa_ref[...], woa_ref[...]) + gate_b * _dot(ob_ref[...], wob_ref[...])
    x1 = x + rows(2) * _dot(mix.astype(_BF16), wout_ref[...])
    x1_ref[...] = x1
    h2 = _adaln(x1, n2_ref[...], rows(4), rows(3))
    h2_ref[...] = h2.astype(_BF16)
    h2_hi, h2_lo = _split(h2)
    logits = _dot(h2_hi, wr_hi_ref[...]) + (_dot(h2_hi, wr_lo_ref[...]) + _dot(h2_lo, wr_hi_ref[...])) + br_ref[...]
    gate_ref[...] = _router_gate(logits, n_experts, n_groups)


def _outproj_call(x2, oa, ob, mod3, mod_off, seq, norm1_g, norm2_g, w_in_bf, w_oa_bf, w_ob_bf, w_out_bf,
                  w_router, b_router, n_experts, n_groups):
    n, d = x2.shape
    tm, n_batch, tiles_per_batch = _token_tiling(n, seq, _OUTPROJ_TILE)
    wr_hi, wr_lo = _split(w_router)
    row_spec = lambda: pl.BlockSpec((tm, d), lambda i: (i, 0))
    full = lambda a: pl.BlockSpec(a.shape, lambda i: (0,) * a.ndim, pipeline_mode=pl.Buffered(1))
    return pl.pallas_call(
        functools.partial(_outproj_kernel, seq=seq, n_batch=n_batch, n_experts=n_experts, n_groups=n_groups),
        grid=(n // tm,),
        in_specs=[
            _mod_spec(mod3.shape[2], n_batch, tiles_per_batch, mod_off),
            row_spec(), row_spec(), row_spec(),
            pl.BlockSpec((1, d), lambda i: (0, 0)), pl.BlockSpec((1, d), lambda i: (0, 0)),
            pl.BlockSpec((d, 2 * d), lambda i: (0, w_in_bf.shape[1] // (2 * d) - 1), pipeline_mode=pl.Buffered(1)),
            full(w_oa_bf), full(w_ob_bf), full(w_out_bf), full(wr_hi), full(wr_lo),
            pl.BlockSpec((1, _LANES), lambda i: (0, 0)),
        ],
        out_specs=[row_spec(), row_spec(), pl.BlockSpec((tm, _LANES), lambda i: (i, 0))],
        out_shape=[
            jax.ShapeDtypeStruct((n, d), _F32),
            jax.ShapeDtypeStruct((n, d), _BF16),
            jax.ShapeDtypeStruct((n, _LANES), _F32),
        ],
        compiler_params=_compiler_params(("arbitrary",)),
        name="outproj",
    )(mod3, x2, oa, ob, norm1_g.reshape(1, d), norm2_g.reshape(1, d), w_in_bf, w_oa_bf, w_ob_bf, w_out_bf,
      wr_hi, wr_lo, b_router)


def _split3(x):
    hi = x.astype(_BF16)
    r = x - hi.astype(_F32)
    mid = r.astype(_BF16)
    return hi, mid, (r - mid.astype(_F32)).astype(_BF16)


def _moe_kernel(mod_ref, x1_ref, h2_ref, gate_ref, w1_ref, w3_ref, w2_ref, y_ref,
                gpack_s, rank_col_s, rank_row_s, count_s, *, seq, n_batch, n_experts, n_groups):
    g = pl.program_id(1)
    tm, d = x1_ref.shape
    lane = lax.broadcasted_iota(jnp.int32, (tm, _LANES), 1)
    assert 3 * n_experts <= _LANES and n_experts & (n_experts - 1) == 0

    @pl.when(g == 0)
    def _():
        y_ref[...] = jnp.zeros(y_ref.shape, _F32)
        gate = gate_ref[...]
        hi, mid, lo = [p.astype(_F32) for p in _split3(jnp.where(lane < n_experts, gate, 0.0))]
        gpack_s[...] = (hi + pltpu.roll(mid, n_experts, 1) + pltpu.roll(lo, 2 * n_experts, 1)).astype(_BF16)
        group_id = gate[:, n_experts:n_experts + 1]
        member = jnp.where((lane < n_groups) & (lane.astype(_F32) == group_id), 1.0, 0.0)
        t_idx = lax.broadcasted_iota(jnp.int32, (tm, tm), 0)
        u_idx = lax.broadcasted_iota(jnp.int32, (tm, tm), 1)
        earlier = jnp.where(u_idx < t_idx, 1.0, 0.0).astype(_BF16)
        rank_col = jnp.where(member > 0.0, _dot(earlier, member.astype(_BF16)), -1.0)
        rank_col_s[...] = rank_col
        rank_row = rank_col.T
        for k in range(n_groups):
            rank_row_s[k] = jnp.broadcast_to(rank_row[k:k + 1, :], (_SUBLANES, tm))
        count_s[...] = jnp.sum(member, axis=0, keepdims=True)

    lane_row = lax.broadcasted_iota(jnp.int32, (1, _LANES), 1)
    count = jnp.sum(jnp.where(lane_row == g, count_s[...], 0.0)).astype(jnp.int32)
    rank_c = jnp.sum(jnp.where(lane == g, rank_col_s[...], 0.0), axis=1, keepdims=True)
    experts_per_group = w1_ref.shape[0]

    def run_chunk(first_rank, rows):
        base = first_rank.astype(_F32)
        rank_r = jnp.concatenate([rank_row_s[g]] * (rows // _SUBLANES), axis=0)
        slot_r = lax.broadcasted_iota(jnp.int32, (rows, tm), 0).astype(_F32)
        slot_c = lax.broadcasted_iota(jnp.int32, (tm, rows), 1).astype(_F32)
        lane_c = lax.broadcasted_iota(jnp.int32, (rows, _LANES), 1)
        pick = jnp.where(rank_r == slot_r + base, 1.0, 0.0).astype(_BF16)
        put = jnp.where(rank_c == slot_c + base, 1.0, 0.0).astype(_BF16)
        xc = _dot(pick, h2_ref[...]).astype(_BF16)
        gc = _dot(pick, gpack_s[...])
        acc = jnp.zeros((rows, d), _F32)
        for e in range(experts_per_group):
            mine = (lane_c & (n_experts - 1)) == g * experts_per_group + e
            ge = jnp.sum(jnp.where(mine, gc, 0.0), axis=1, keepdims=True)
            hid = _silu(_dot(xc, w1_ref[e])) * _dot(xc, w3_ref[e])
            acc = acc + _dot((hid * ge).astype(_BF16), w2_ref[e])
        y_ref[...] += _dot(put, acc.astype(_BF16))

    sizes = [s for s in _MOE_CHUNK_ROWS if s <= tm]
    biggest = sizes[-1]
    n_full = count // biggest

    def full_chunk(k, carry):
        run_chunk(k * biggest, biggest)
        return carry

    lax.fori_loop(0, n_full, full_chunk, 0)
    rest = count - n_full * biggest
    for smaller, rows in zip([0] + sizes[:-1], sizes):

        @pl.when((rest > smaller) & (rest <= rows))
        def _(rows=rows):
            run_chunk(n_full * biggest, rows)

    @pl.when(g == pl.num_programs(1) - 1)
    def _():
        y_ref[...] = x1_ref[...] + _mod_rows(mod_ref, 5, d, seq, n_batch) * y_ref[...]


def _moe_call(x1, h2, gate, mod3, mod_off, seq, w1_bf, w3_bf, w2_bf, n_groups):
    n, d = x1.shape
    tm, n_batch, tiles_per_batch = _token_tiling(n, seq, _MOE_TILE)
    n_experts, _, d_expert = w1_bf.shape
    eg = n_experts // n_groups
    assert eg == _EXP_PER_GROUP
    row_spec = lambda w, **kw: pl.BlockSpec((tm, w), lambda i, g: (i, 0), **kw)
    return pl.pallas_call(
        functools.partial(_moe_kernel, seq=seq, n_batch=n_batch, n_experts=n_experts, n_groups=n_groups),
        grid=(n // tm, n_groups),
        in_specs=[
            _mod_spec(mod3.shape[2], n_batch, tiles_per_batch, mod_off),
            row_spec(d, pipeline_mode=pl.Buffered(1)), row_spec(d), row_spec(_LANES),
            pl.BlockSpec((eg, d, d_expert), lambda i, g: (g, 0, 0)),
            pl.BlockSpec((eg, d, d_expert), lambda i, g: (g, 0, 0)),
            pl.BlockSpec((eg, d_expert, d), lambda i, g: (g, 0, 0)),
        ],
        out_specs=row_spec(d),
        out_shape=jax.ShapeDtypeStruct((n, d), _F32),
        scratch_shapes=[
            pltpu.VMEM((tm, _LANES), _BF16),
            pltpu.VMEM((tm, _LANES), _F32),
            pltpu.VMEM((n_groups, _SUBLANES, tm), _F32),
            pltpu.VMEM((1, _LANES), _F32),
        ],
        compiler_params=_compiler_params(("arbitrary", "arbitrary"), _MOE_VMEM_LIMIT_BYTES),
        name="moe",
    )(mod3, x1, h2, gate, w1_bf, w3_bf, w2_bf)


def _feature_major(c):
    b, t, h, dh = c.shape
    return jnp.transpose(c, (0, 2, 3, 1)).reshape(b, h * dh, t)


def _token_major(ct, head_dim):
    b, d, t = ct.shape
    return jnp.transpose(ct.reshape(b, d // head_dim, head_dim, t), (0, 3, 1, 2))


def _layer(x, mod3, mod_off, lam, lam_init, past, p):
    batch, seq, d = x.shape
    x2 = x.reshape(batch * seq, d)
    past_len = 0 if past is None else past[0].shape[1]
    pos = past_len + jnp.arange(seq)
    seq_pad = -(-seq // _LANES) * _LANES
    qa_t, ka_t, va, va_t, qs_t, ks_t, vs_t = _proj_call(
        x2, mod3, mod_off, batch, seq, seq_pad, pos, p["norm1_g"], p["w_in"], p["gq"], p["gk"])
    past_a = past_b = None
    if past is not None:
        past_a = (_feature_major(past[0]), past[1].reshape(batch * past_len, d))
        past_b = (_feature_major(past[2]), _feature_major(past[3]))
    oa = _attn_call("a", qa_t, ka_t, va_t, past_a, seq, [lam, p["subln_g"]], lam_init=lam_init)
    ob = _attn_call("b", qs_t, ks_t, vs_t, past_b, seq, [])
    if seq_pad != seq:
        ka_t, ks_t, vs_t = ka_t[:, :, :seq], ks_t[:, :, :seq], vs_t[:, :, :seq]
    x1, h2, gate = _outproj_call(x2, oa, ob, mod3, mod_off, seq, p["norm1_g"], p["norm2_g"], p["w_in"],
                                 p["w_oa"], p["w_ob"], p["w_out"], p["w_router"], p["b_router"],
                                 p["n_experts"], p["n_groups"])
    y = _moe_call(x1, h2, gate, mod3, mod_off, seq, p["w1"], p["w3"], p["w2"], p["n_groups"])
    new = (_token_major(ka_t, _HEAD_DIM), va.reshape(batch, seq, -1, 2 * _HEAD_DIM),
           _token_major(ks_t, _HEAD_DIM), _token_major(vs_t, _HEAD_DIM))
    return y.reshape(batch, seq, d), new


def kernel(x_prompt, x_sample, cache_a_k, cache_a_v, cache_b_k, cache_b_v, c_prompt, c_sample, norm1_g, norm2_g, w_ada, b_ada, w_in, a_qnorm_g, a_knorm_g, a_lam_q1, a_lam_k1, a_lam_q2, a_lam_k2, a_subln_g, w_oa, w_ob, w_out, w_rg, b_rg, w_re, b_re, w1, w3, w2):
    depth, d = norm1_g.shape
    n_groups, n_experts = w_rg.shape[2], w_re.shape[2]
    assert w_in.shape[2] == 8 * d and n_experts + n_groups <= _LANES
    batch_p = x_prompt.shape[0]
    xp, xs = x_prompt, x_sample
    c_all = jnp.concatenate([c_prompt, c_sample], axis=0)
    rows_p, rows_s = [], []
    for l in range(depth):
        lam_init = 0.8 - 0.6 * math.exp(-0.3 * l)
        lam_vecs = jnp.stack([a_lam_q1[l], a_lam_k1[l], a_lam_q2[l], a_lam_k2[l]])
        mod, lam = _ada_call(c_all, w_ada[l], b_ada[l], lam_vecs, lam_init)
        mod3 = mod.reshape(mod.shape[0], 1, mod.shape[1])
        pad = _LANES - n_experts - n_groups
        params = dict(
            norm1_g=norm1_g[l], norm2_g=norm2_g[l],
            w_in=w_in[l].astype(_BF16),
            gq=a_qnorm_g[l], gk=a_knorm_g[l],
            subln_g=jnp.broadcast_to(a_subln_g[l].reshape(-1, 1), (a_subln_g.shape[1], _LANES)),
            w_oa=w_oa[l].astype(_BF16), w_ob=w_ob[l].astype(_BF16), w_out=w_out[l].astype(_BF16),
            w_router=jnp.pad(jnp.concatenate([w_re[l], w_rg[l]], axis=1), ((0, 0), (0, pad))),
            b_router=jnp.pad(jnp.concatenate([b_re[l], b_rg[l]]), (0, pad)).reshape(1, _LANES),
            w1=w1[l].astype(_BF16), w3=w3[l].astype(_BF16), w2=w2[l].astype(_BF16),
            n_experts=n_experts, n_groups=n_groups,
        )
        xp, new_p = _layer(xp, mod3, 0, lam, lam_init, None, params)
        past = (cache_a_k[l], cache_a_v[l], cache_b_k[l], cache_b_v[l])
        xs, new_s = _layer(xs, mod3, batch_p, lam, lam_init, past, params)
        rows_p.append(new_p)
        rows_s.append(new_s)
    stack = lambda rows, k: jnp.stack([r[k] for r in rows])
    return (xp, xs, stack(rows_p, 0), stack(rows_p, 1), stack(rows_p, 2), stack(rows_p, 3),
            stack(rows_s, 0), stack(rows_s, 1), stack(rows_s, 2), stack(rows_s, 3))
```

```python
import functools
import math

import jax
import jax.numpy as jnp
from jax import lax
from jax.experimental import pallas as pl
from jax.experimental.pallas import tpu as pltpu

_F32 = jnp.float32
_BF16 = jnp.bfloat16

_LANES = 128
_SUBLANES = 8
_MXU_DIM = 256
_VMEM_LIMIT_BYTES = 48 * 1024 * 1024

_CHUNK = 64
_HEAD_DIM = 64
_ROPE_DIMS = _HEAD_DIM // 4
_ROPE_THETA = 500000.0
_EXP_PER_GROUP = 8
_EPS = 1e-6
_NEG_INF = -1e30
_Q_SCALE = 1.0 / math.sqrt(_HEAD_DIM)
_EXP_ZERO_BELOW = -104.0
_LOG2_E = math.log2(math.e)

_TOKEN_TILE = 512
_OUTPROJ_TILE = 512
_ATTN_TILE = _MXU_DIM
_ATTN_GROUPS = 4
_ONES_ROWS = 16
_MOE_TILE = 1024
_MOE_CHUNK_ROWS = (128, 256, 384)
_MOE_VMEM_LIMIT_BYTES = 56 * 1024 * 1024


def _dot(a, b):
    return jnp.dot(a, b, preferred_element_type=_F32)


def _split(x):
    hi = x.astype(_BF16)
    lo = (x - hi.astype(_F32)).astype(_BF16)
    return hi, lo


def _dot3(a, b):
    a_hi, a_lo = _split(a)
    b_hi, b_lo = _split(b)
    return _dot(a_hi, b_hi) + (_dot(a_hi, b_lo) + _dot(a_lo, b_hi))


def _silu(x):
    return x / (1.0 + jnp.exp(-x))


def _sigmoid(x):
    return 1.0 / (1.0 + jnp.exp(-x))


def _rms(x, axis=-1):
    return x * lax.rsqrt(jnp.mean(x * x, axis=axis, keepdims=True) + _EPS)


def _mod_rows(mod_ref, k, d, seq, n_batch):
    if n_batch == 1:
        return mod_ref[0, :, k * d:(k + 1) * d]
    rows = [jnp.broadcast_to(mod_ref[b, :, k * d:(k + 1) * d], (seq, d)) for b in range(n_batch)]
    return jnp.concatenate(rows, axis=0)


def _lane_tile(x, n):
    assert n % _LANES == 0
    return jnp.concatenate([x] * (n // _LANES), axis=1) if n > _LANES else x


def _compiler_params(semantics, vmem_limit_bytes=_VMEM_LIMIT_BYTES):
    return pltpu.CompilerParams(dimension_semantics=semantics, vmem_limit_bytes=vmem_limit_bytes)


def _ada_kernel(c_ref, w_ref, b_ref, lam_ref, mod_ref, lam_out_ref, *, lam_init):
    mod_ref[...] = _dot3(_silu(c_ref[...]), w_ref[...]) + b_ref[...]
    lv = lam_ref[...]
    s1 = jnp.sum(lv[0:1] * lv[1:2], axis=-1, keepdims=True)
    s2 = jnp.sum(lv[2:3] * lv[3:4], axis=-1, keepdims=True)
    lam = jnp.exp(s1) - jnp.exp(s2) + lam_init
    lam_out_ref[...] = jnp.broadcast_to(lam, lam_out_ref.shape)


def _ada_call(c_all, w_ada, b_ada, lam_vecs, lam_init):
    rows, d = c_all.shape
    cols = w_ada.shape[1]
    tn = d
    return pl.pallas_call(
        functools.partial(_ada_kernel, lam_init=lam_init),
        grid=(cols // tn,),
        in_specs=[
            pl.BlockSpec((rows, d), lambda j: (0, 0)),
            pl.BlockSpec((d, tn), lambda j: (0, j)),
            pl.BlockSpec((1, tn), lambda j: (0, j)),
            pl.BlockSpec(lam_vecs.shape, lambda j: (0, 0)),
        ],
        out_specs=[
            pl.BlockSpec((rows, tn), lambda j: (0, j)),
            pl.BlockSpec((_SUBLANES, _LANES), lambda j: (0, 0)),
        ],
        out_shape=[
            jax.ShapeDtypeStruct((rows, cols), _F32),
            jax.ShapeDtypeStruct((_SUBLANES, _LANES), _F32),
        ],
        compiler_params=_compiler_params(("arbitrary",)),
        name="ada",
    )(c_all, w_ada, b_ada.reshape(1, cols), lam_vecs)


def _adaln(x, gain, scale, shift):
    return (_rms(x) * gain) * (1.0 + scale) + shift


def _proj_kernel(mod_ref, x_ref, n1_ref, w_ref, gq_ref, gk_ref, cos_ref, sin_ref,
                 qa_ref, ka_ref, va_ref, vat_ref, qs_ref, ks_ref, vs_ref, h_s, acc_s, *, seq, n_batch):
    tm, d = x_ref.shape
    shift = _mod_rows(mod_ref, 0, d, seq, n_batch)
    scale = _mod_rows(mod_ref, 1, d, seq, n_batch)
    h_s[...] = _adaln(x_ref[...], n1_ref[...], scale, shift).astype(_BF16)

    def proj(section):
        return _dot(h_s[...], w_ref[:, section * d:(section + 1) * d])

    def transposed(section, acc):
        acc_s[section % 2] = acc
        return acc_s[section % 2].T

    def proj_t(section):
        return transposed(section, proj(section))

    def store_t(out_ref, val):
        if tm < seq:
            out_ref[0] = val.astype(out_ref.dtype)
        else:
            pad = out_ref.shape[2] - seq
            for b in range(n_batch):
                out_ref[b, :, :seq] = val[:, b * seq:(b + 1) * seq].astype(out_ref.dtype)
                if pad:
                    out_ref[b, :, seq:] = jnp.zeros((val.shape[0], pad), out_ref.dtype)

    def head_norm_rope(acc, gain_ref, out_ref, out_scale):
        gain = _lane_tile(gain_ref[...], tm)
        cos, sin = cos_ref[...], sin_ref[...]
        half = _ROPE_DIMS // 2
        parts = []
        for h in range(d // _HEAD_DIM):
            y = _rms(acc[h * _HEAD_DIM:(h + 1) * _HEAD_DIM], axis=0) * gain
            x1, x2 = y[:half], y[half:2 * half]
            parts += [x1 * cos - x2 * sin, x2 * cos + x1 * sin, y[2 * half:]]
        out = jnp.concatenate(parts, axis=0)
        store_t(out_ref, out if out_scale == 1.0 else out * out_scale)

    head_norm_rope(proj_t(0), gq_ref, qa_ref, _Q_SCALE * _LOG2_E)
    head_norm_rope(proj_t(1), gk_ref, ka_ref, 1.0)
    va = proj(2)
    va_ref[...] = va
    store_t(vat_ref, transposed(2, va))
    store_t(qs_ref, proj_t(3) * _Q_SCALE)
    store_t(ks_ref, proj_t(4))
    store_t(vs_ref, proj_t(5))


def _rope_tables_t(pos):
    half = _ROPE_DIMS // 2
    inv_freq = jnp.exp(-math.log(_ROPE_THETA) * 2.0 * jnp.arange(half, dtype=_F32) / _ROPE_DIMS)
    ang = inv_freq[:, None] * pos.astype(_F32)[None, :]
    return jnp.cos(ang), jnp.sin(ang)


def _token_tiling(n_tokens, seq, tile=_TOKEN_TILE):
    tm = min(tile, n_tokens)
    if tm >= seq:
        assert tm % seq == 0 and n_tokens % tm == 0
        return tm, tm // seq, 1
    assert seq % tm == 0
    return tm, 1, seq // tm


def _mod_spec(width, n_batch, tiles_per_batch, mod_off):
    assert mod_off % n_batch == 0
    first = mod_off // n_batch
    return pl.BlockSpec((n_batch, 1, width), lambda i, *_: (first + i // tiles_per_batch, 0, 0))


def _proj_call(x2, mod3, mod_off, batch, seq, seq_pad, pos, norm1_g, w_in_bf, gq, gk):
    n, d = x2.shape
    tm, n_batch, tiles_per_batch = _token_tiling(n, seq, min(_TOKEN_TILE, max(n // 2, seq)))
    assert seq_pad == seq or tm >= seq
    cos, sin = _rope_tables_t(pos)
    if n_batch > 1:
        cos, sin = jnp.tile(cos, (1, n_batch)), jnp.tile(sin, (1, n_batch))
    lanes_t = tm if tm < seq else seq_pad
    gain_t = lambda g: jnp.broadcast_to(g.reshape(_HEAD_DIM, 1), (_HEAD_DIM, _LANES))
    row_spec = pl.BlockSpec((tm, d), lambda i: (i, 0))
    t_spec = lambda: pl.BlockSpec((n_batch, d, lanes_t), lambda i: (i // tiles_per_batch, 0, i % tiles_per_batch))
    rope_spec = lambda: pl.BlockSpec((_ROPE_DIMS // 2, tm), lambda i: (0, i % tiles_per_batch))
    const = lambda shape: pl.BlockSpec(shape, lambda i: (0, 0))
    t_out = lambda dt: jax.ShapeDtypeStruct((batch, d, seq_pad), dt)
    return pl.pallas_call(
        functools.partial(_proj_kernel, seq=seq, n_batch=n_batch),
        grid=(n // tm,),
        in_specs=[
            _mod_spec(mod3.shape[2], n_batch, tiles_per_batch, mod_off),
            row_spec,
            const((1, d)),
            pl.BlockSpec((d, 6 * d), lambda i: (0, 0), pipeline_mode=pl.Buffered(1)),
            const((_HEAD_DIM, _LANES)), const((_HEAD_DIM, _LANES)),
            rope_spec(), rope_spec(),
        ],
        out_specs=[t_spec(), t_spec(), row_spec, t_spec(), t_spec(), t_spec(), t_spec()],
        out_shape=[t_out(_BF16), t_out(_F32), jax.ShapeDtypeStruct((n, d), _F32), t_out(_BF16),
                   t_out(_BF16), t_out(_F32), t_out(_F32)],
        scratch_shapes=[pltpu.VMEM((tm, d), _BF16), pltpu.VMEM((2, tm, d), _F32)],
        compiler_params=_compiler_params(("arbitrary",)),
        name="proj",
    )(mod3, x2, norm1_g.reshape(1, d), w_in_bf, gain_t(gq), gain_t(gk), cos, sin)


def _stack_heads_t(qt):
    row = lax.broadcasted_iota(jnp.int32, qt.shape, 0)
    zero = jnp.zeros_like(qt)
    return jnp.concatenate([jnp.where(row < _HEAD_DIM, qt, zero), jnp.where(row >= _HEAD_DIM, qt, zero)], axis=1)


def _diag_offsets(tq):
    assert tq & (tq - 1) == 0
    kpos = lax.broadcasted_iota(jnp.int32, (tq, 2 * tq), 0)
    qpos = lax.broadcasted_iota(jnp.int32, (tq, 2 * tq), 1) & (tq - 1)
    return kpos, qpos


def _load_kv_blocks(groups, tq, n_past, first, kt_ref, vt_ref, past_refs, past_values_token_major, k_s, v_s):
    g = _LANES
    for c in range(groups):
        rows = slice(c * g, (c + 1) * g)
        for jb in range(n_past):
            pk_ref, pv_ref = past_refs
            cols = slice(jb * tq, (jb + 1) * tq)
            k_s[c, first + jb] = pk_ref[0, rows, cols].T.astype(_BF16)
            if past_values_token_major:
                v_s[c, first + jb, :g, :] = pv_ref[cols, rows].T.astype(_BF16)
            else:
                v_s[c, first + jb, :g, :] = pv_ref[0, rows, cols].astype(_BF16)
        for jb in range(k_s.shape[1] - n_past - first):
            cols = slice(jb * tq, (jb + 1) * tq)
            k_s[c, first + n_past + jb] = kt_ref[0, rows, cols].T.astype(_BF16)
            v_s[c, first + n_past + jb, :g, :] = vt_ref[0, rows, cols].astype(_BF16)


def _attn_a_kernel(*refs, tq, n_past, groups, seq, lam_init):
    if n_past:
        lam_ref, subg_ref, qt_ref, kt_ref, vt_ref, pk_ref, pv_ref, o_ref, k_s, v_s, qq_s, s_s, m_s, acc_s = refs
        past_refs = (pk_ref, pv_ref)
    else:
        lam_ref, subg_ref, qt_ref, kt_ref, vt_ref, o_ref, k_s, v_s, qq_s, s_s, m_s, acc_s = refs
        past_refs = None
    qi = pl.program_id(2)

    @pl.when(qi == 0)
    def _():
        _load_kv_blocks(groups, tq, n_past, 0, kt_ref, vt_ref, past_refs, True, k_s, v_s)
        row = lax.broadcasted_iota(jnp.int32, (_ONES_ROWS, tq), 0)
        ones_rows = jnp.where(row == 0, 1.0, 0.0).astype(_BF16)
        for c in range(groups):
            for jb in range(v_s.shape[1]):
                v_s[c, jb, _LANES:, :] = ones_rows

    for c in range(groups):
        qq_s[c] = _stack_heads_t(qt_ref[0, c * _LANES:(c + 1) * _LANES, :])
    m_s[...] = jnp.full(m_s.shape, _NEG_INF, _F32)
    acc_s[...] = jnp.zeros(acc_s.shape, _F32)

    def scores(i):
        return [_dot(k_s[c, i], qq_s[c]) for c in range(groups)]

    def store(slot, blocks):
        for c, st in enumerate(blocks):
            s_s[slot, c] = st

    def update(slot, i, mask):
        for c in range(groups):
            st = s_s[slot, c]
            if mask is not None:
                st = jnp.where(mask, st, _NEG_INF)
            m_prev = m_s[c]
            m_next = jnp.maximum(m_prev, jnp.max(st, axis=0, keepdims=True))
            alpha = jnp.exp2(m_prev - m_next)
            pt = jnp.exp2(st - m_next)
            m_s[c] = m_next
            acc_s[c] = alpha * acc_s[c] + _dot(v_s[c, i], pt.astype(_BF16))

    def step(slot, i, nxt_i, mask=None):
        nxt = scores(nxt_i)
        update(slot, i, mask)
        store(1 - slot, nxt)

    last = n_past + qi
    kpos, qpos = _diag_offsets(tq)
    chunk_bits = _CHUNK.bit_length() - 1
    mask = (kpos >> chunk_bits) <= (qpos >> chunk_bits)
    if seq < tq:
        mask = mask & (kpos < seq)
    store(0, scores(last))
    step(0, last, 0, mask)

    def pair(j, carry):
        step(1, 2 * j, 2 * j + 1)
        step(0, 2 * j + 1, jnp.minimum(2 * j + 2, last))
        return carry

    lax.fori_loop(0, last // 2, pair, 0)

    @pl.when((last & 1) == 1)
    def _():
        update(1, last - 1, None)

    gain = _lane_tile(subg_ref[...], tq)
    for c in range(groups):
        acc = acc_s[c]
        out = acc[:_LANES] / acc[_LANES:_LANES + 1]
        o = out[:, :tq] - lam_ref[0:1, 0:1] * out[:, tq:]
        y = (_rms(o, axis=0) * gain) * (1.0 - lam_init)
        o_ref[:, c * _LANES:(c + 1) * _LANES] = y.T[:o_ref.shape[0]].astype(o_ref.dtype)


def _attn_b_kernel(*refs, tq, n_past, groups, seq):
    if n_past:
        qt_ref, kt_ref, vt_ref, pk_ref, pv_ref, o_ref, k_s, v_s, qq_s, s_s, u_s, c_s, acc_s = refs
        past_refs = (pk_ref, pv_ref)
    else:
        qt_ref, kt_ref, vt_ref, o_ref, k_s, v_s, qq_s, s_s, u_s, c_s, acc_s = refs
        past_refs = None
    qi = pl.program_id(2)

    @pl.when(qi == 0)
    def _():
        for c in range(groups):
            k_s[c, 0] = jnp.zeros(k_s.shape[2:], _BF16)
            v_s[c, 0] = jnp.zeros(v_s.shape[2:], _BF16)
        _load_kv_blocks(groups, tq, n_past, 1, kt_ref, vt_ref, past_refs, False, k_s, v_s)

    for c in range(groups):
        qq_s[c] = _stack_heads_t(qt_ref[0, c * _LANES:(c + 1) * _LANES, :])
    s_idx = lax.broadcasted_iota(jnp.int32, (tq, 2 * tq), 0)
    j_idx = lax.broadcasted_iota(jnp.int32, (tq, 2 * tq), 1) & (tq - 1)
    u_s[...] = jnp.where(j_idx >= s_idx, -1.0, 0.0).astype(_BF16)
    c_s[...] = jnp.zeros(c_s.shape, _F32)
    acc_s[...] = jnp.zeros(acc_s.shape, _F32)

    def logits(i):
        return [_dot(k_s[c, i], qq_s[c]) for c in range(groups)]

    def update(slot, i, mask):
        rests = []
        for c in range(groups):
            zt = s_s[slot, c]
            neg_log_rest = jnp.maximum(zt, 0.0) + jnp.log(1.0 + jnp.exp2(jnp.abs(zt) * (-_LOG2_E)))
            if mask is not None:
                neg_log_rest = jnp.where(mask, neg_log_rest, 0.0)
            hi, lo = _split(neg_log_rest)
            rests.append(_dot(u_s[...], jnp.concatenate([hi, lo], axis=0)))
        for c, rest_from_here in enumerate(rests):
            wt = jnp.exp(s_s[slot, c] + rest_from_here + c_s[c])
            if mask is not None:
                wt = jnp.where(mask, wt, 0.0)
            acc_s[c] = acc_s[c] + _dot(v_s[c, i], wt.astype(_BF16))
            c_s[c] = c_s[c] + rest_from_here[0:1, :]

    def carry_max():
        return functools.reduce(jnp.maximum, [jnp.max(c_s[c]) for c in range(groups)])

    top = n_past + qi + 1

    def store(slot, blocks):
        for c, zt in enumerate(blocks):
            s_s[slot, c] = zt

    def step(slot, i, nxt_i, mask=None):
        nxt = logits(nxt_i)
        update(slot, i, mask)
        store(1 - slot, nxt)

    kpos, qpos = _diag_offsets(tq)
    store(0, logits(top))
    step(0, top, top - 1, kpos < qpos)
    step(1, top - 1, jnp.maximum(top - 2, 0))

    def pair(state):
        i = state[0]
        step(0, i, i - 1)
        step(1, i - 1, jnp.maximum(i - 2, 0))
        return i - 2, carry_max()

    lax.while_loop(lambda s: (s[0] >= 1) & (s[1] > _EXP_ZERO_BELOW), pair, (top - 2, carry_max()))

    row = lax.broadcasted_iota(jnp.int32, (_LANES, tq), 0)
    for c in range(groups):
        acc = acc_s[c]
        out = jnp.where(row < _HEAD_DIM, acc[:, :tq], acc[:, tq:])
        o_ref[:, c * _LANES:(c + 1) * _LANES] = out.T[:o_ref.shape[0]].astype(o_ref.dtype)


def _attn_call(kind, qt, kt, vt, past, seq, extra_inputs, **kernel_kwargs):
    batch, d, seq_pad = qt.shape
    tq = min(_ATTN_TILE, seq_pad)
    assert seq_pad % tq == 0 and tq % _LANES == 0
    nq = seq_pad // tq
    groups = _ATTN_GROUPS * (2 if kind == "a" and nq > 1 else 1)
    gw = groups * _LANES
    assert d % gw == 0
    const = lambda a: pl.BlockSpec(a.shape, lambda b, g, i: (0,) * a.ndim)
    q_spec = pl.BlockSpec((1, gw, tq), lambda b, g, i: (b, g, i))
    kv_spec = pl.BlockSpec((1, gw, seq_pad), lambda b, g, i: (b, g, 0))
    inputs = list(extra_inputs) + [qt, kt, vt]
    in_specs = [const(a) for a in extra_inputs] + [q_spec, kv_spec, kv_spec]
    n_past = 0
    if past is not None:
        pkt, pv = past
        past_len = pkt.shape[2]
        assert past_len % tq == 0 and past_len % _CHUNK == 0
        n_past = past_len // tq
        inputs += [pkt, pv]
        in_specs.append(pl.BlockSpec((1, gw, past_len), lambda b, g, i: (b, g, 0)))
        if kind == "a":
            in_specs.append(pl.BlockSpec((past_len, gw), lambda b, g, i: (b, g)))
        else:
            in_specs.append(pl.BlockSpec((1, gw, past_len), lambda b, g, i: (b, g, 0)))
    n_blocks = n_past + nq + (1 if kind == "b" else 0)
    v_rows = _LANES + (_ONES_ROWS if kind == "a" else 0)
    scratch = [
        pltpu.VMEM((groups, n_blocks, tq, _LANES), _BF16),
        pltpu.VMEM((groups, n_blocks, v_rows, tq), _BF16),
        pltpu.VMEM((groups, _LANES, 2 * tq), _BF16),
        pltpu.VMEM((2, groups, tq, 2 * tq), _F32),
    ]
    row_state = pltpu.VMEM((groups, 1, 2 * tq), _F32)
    acc_state = pltpu.VMEM((groups, v_rows, 2 * tq), _F32)
    if kind == "a":
        body = functools.partial(_attn_a_kernel, tq=tq, n_past=n_past, groups=groups, seq=seq, **kernel_kwargs)
        scratch += [row_state, acc_state]
    else:
        body = functools.partial(_attn_b_kernel, tq=tq, n_past=n_past, groups=groups, seq=seq)
        scratch += [pltpu.VMEM((tq, 2 * tq), _BF16), row_state, acc_state]
    return pl.pallas_call(
        body,
        grid=(batch, d // gw, nq),
        in_specs=in_specs,
        out_specs=pl.BlockSpec((min(tq, seq), gw), lambda b, g, i: (b * nq + i, g)),
        out_shape=jax.ShapeDtypeStruct((batch * seq, d), _BF16),
        scratch_shapes=scratch,
        compiler_params=_compiler_params(("arbitrary", "arbitrary", "arbitrary"), _MOE_VMEM_LIMIT_BYTES),
        name="attn_" + kind,
    )(*inputs)


def _router_gate(logits, n_experts, n_groups):
    lane_i = lax.broadcasted_iota(jnp.int32, logits.shape, 1)
    lane = lane_i.astype(_F32)
    group_of_lane = (lane_i >> (_EXP_PER_GROUP.bit_length() - 1)).astype(_F32)
    big = float(4 * _LANES)
    row_max = lambda v: jnp.max(v, axis=1, keepdims=True)
    row_min = lambda v: jnp.min(v, axis=1, keepdims=True)
    row_sum = lambda v: jnp.sum(v, axis=1, keepdims=True)

    is_group = (lane_i >= n_experts) & (lane_i < n_experts + n_groups)
    lg = jnp.where(is_group, logits, _NEG_INF)
    eg = jnp.where(is_group, jnp.exp(lg - row_max(lg)), 0.0)
    pg = eg / row_sum(eg)
    pg_sel = row_max(pg)
    g_sel = row_min(jnp.where(is_group & (pg == pg_sel), lane - n_experts, big))

    in_group = (lane_i < n_experts) & (group_of_lane == g_sel)
    le = jnp.where(in_group, logits, _NEG_INF)
    ee = jnp.where(in_group, jnp.exp(le - row_max(le)), 0.0)
    pe = ee / row_sum(ee)
    p1 = row_max(pe)
    i1 = row_min(jnp.where(in_group & (pe == p1), lane, big))
    rest = in_group & (lane != i1)
    p2 = row_max(jnp.where(rest, pe, -1.0))
    i2 = row_min(jnp.where(rest & (pe == p2), lane, big))
    total = p1 + p2
    gate = jnp.where(lane == i1, p1 / total * pg_sel, 0.0) + jnp.where(lane == i2, p2 / total * pg_sel, 0.0)
    return jnp.where(lane_i == n_experts, g_sel, gate)


def _outproj_kernel(mod_ref, x_ref, oa_ref, ob_ref, n1_ref, n2_ref, wg_ref, woa_ref, wob_ref, wout_ref,
                    wr_hi_ref, wr_lo_ref, br_ref, x1_ref, h2_ref, gate_ref, *, seq, n_batch, n_experts, n_groups):
    d = x_ref.shape[1]
    rows = lambda k: _mod_rows(mod_ref, k, d, seq, n_batch)
    x = x_ref[...]
    h = _adaln(x, n1_ref[...], rows(1), rows(0)).astype(_BF16)
    gate_a = _sigmoid(_dot(h, wg_ref[:, :d]))
    gate_b = _sigmoid(_dot(h, wg_ref[:, d:]))
    mix = gate_a * _dot(oa_ref[...], woa_ref[...]) + gate_b * _dot(ob_ref[...], wob_ref[...])
    x1 = x + rows(2) * _dot(mix.astype(_BF16), wout_ref[...])
    x1_ref[...] = x1
    h2 = _adaln(x1, n2_ref[...], rows(4), rows(3))
    h2_ref[...] = h2.astype(_BF16)
    h2_hi, h2_lo = _split(h2)
    logits = _dot(h2_hi, wr_hi_ref[...]) + (_dot(h2_hi, wr_lo_ref[...]) + _dot(h2_lo, wr_hi_ref[...])) + br_ref[...]
    gate_ref[...] = _router_gate(logits, n_experts, n_groups)


def _outproj_call(x2, oa, ob, mod3, mod_off, seq, norm1_g, norm2_g, w_in_bf, w_oa_bf, w_ob_bf, w_out_bf,
                  w_router, b_router, n_experts, n_groups):
    n, d = x2.shape
    tm, n_batch, tiles_per_batch = _token_tiling(n, seq, _OUTPROJ_TILE)
    wr_hi, wr_lo = _split(w_router)
    row_spec = lambda: pl.BlockSpec((tm, d), lambda i: (i, 0))
    full = lambda a: pl.BlockSpec(a.shape, lambda i: (0,) * a.ndim, pipeline_mode=pl.Buffered(1))
    return pl.pallas_call(
        functools.partial(_outproj_kernel, seq=seq, n_batch=n_batch, n_experts=n_experts, n_groups=n_groups),
        grid=(n // tm,),
        in_specs=[
            _mod_spec(mod3.shape[2], n_batch, tiles_per_batch, mod_off),
            row_spec(), row_spec(), row_spec(),
            pl.BlockSpec((1, d), lambda i: (0, 0)), pl.BlockSpec((1, d), lambda i: (0, 0)),
            pl.BlockSpec((d, 2 * d), lambda i: (0, w_in_bf.shape[1] // (2 * d) - 1), pipeline_mode=pl.Buffered(1)),
            full(w_oa_bf), full(w_ob_bf), full(w_out_bf), full(wr_hi), full(wr_lo),
            pl.BlockSpec((1, _LANES), lambda i: (0, 0)),
        ],
        out_specs=[row_spec(), row_spec(), pl.BlockSpec((tm, _LANES), lambda i: (i, 0))],
        out_shape=[
            jax.ShapeDtypeStruct((n, d), _F32),
            jax.ShapeDtypeStruct((n, d), _BF16),
            jax.ShapeDtypeStruct((n, _LANES), _F32),
        ],
        compiler_params=_compiler_params(("arbitrary",)),
        name="outproj",
    )(mod3, x2, oa, ob, norm1_g.reshape(1, d), norm2_g.reshape(1, d), w_in_bf, w_oa_bf, w_ob_bf, w_out_bf,
      wr_hi, wr_lo, b_router)


def _split3(x):
    hi = x.astype(_BF16)
    r = x - hi.astype(_F32)
    mid = r.astype(_BF16)
    return hi, mid, (r - mid.astype(_F32)).astype(_BF16)


def _moe_kernel(mod_ref, x1_ref, h2_ref, gate_ref, w1_ref, w3_ref, w2_ref, y_ref,
                gpack_s, rank_col_s, rank_row_s, count_s, *, seq, n_batch, n_experts, n_groups):
    g = pl.program_id(1)
    tm, d = x1_ref.shape
    lane = lax.broadcasted_iota(jnp.int32, (tm, _LANES), 1)
    assert 3 * n_experts <= _LANES and n_experts & (n_experts - 1) == 0

    @pl.when(g == 0)
    def _():
        y_ref[...] = jnp.zeros(y_ref.shape, _F32)
        gate = gate_ref[...]
        hi, mid, lo = [p.astype(_F32) for p in _split3(jnp.where(lane < n_experts, gate, 0.0))]
        gpack_s[...] = (hi + pltpu.roll(mid, n_experts, 1) + pltpu.roll(lo, 2 * n_experts, 1)).astype(_BF16)
        group_id = gate[:, n_experts:n_experts + 1]
        member = jnp.where((lane < n_groups) & (lane.astype(_F32) == group_id), 1.0, 0.0)
        t_idx = lax.broadcasted_iota(jnp.int32, (tm, tm), 0)
        u_idx = lax.broadcasted_iota(jnp.int32, (tm, tm), 1)
        earlier = jnp.where(u_idx < t_idx, 1.0, 0.0).astype(_BF16)
        rank_col = jnp.where(member > 0.0, _dot(earlier, member.astype(_BF16)), -1.0)
        rank_col_s[...] = rank_col
        rank_row = rank_col.T
        for k in range(n_groups):
            rank_row_s[k] = jnp.broadcast_to(rank_row[k:k + 1, :], (_SUBLANES, tm))
        count_s[...] = jnp.sum(member, axis=0, keepdims=True)

    lane_row = lax.broadcasted_iota(jnp.int32, (1, _LANES), 1)
    count = jnp.sum(jnp.where(lane_row == g, count_s[...], 0.0)).astype(jnp.int32)
    rank_c = jnp.sum(jnp.where(lane == g, rank_col_s[...], 0.0), axis=1, keepdims=True)
    experts_per_group = w1_ref.shape[0]

    def run_chunk(first_rank, rows):
        base = first_rank.astype(_F32)
        rank_r = jnp.concatenate([rank_row_s[g]] * (rows // _SUBLANES), axis=0)
        slot_r = lax.broadcasted_iota(jnp.int32, (rows, tm), 0).astype(_F32)
        slot_c = lax.broadcasted_iota(jnp.int32, (tm, rows), 1).astype(_F32)
        lane_c = lax.broadcasted_iota(jnp.int32, (rows, _LANES), 1)
        pick = jnp.where(rank_r == slot_r + base, 1.0, 0.0).astype(_BF16)
        put = jnp.where(rank_c == slot_c + base, 1.0, 0.0).astype(_BF16)
        xc = _dot(pick, h2_ref[...]).astype(_BF16)
        gc = _dot(pick, gpack_s[...])
        acc = jnp.zeros((rows, d), _F32)
        for e in range(experts_per_group):
            mine = (lane_c & (n_experts - 1)) == g * experts_per_group + e
            ge = jnp.sum(jnp.where(mine, gc, 0.0), axis=1, keepdims=True)
            hid = _silu(_dot(xc, w1_ref[e])) * _dot(xc, w3_ref[e])
            acc = acc + _dot((hid * ge).astype(_BF16), w2_ref[e])
        y_ref[...] += _dot(put, acc.astype(_BF16))

    sizes = [s for s in _MOE_CHUNK_ROWS if s <= tm]
    biggest = sizes[-1]
    n_full = count // biggest

    def full_chunk(k, carry):
        run_chunk(k * biggest, biggest)
        return carry

    lax.fori_loop(0, n_full, full_chunk, 0)
    rest = count - n_full * biggest
    for smaller, rows in zip([0] + sizes[:-1], sizes):

        @pl.when((rest > smaller) & (rest <= rows))
        def _(rows=rows):
            run_chunk(n_full * biggest, rows)

    @pl.when(g == pl.num_programs(1) - 1)
    def _():
        y_ref[...] = x1_ref[...] + _mod_rows(mod_ref, 5, d, seq, n_batch) * y_ref[...]


def _moe_call(x1, h2, gate, mod3, mod_off, seq, w1_bf, w3_bf, w2_bf, n_groups):
    n, d = x1.shape
    tm, n_batch, tiles_per_batch = _token_tiling(n, seq, _MOE_TILE)
    n_experts, _, d_expert = w1_bf.shape
    eg = n_experts // n_groups
    assert eg == _EXP_PER_GROUP
    row_spec = lambda w, **kw: pl.BlockSpec((tm, w), lambda i, g: (i, 0), **kw)
    return pl.pallas_call(
        functools.partial(_moe_kernel, seq=seq, n_batch=n_batch, n_experts=n_experts, n_groups=n_groups),
        grid=(n // tm, n_groups),
        in_specs=[
            _mod_spec(mod3.shape[2], n_batch, tiles_per_batch, mod_off),
            row_spec(d, pipeline_mode=pl.Buffered(1)), row_spec(d), row_spec(_LANES),
            pl.BlockSpec((eg, d, d_expert), lambda i, g: (g, 0, 0)),
            pl.BlockSpec((eg, d, d_expert), lambda i, g: (g, 0, 0)),
            pl.BlockSpec((eg, d_expert, d), lambda i, g: (g, 0, 0)),
        ],
        out_specs=row_spec(d),
        out_shape=jax.ShapeDtypeStruct((n, d), _F32),
        scratch_shapes=[
            pltpu.VMEM((tm, _LANES), _BF16),
            pltpu.VMEM((tm, _LANES), _F32),
            pltpu.VMEM((n_groups, _SUBLANES, tm), _F32),
            pltpu.VMEM((1, _LANES), _F32),
        ],
        compiler_params=_compiler_params(("arbitrary", "arbitrary"), _MOE_VMEM_LIMIT_BYTES),
        name="moe",
    )(mod3, x1, h2, gate, w1_bf, w3_bf, w2_bf)


def _feature_major(c):
    b, t, h, dh = c.shape
    return jnp.transpose(c, (0, 2, 3, 1)).reshape(b, h * dh, t)


def _token_major(ct, head_dim):
    b, d, t = ct.shape
    return jnp.transpose(ct.reshape(b, d // head_dim, head_dim, t), (0, 3, 1, 2))


def _layer(x, mod3, mod_off, lam, lam_init, past, p):
    batch, seq, d = x.shape
    x2 = x.reshape(batch * seq, d)
    past_len = 0 if past is None else past[0].shape[1]
    pos = past_len + jnp.arange(seq)
    seq_pad = -(-seq // _LANES) * _LANES
    qa_t, ka_t, va, va_t, qs_t, ks_t, vs_t = _proj_call(
        x2, mod3, mod_off, batch, seq, seq_pad, pos, p["norm1_g"], p["w_in"], p["gq"], p["gk"])
    past_a = past_b = None
    if past is not None:
        past_a = (_feature_major(past[0]), past[1].reshape(batch * past_len, d))
        past_b = (_feature_major(past[2]), _feature_major(past[3]))
    oa = _attn_call("a", qa_t, ka_t, va_t, past_a, seq, [lam, p["subln_g"]], lam_init=lam_init)
    ob = _attn_call("b", qs_t, ks_t, vs_t, past_b, seq, [])
    if seq_pad != seq:
        ka_t, ks_t, vs_t = ka_t[:, :, :seq], ks_t[:, :, :seq], vs_t[:, :, :seq]
    x1, h2, gate = _outproj_call(x2, oa, ob, mod3, mod_off, seq, p["norm1_g"], p["norm2_g"], p["w_in"],
                                 p["w_oa"], p["w_ob"], p["w_out"], p["w_router"], p["b_router"],
                                 p["n_experts"], p["n_groups"])
    y = _moe_call(x1, h2, gate, mod3, mod_off, seq, p["w1"], p["w3"], p["w2"], p["n_groups"])
    new = (_token_major(ka_t, _HEAD_DIM), va.reshape(batch, seq, -1, 2 * _HEAD_DIM),
           _token_major(ks_t, _HEAD_DIM), _token_major(vs_t, _HEAD_DIM))
    return y.reshape(batch, seq, d), new


def kernel(x_prompt, x_sample, cache_a_k, cache_a_v, cache_b_k, cache_b_v, c_prompt, c_sample, norm1_g, norm2_g, w_ada, b_ada, w_in, a_qnorm_g, a_knorm_g, a_lam_q1, a_lam_k1, a_lam_q2, a_lam_k2, a_subln_g, w_oa, w_ob, w_out, w_rg, b_rg, w_re, b_re, w1, w3, w2):
    depth, d = norm1_g.shape
    n_groups, n_experts = w_rg.shape[2], w_re.shape[2]
    assert w_in.shape[2] == 8 * d and n_experts + n_groups <= _LANES
    batch_p = x_prompt.shape[0]
    xp, xs = x_prompt, x_sample
    c_all = jnp.concatenate([c_prompt, c_sample], axis=0)
    rows_p, rows_s = [], []
    for l in range(depth):
        lam_init = 0.8 - 0.6 * math.exp(-0.3 * l)
        lam_vecs = jnp.stack([a_lam_q1[l], a_lam_k1[l], a_lam_q2[l], a_lam_k2[l]])
        mod, lam = _ada_call(c_all, w_ada[l], b_ada[l], lam_vecs, lam_init)
        mod3 = mod.reshape(mod.shape[0], 1, mod.shape[1])
        pad = _LANES - n_experts - n_groups
        params = dict(
            norm1_g=norm1_g[l], norm2_g=norm2_g[l],
            w_in=w_in[l].astype(_BF16),
            gq=a_qnorm_g[l], gk=a_knorm_g[l],
            subln_g=jnp.broadcast_to(a_subln_g[l].reshape(-1, 1), (a_subln_g.shape[1], _LANES)),
            w_oa=w_oa[l].astype(_BF16), w_ob=w_ob[l].astype(_BF16), w_out=w_out[l].astype(_BF16),
            w_router=jnp.pad(jnp.concatenate([w_re[l], w_rg[l]], axis=1), ((0, 0), (0, pad))),
            b_router=jnp.pad(jnp.concatenate([b_re[l], b_rg[l]]), (0, pad)).reshape(1, _LANES),
            w1=w1[l].astype(_BF16), w3=w3[l].astype(_BF16), w2=w2[l].astype(_BF16),
            n_experts=n_experts, n_groups=n_groups,
        )
        xp, new_p = _layer(xp, mod3, 0, lam, lam_init, None, params)
        past = (cache_a_k[l], cache_a_v[l], cache_b_k[l], cache_b_v[l])
        xs, new_s = _layer(xs, mod3, batch_p, lam, lam_init, past, params)
        rows_p.append(new_p)
        rows_s.append(new_s)
    stack = lambda rows, k: jnp.stack([r[k] for r in rows])
    return (xp, xs, stack(rows_p, 0), stack(rows_p, 1), stack(rows_p, 2), stack(rows_p, 3),
            stack(rows_s, 0), stack(rows_s, 1), stack(rows_s, 2), stack(rows_s, 3))
```
